```python
import jax, jax.numpy as jnp
from jax import lax
import numpy as np

D_MODEL = 2048
BATCH = 8
SEQ = 2048
DEPTH = 1

GRID_W = 64
CTX_LEN = 256
D_MIX = D_MODEL
D_CONV = D_MIX // 2
CONV_GROUPS = 8
CONV_GROUP_DIM = D_CONV // CONV_GROUPS
CONV_WIDTH = 3
D_ATTN = D_MIX - D_CONV
N_HEADS = 8
HEAD_DIM = D_ATTN // N_HEADS
NA_ROWS = 8
NA_COLS = 16
ROPE_THETA = 10000.0
N_EXPERTS = 64
TOP_K = 8
N_EXPERT_GROUPS = 8
TOPK_GROUPS = 4
D_EXPERT = 512
D_SHARED = 512
ROUTED_SCALE = 2.5
EXPERT_CHUNK = 8
EPS = 1e-6
D_IN_PROJ = 3 * D_CONV + 3 * D_ATTN
IN_SPLITS = (D_CONV, 2 * D_CONV, 3 * D_CONV, 3 * D_CONV + D_ATTN, 3 * D_CONV + 2 * D_ATTN)
KV_OFFSET = 3 * D_CONV + D_ATTN

kernel_name = "hybrid_conv_natten_moe_dit_layer"


def _normalize(x):
    xf = x.astype(jnp.float32)
    return xf * lax.rsqrt(jnp.mean(xf * xf, axis=-1, keepdims=True) + EPS)


def rmsnorm(x, g):
    return (_normalize(x) * g.astype(jnp.float32)).astype(x.dtype)


def modulate(h, shift, scale):
    return h * (1 + scale) + shift


def axial_rope_tables(n_tokens):
    t = jnp.arange(n_tokens, dtype=jnp.int32)
    pos = jnp.stack([t // GRID_W, t % GRID_W], axis=-1).astype(jnp.float32)
    n_freq = HEAD_DIM // 4
    inv_freq = ROPE_THETA ** (-jnp.arange(n_freq, dtype=jnp.float32) / n_freq)
    ang = pos[:, :, None] * inv_freq
    return jnp.cos(ang), jnp.sin(ang)


def apply_rope(x, cos, sin):
    b, s, h, d = x.shape
    xr = x.reshape(b, s, h, 2, 2, d // 4)
    x1, x2 = xr[..., 0, :], xr[..., 1, :]
    c = cos[None, :, None].astype(x.dtype)
    sn = sin[None, :, None].astype(x.dtype)
    out = jnp.stack([x1 * c - x2 * sn, x2 * c + x1 * sn], axis=-2)
    return out.reshape(b, s, h, d)


def heads(t):
    b, l, _ = t.shape
    return t.reshape(b, l, N_HEADS, HEAD_DIM)


def short_conv(u, w, bias):
    up = jnp.pad(u, ((0, 0), (1, 1), (0, 0)))
    return up[:, :-2] * w[0] + up[:, 1:-1] * w[1] + up[:, 2:] * w[2] + bias


def gated_conv_mixer(bg, cg, u, w, bias):
    return bg * short_conv(cg * u, w, bias)


def neighbourhood_attention(q, k, v, kc, vc, rpb):
    b, s, h, d = q.shape
    rows = s // GRID_W
    kh = min(NA_ROWS, rows)
    kw = NA_COLS
    scale = HEAD_DIM ** -0.5
    qg = q.reshape(b, rows, GRID_W, h, d)
    kg = k.reshape(b, rows, GRID_W, h, d)
    vg = v.reshape(b, rows, GRID_W, h, d)
    row_start = jnp.clip(jnp.arange(rows) - kh // 2, 0, rows - kh)
    qcol = jnp.arange(GRID_W)
    col_idx = jnp.clip(qcol - kw // 2, 0, GRID_W - kw)[:, None] + jnp.arange(kw)
    col_off = col_idx - qcol[:, None] + (NA_COLS - 1)
    n_loc = kh * kw

    def one_row(r):
        r0 = row_start[r]
        kr = lax.dynamic_slice_in_dim(kg, r0, kh, axis=1)
        vr = lax.dynamic_slice_in_dim(vg, r0, kh, axis=1)
        k_sel = jnp.take(kr, col_idx, axis=2)
        v_sel = jnp.take(vr, col_idx, axis=2)
        qr = lax.dynamic_index_in_dim(qg, r, axis=1, keepdims=False)
        s_loc = jnp.einsum('bqhd,biqjhd->bhqij', qr, k_sel).astype(jnp.float32) * scale
        row_off = r0 + jnp.arange(kh) - r + (NA_ROWS - 1)
        bias = rpb[:, row_off[None, :, None], col_off[:, None, :]]
        s_loc = s_loc + bias[None].astype(jnp.float32)
        s_ctx = jnp.einsum('bqhd,bchd->bhqc', qr, kc).astype(jnp.float32) * scale
        logits = jnp.concatenate([s_loc.reshape(b, h, GRID_W, n_loc), s_ctx], axis=-1)
        p = jax.nn.softmax(logits, axis=-1).astype(v.dtype)
        p_loc = p[..., :n_loc].reshape(b, h, GRID_W, kh, kw)
        p_ctx = p[..., n_loc:]
        return (jnp.einsum('bhqij,biqjhd->bqhd', p_loc, v_sel)
                + jnp.einsum('bhqc,bchd->bqhd', p_ctx, vc))

    out = lax.map(one_row, jnp.arange(rows))
    return jnp.transpose(out, (1, 0, 2, 3, 4)).reshape(b, s, h, d)


def context_attention(q, k, v):
    scale = HEAD_DIM ** -0.5
    logits = jnp.einsum('bqhd,bkhd->bhqk', q, k).astype(jnp.float32) * scale
    p = jax.nn.softmax(logits, axis=-1).astype(v.dtype)
    return jnp.einsum('bhqk,bkhd->bqhd', p, v)


def merge_head_groups(conv_out, attn_out, g_conv, g_attn, w_out):
    b, l, _ = conv_out.shape
    yc = _normalize(conv_out.reshape(b, l, CONV_GROUPS, CONV_GROUP_DIM)).reshape(b, l, D_CONV)
    ya = _normalize(attn_out).reshape(b, l, D_ATTN)
    y = jnp.concatenate([yc * g_conv.astype(jnp.float32), ya * g_attn.astype(jnp.float32)], axis=-1)
    return y.astype(conv_out.dtype) @ w_out


def moe_ffn(h, w_router, router_bias, w_gate, w_up, w_down, ws_gate, ws_up, ws_down):
    b, s, d = h.shape
    t = h.reshape(b * s, d)
    n_tok = t.shape[0]
    scores = jax.nn.sigmoid((t @ w_router).astype(jnp.float32))
    sel = scores + router_bias.astype(jnp.float32)
    grp_scores = lax.top_k(sel.reshape(n_tok, N_EXPERT_GROUPS, N_EXPERTS // N_EXPERT_GROUPS), 2)[0].sum(-1)
    _, top_g = lax.top_k(grp_scores, TOPK_GROUPS)
    gmask = jnp.any(top_g[:, :, None] == jnp.arange(N_EXPERT_GROUPS), axis=1)
    emask = jnp.repeat(gmask, N_EXPERTS // N_EXPERT_GROUPS, axis=-1)
    _, idx = lax.top_k(jnp.where(emask, sel, -jnp.inf), TOP_K)
    wsel = jnp.take_along_axis(scores, idx, axis=-1)
    wsel = wsel / jnp.sum(wsel, axis=-1, keepdims=True) * ROUTED_SCALE
    gates = jnp.zeros((n_tok, N_EXPERTS), jnp.float32).at[jnp.arange(n_tok)[:, None], idx].set(wsel)
    gates = gates.astype(t.dtype)
    out = (jax.nn.silu(t @ ws_gate) * (t @ ws_up)) @ ws_down
    for e0 in range(0, N_EXPERTS, EXPERT_CHUNK):
        sl = slice(e0, e0 + EXPERT_CHUNK)
        hid = jax.nn.silu(jnp.einsum('td,edf->tef', t, w_gate[sl])) * jnp.einsum('td,edf->tef', t, w_up[sl])
        out = out + jnp.einsum('tef,efd->td', hid * gates[:, sl, None], w_down[sl])
    return out.reshape(b, s, d)


def setup_inputs(seed: int = 0) -> dict:
    key = jax.random.key(seed)
    ks = jax.random.split(key, 25)
    f32 = jnp.float32
    L = DEPTH

    def nrm(k, shape, scale):
        return jax.random.normal(k, shape, f32) * scale

    def gain(k, shape):
        return 1.0 + 0.01 * jax.random.normal(k, shape, f32)

    return {
        "x": nrm(ks[0], (BATCH, SEQ, D_MODEL), 1.0),
        "c": nrm(ks[1], (BATCH, D_MODEL), 1.0),
        "ctx": nrm(ks[2], (BATCH, CTX_LEN, D_MODEL), 1.0),
        "c_ctx": nrm(ks[3], (D_MODEL,), 1.0),
        "w_mod": nrm(ks[4], (L, D_MODEL, 6 * D_MODEL), 0.5 * D_MODEL ** -0.5),
        "b_mod": nrm(ks[5], (L, 6 * D_MODEL), 0.02),
        "g_pre_mix": gain(ks[6], (L, D_MODEL)),
        "g_post_mix": gain(ks[7], (L, D_MODEL)),
        "g_pre_ffn": gain(ks[8], (L, D_MODEL)),
        "g_post_ffn": gain(ks[9], (L, D_MODEL)),
        "w_in": nrm(ks[10], (L, D_MODEL, D_IN_PROJ), D_MODEL ** -0.5),
        "conv_w": nrm(ks[11], (L, CONV_WIDTH, D_CONV), CONV_WIDTH ** -0.5),
        "conv_b": nrm(ks[12], (L, D_CONV), 0.02),
        "rpb": nrm(ks[13], (L, N_HEADS, 2 * NA_ROWS - 1, 2 * NA_COLS - 1), 0.1),
        "g_conv_out": gain(ks[14], (L, D_CONV)),
        "g_attn_out": gain(ks[15], (L, D_ATTN)),
        "w_out": nrm(ks[16], (L, D_MIX, D_MODEL), D_MIX ** -0.5),
        "w_router": nrm(ks[17], (L, D_MODEL, N_EXPERTS), D_MODEL ** -0.5),
        "router_bias": nrm(ks[18], (L, N_EXPERTS), 0.01),
        "w_exp_gate": nrm(ks[19], (L, N_EXPERTS, D_MODEL, D_EXPERT), D_MODEL ** -0.5),
        "w_exp_up": nrm(ks[20], (L, N_EXPERTS, D_MODEL, D_EXPERT), D_MODEL ** -0.5),
        "w_exp_down": nrm(ks[21], (L, N_EXPERTS, D_EXPERT, D_MODEL), D_EXPERT ** -0.5),
        "w_sh_gate": nrm(ks[22], (L, D_MODEL, D_SHARED), D_MODEL ** -0.5),
        "w_sh_up": nrm(ks[23], (L, D_MODEL, D_SHARED), D_MODEL ** -0.5),
        "w_sh_down": nrm(ks[24], (L, D_SHARED, D_MODEL), D_SHARED ** -0.5),
    }


def reference(x, c, ctx, c_ctx, w_mod, b_mod, g_pre_mix, g_post_mix, g_pre_ffn, g_post_ffn,
              w_in, conv_w, conv_b, rpb, g_conv_out, g_attn_out, w_out, w_router, router_bias,
              w_exp_gate, w_exp_up, w_exp_down, w_sh_gate, w_sh_up, w_sh_down):
    seq = x.shape[1]
    cos, sin = axial_rope_tables(seq)
    xc = ctx
    for l in range(DEPTH):
        last = l == DEPTH - 1
        mod = jax.nn.silu(c) @ w_mod[l] + b_mod[l]
        mod_c = jax.nn.silu(c_ctx) @ w_mod[l] + b_mod[l]
        sh_a, sc_a, gt_a, sh_f, sc_f, gt_f = jnp.split(mod[:, None, :], 6, axis=-1)
        shc_a, scc_a, gtc_a, shc_f, scc_f, gtc_f = jnp.split(mod_c[None, None, :], 6, axis=-1)

        hl = modulate(rmsnorm(x, g_pre_mix[l]), sh_a, sc_a)
        hc = modulate(rmsnorm(xc, g_pre_mix[l]), shc_a, scc_a)
        ub, uc, ux, q, k, v = jnp.split(hl @ w_in[l], IN_SPLITS, axis=-1)
        if last:
            kc_, vc_ = jnp.split(hc @ w_in[l][:, KV_OFFSET:], 2, axis=-1)
        else:
            cb, cc, cx, qc_, kc_, vc_ = jnp.split(hc @ w_in[l], IN_SPLITS, axis=-1)
        kc, vc = heads(kc_), heads(vc_)
        attn_l = neighbourhood_attention(apply_rope(heads(q), cos, sin), apply_rope(heads(k), cos, sin),
                                         heads(v), kc, vc, rpb[l])
        conv_l = gated_conv_mixer(ub, uc, ux, conv_w[l], conv_b[l])
        y_l = merge_head_groups(conv_l, attn_l, g_conv_out[l], g_attn_out[l], w_out[l])
        x = x + gt_a * rmsnorm(y_l, g_post_mix[l])
        if not last:
            attn_c = context_attention(heads(qc_), kc, vc)
            conv_c = gated_conv_mixer(cb, cc, cx, conv_w[l], conv_b[l])
            y_c = merge_head_groups(conv_c, attn_c, g_conv_out[l], g_attn_out[l], w_out[l])
            xc = xc + gtc_a * rmsnorm(y_c, g_post_mix[l])

        ffn_args = (w_router[l], router_bias[l], w_exp_gate[l], w_exp_up[l], w_exp_down[l],
                    w_sh_gate[l], w_sh_up[l], w_sh_down[l])
        f_l = moe_ffn(modulate(rmsnorm(x, g_pre_ffn[l]), sh_f, sc_f), *ffn_args)
        x = x + gt_f * rmsnorm(f_l, g_post_ffn[l])
        if not last:
            f_c = moe_ffn(modulate(rmsnorm(xc, g_pre_ffn[l]), shc_f, scc_f), *ffn_args)
            xc = xc + gtc_f * rmsnorm(f_c, g_post_ffn[l])
    return x
```

```python
import functools

import numpy as np
import jax
import jax.numpy as jnp
from jax import lax
from jax.experimental import pallas as pl
from jax.experimental.pallas import tpu as pltpu

F32 = jnp.float32
BF16 = jnp.bfloat16

GRID_W = 64
N_HEADS = 8
HEAD_DIM = 128
NA_ROWS = 8
NA_COLS = 16
ROPE_THETA = 10000.0
CONV_GROUP_DIM = 128
N_EXPERTS = 64
N_EXPERT_GROUPS = 8
TOPK_GROUPS = 4
TOP_K = 8
ROUTED_SCALE = 2.5
EPS = 1e-6
MASK_VALUE = -1e30

Q_ROWS = 4
K_ROWS = 12

V7X_VMEM_BYTES = 64 * 1024 * 1024


def _vmem_limit(nbytes):
    return int(min(nbytes, V7X_VMEM_BYTES - 6 * 1024 * 1024))


def _rms(x):
    return x * lax.rsqrt(jnp.mean(x * x, axis=-1, keepdims=True) + EPS)


def _silu(x):
    return x * jax.nn.sigmoid(x)


def _dot(a, b):
    return jnp.dot(a, b, preferred_element_type=F32)


def _dot_nt(a, b, precision=None):
    return lax.dot_general(a, b, (((1,), (1,)), ((), ())), preferred_element_type=F32, precision=precision)


def _mod_kernel(c_ref, w_ref, b_ref, o_ref):
    a = _silu(c_ref[...]).astype(BF16)
    o_ref[...] = _dot(a, w_ref[...].astype(BF16)) + b_ref[...]


def _mod_call(cvec, w_mod, b_mod, tn=1024):
    rows, d = cvec.shape
    n = w_mod.shape[1]
    return pl.pallas_call(
        _mod_kernel,
        grid=(n // tn,),
        in_specs=[
            pl.BlockSpec((rows, d), lambda j: (0, 0)),
            pl.BlockSpec((d, tn), lambda j: (0, j)),
            pl.BlockSpec((1, tn), lambda j: (0, j)),
        ],
        out_specs=pl.BlockSpec((rows, tn), lambda j: (0, j)),
        out_shape=jax.ShapeDtypeStruct((rows, n), F32),
        compiler_params=pltpu.CompilerParams(
            dimension_semantics=("arbitrary",), vmem_limit_bytes=_vmem_limit(40 << 20)),
        name="mod",
    )(cvec, w_mod, b_mod)


def _inproj_kernel(x_ref, g_ref, sh_ref, sc_ref, w_ref, o_ref, h_ref):
    @pl.when(pl.program_id(1) == 0)
    def _():
        h = _rms(x_ref[...]) * g_ref[...]
        h = h * (1.0 + sc_ref[0]) + sh_ref[0]
        h_ref[...] = h.astype(BF16)

    o_ref[...] = _dot(h_ref[...], w_ref[...])


def _inproj_call(x2, g, mod3, w_bf, row_of_tile, tm, tn, name):
    t, d = x2.shape
    n = w_bf.shape[1]
    nd = d
    return pl.pallas_call(
        _inproj_kernel,
        grid=(t // tm, n // tn),
        in_specs=[
            pl.BlockSpec((tm, d), lambda i, j: (i, 0)),
            pl.BlockSpec((1, d), lambda i, j: (0, 0)),
            pl.BlockSpec((1, 1, nd), lambda i, j: (row_of_tile(i), 0, 0)),
            pl.BlockSpec((1, 1, nd), lambda i, j: (row_of_tile(i), 0, 1)),
            pl.BlockSpec((d, tn), lambda i, j: (0, j)),
        ],
        out_specs=pl.BlockSpec((tm, tn), lambda i, j: (i, j)),
        out_shape=jax.ShapeDtypeStruct((t, n), F32),
        scratch_shapes=[pltpu.VMEM((tm, d), BF16)],
        compiler_params=pltpu.CompilerParams(
            dimension_semantics=("arbitrary", "arbitrary"), vmem_limit_bytes=_vmem_limit(52 << 20)),
        name=name,
    )(x2, g, mod3, mod3, w_bf)


def _attn_kernel(q_ref, k_ref, v_ref, kc_ref, vc_ref, cos_ref, sin_ref, bias_ref, g_ref, o_ref,
                 qs_ref, ks_ref, vs_ref, kcs_ref, vcs_ref):
    cos = cos_ref[...]
    sin = sin_ref[...]
    lane = lax.broadcasted_iota(jnp.int32, cos.shape, 1)
    first_half = (lane % (HEAD_DIM // 2)) < (HEAD_DIM // 4)

    def rope(x):
        partner = jnp.where(first_half, pltpu.roll(x, HEAD_DIM - HEAD_DIM // 4, 1),
                            pltpu.roll(x, HEAD_DIM // 4, 1))
        return x * cos + partner * sin

    qs_ref[...] = rope(q_ref[...]).astype(BF16)
    ks_ref[...] = rope(k_ref[...]).astype(BF16)
    vs_ref[...] = v_ref[...].astype(BF16)
    kcs_ref[...] = kc_ref[...].astype(BF16)
    vcs_ref[...] = vc_ref[...].astype(BF16)

    scale = HEAD_DIM ** -0.5
    qb = Q_ROWS * GRID_W
    kb = K_ROWS * GRID_W
    n_blocks = q_ref.shape[0] // qb
    max_start = q_ref.shape[0] // GRID_W - K_ROWS

    def body(j, carry):
        q0 = pl.multiple_of(j * qb, qb)
        k_row = jnp.clip(j * Q_ROWS - NA_ROWS // 2, 0, max_start)
        k0 = pl.multiple_of(k_row * GRID_W, qb)
        q = qs_ref[pl.ds(q0, qb), :]
        s_loc = _dot_nt(q, ks_ref[pl.ds(k0, kb), :]) * scale + bias_ref[0, j]
        s_ctx = _dot_nt(q, kcs_ref[...]) * scale
        m = jnp.maximum(jnp.max(s_loc, axis=-1, keepdims=True), jnp.max(s_ctx, axis=-1, keepdims=True))
        p_loc = jnp.exp(s_loc - m)
        p_ctx = jnp.exp(s_ctx - m)
        denom = jnp.sum(p_loc, axis=-1, keepdims=True) + jnp.sum(p_ctx, axis=-1, keepdims=True)
        o = _dot(p_loc.astype(BF16), vs_ref[pl.ds(k0, kb), :]) + _dot(p_ctx.astype(BF16), vcs_ref[...])
        o = o / denom
        o_ref[pl.ds(q0, qb), :] = (_rms(o) * g_ref[...]).astype(BF16)
        return carry

    lax.fori_loop(0, n_blocks, body, 0)


def _attn_call(proj, proj_ctx, cos_t, sin_t, bias, g_attn, batch, seq, ctx_len):
    d_attn = N_HEADS * HEAD_DIM
    q_blk = (proj.shape[1] - 3 * d_attn) // HEAD_DIM
    n_blocks = seq // (Q_ROWS * GRID_W)
    qb, kb = Q_ROWS * GRID_W, K_ROWS * GRID_W
    return pl.pallas_call(
        _attn_kernel,
        grid=(N_HEADS, batch),
        in_specs=[
            pl.BlockSpec((seq, HEAD_DIM), lambda h, b: (b, q_blk + h)),
            pl.BlockSpec((seq, HEAD_DIM), lambda h, b: (b, q_blk + N_HEADS + h)),
            pl.BlockSpec((seq, HEAD_DIM), lambda h, b: (b, q_blk + 2 * N_HEADS + h)),
            pl.BlockSpec((ctx_len, HEAD_DIM), lambda h, b: (b, h)),
            pl.BlockSpec((ctx_len, HEAD_DIM), lambda h, b: (b, N_HEADS + h)),
            pl.BlockSpec((seq, HEAD_DIM), lambda h, b: (0, 0)),
            pl.BlockSpec((seq, HEAD_DIM), lambda h, b: (0, 0)),
            pl.BlockSpec((1, n_blocks, qb, kb), lambda h, b: (h, 0, 0, 0)),
            pl.BlockSpec((1, HEAD_DIM), lambda h, b: (0, h)),
        ],
        out_specs=pl.BlockSpec((seq, HEAD_DIM), lambda h, b: (b, h)),
        out_shape=jax.ShapeDtypeStruct((batch * seq, d_attn), BF16),
        scratch_shapes=[
            pltpu.VMEM((seq, HEAD_DIM), BF16),
            pltpu.VMEM((seq, HEAD_DIM), BF16),
            pltpu.VMEM((seq, HEAD_DIM), BF16),
            pltpu.VMEM((ctx_len, HEAD_DIM), BF16),
            pltpu.VMEM((ctx_len, HEAD_DIM), BF16),
        ],
        compiler_params=pltpu.CompilerParams(
            dimension_semantics=("arbitrary", "arbitrary"), vmem_limit_bytes=_vmem_limit(48 << 20)),
        name="attn",
    )(proj, proj, proj, proj_ctx, proj_ctx, cos_t, sin_t, bias, g_attn)


def _rope_tables(seq):
    t = jnp.arange(seq, dtype=jnp.int32)
    pos = jnp.stack([t // GRID_W, t % GRID_W], axis=-1).astype(F32)
    n_freq = HEAD_DIM // 4
    inv_freq = ROPE_THETA ** (-jnp.arange(n_freq, dtype=F32) / n_freq)
    ang = pos[:, :, None] * inv_freq
    cos, sin = jnp.cos(ang), jnp.sin(ang)
    cos_t = jnp.concatenate([cos, cos], axis=-1).reshape(seq, HEAD_DIM)
    sin_t = jnp.concatenate([-sin, sin], axis=-1).reshape(seq, HEAD_DIM)
    return cos_t, sin_t


def _bias_table(rpb, rows):
    w = GRID_W
    qc = np.arange(w)[:, None]
    kc = np.arange(w)[None, :]
    c0 = np.clip(qc - NA_COLS // 2, 0, w - NA_COLS)
    col_ok = (kc >= c0) & (kc < c0 + NA_COLS)
    col_off = np.clip(kc - qc + NA_COLS - 1, 0, 2 * NA_COLS - 2)
    toep = jnp.where(col_ok[None, None], rpb[:, :, col_off], MASK_VALUE)
    masked = jnp.full((rpb.shape[0], 1, w, w), MASK_VALUE, F32)
    toep = jnp.concatenate([toep, masked], axis=1)
    n_blocks = rows // Q_ROWS
    kh = min(NA_ROWS, rows)
    sel = np.zeros((n_blocks, Q_ROWS, K_ROWS), np.int32)
    for j in range(n_blocks):
        k_start = int(np.clip(j * Q_ROWS - NA_ROWS // 2, 0, rows - K_ROWS))
        for a in range(Q_ROWS):
            r = j * Q_ROWS + a
            r0 = int(np.clip(r - kh // 2, 0, rows - kh))
            for b in range(K_ROWS):
                kr = k_start + b
                sel[j, a, b] = kr - r + NA_ROWS - 1 if r0 <= kr < r0 + kh else 2 * NA_ROWS - 1
    blocks = toep[:, sel]
    blocks = jnp.transpose(blocks, (0, 1, 2, 4, 3, 5))
    return blocks.reshape(rpb.shape[0], n_blocks, Q_ROWS * w, K_ROWS * w)


def _conv_kernel(ub_ref, uc_ref, ux_ref, w_ref, b_ref, g_ref, o_ref):
    p = uc_ref[...] * ux_ref[...]
    seq = p.shape[0]
    row = lax.broadcasted_iota(jnp.int32, p.shape, 0)
    prev = jnp.where(row == 0, 0.0, pltpu.roll(p, 1, 0))
    nxt = jnp.where(row == seq - 1, 0.0, pltpu.roll(p, seq - 1, 0))
    w = w_ref[...]
    conv = prev * w[0:1] + p * w[1:2] + nxt * w[2:3] + b_ref[...]
    y = ub_ref[...] * conv
    g = g_ref[...]
    for c0 in range(0, y.shape[1], CONV_GROUP_DIM):
        sl = slice(c0, c0 + CONV_GROUP_DIM)
        o_ref[:, sl] = (_rms(y[:, sl]) * g[:, sl]).astype(BF16)


def _conv_call(proj, conv_w, conv_b, g_conv, batch, seq, cb=256):
    d_conv = conv_w.shape[1]
    nc = d_conv // cb
    return pl.pallas_call(
        _conv_kernel,
        grid=(batch, nc),
        in_specs=[
            pl.BlockSpec((seq, cb), lambda b, c: (b, c)),
            pl.BlockSpec((seq, cb), lambda b, c: (b, nc + c)),
            pl.BlockSpec((seq, cb), lambda b, c: (b, 2 * nc + c)),
            pl.BlockSpec((conv_w.shape[0], cb), lambda b, c: (0, c)),
            pl.BlockSpec((1, cb), lambda b, c: (0, c)),
            pl.BlockSpec((1, cb), lambda b, c: (0, c)),
        ],
        out_specs=pl.BlockSpec((seq, cb), lambda b, c: (b, c)),
        out_shape=jax.ShapeDtypeStruct((batch * seq, d_conv), BF16),
        compiler_params=pltpu.CompilerParams(
            dimension_semantics=("arbitrary", "arbitrary"), vmem_limit_bytes=_vmem_limit(48 << 20)),
        name="conv",
    )(proj, proj, proj, conv_w, conv_b, g_conv)


def _outproj_kernel(yc_ref, ya_ref, w_ref, x_ref, gpost_ref, gate_ref, gpre_ref, sh_ref, sc_ref, wr_ref,
                    x1_ref, t_ref, lg_ref):
    dc = yc_ref.shape[1]
    y = _dot(yc_ref[...], w_ref[0:dc, :]) + _dot(ya_ref[...], w_ref[dc:, :])
    x1 = x_ref[...] + gate_ref[0] * (_rms(y) * gpost_ref[...])
    x1_ref[...] = x1
    t = (_rms(x1) * gpre_ref[...]) * (1.0 + sc_ref[0]) + sh_ref[0]
    t_ref[...] = t.astype(BF16)
    lg_ref[...] = _dot_nt(wr_ref[...], t, precision=lax.Precision.HIGHEST)


def _outproj_call(yc, ya, w_out_bf, x2, g_post, g_pre, mod3, w_router_t, seq, tm=512):
    t, d = x2.shape
    dc = yc.shape[1]
    ne = w_router_t.shape[0]
    tiles_per_seq = seq // tm
    row = lambda i: i // tiles_per_seq
    return pl.pallas_call(
        _outproj_kernel,
        grid=(t // tm,),
        in_specs=[
            pl.BlockSpec((tm, dc), lambda i: (i, 0)),
            pl.BlockSpec((tm, ya.shape[1]), lambda i: (i, 0)),
            pl.BlockSpec(w_out_bf.shape, lambda i: (0, 0)),
            pl.BlockSpec((tm, d), lambda i: (i, 0)),
            pl.BlockSpec((1, d), lambda i: (0, 0)),
            pl.BlockSpec((1, 1, d), lambda i: (row(i), 0, 2)),
            pl.BlockSpec((1, d), lambda i: (0, 0)),
            pl.BlockSpec((1, 1, d), lambda i: (row(i), 0, 3)),
            pl.BlockSpec((1, 1, d), lambda i: (row(i), 0, 4)),
            pl.BlockSpec((ne, d), lambda i: (0, 0)),
        ],
        out_specs=[
            pl.BlockSpec((tm, d), lambda i: (i, 0)),
            pl.BlockSpec((tm, d), lambda i: (i, 0)),
            pl.BlockSpec((ne, tm), lambda i: (0, i)),
        ],
        out_shape=[
            jax.ShapeDtypeStruct((t, d), F32),
            jax.ShapeDtypeStruct((t, d), BF16),
            jax.ShapeDtypeStruct((ne, t), F32),
        ],
        compiler_params=pltpu.CompilerParams(
            dimension_semantics=("arbitrary",), vmem_limit_bytes=_vmem_limit(58 << 20)),
        name="out_proj",
    )(yc, ya, w_out_bf, x2, g_post, mod3, g_pre, mod3, mod3, w_router_t)


def _first_argmax_mask(v, axis):
    m = jnp.max(v, axis=axis, keepdims=True)
    idx = lax.broadcasted_iota(jnp.int32, v.shape, axis)
    big = v.shape[axis]
    first = jnp.min(jnp.where(v == m, idx, big), axis=axis, keepdims=True)
    return idx == first, m


def _router_kernel(lg_ref, rb_ref, gates_ref):
    ne, tm = lg_ref.shape
    per = ne // N_EXPERT_GROUPS
    scores = jax.nn.sigmoid(lg_ref[...])
    sel = scores + rb_ref[...]
    s3 = sel.reshape(N_EXPERT_GROUPS, per, tm)
    hit1, m1 = _first_argmax_mask(s3, 1)
    m2 = jnp.max(jnp.where(hit1, -jnp.inf, s3), axis=1, keepdims=True)
    grp = (m1 + m2).reshape(N_EXPERT_GROUPS, tm)
    gmask = jnp.zeros(grp.shape, jnp.bool_)
    for _ in range(TOPK_GROUPS):
        hit, _ = _first_argmax_mask(jnp.where(gmask, -jnp.inf, grp), 0)
        gmask = gmask | hit
    emask = jnp.broadcast_to(gmask.reshape(N_EXPERT_GROUPS, 1, tm), s3.shape).reshape(ne, tm)
    cand = jnp.where(emask, sel, -jnp.inf)
    chosen = jnp.zeros(cand.shape, jnp.bool_)
    for _ in range(TOP_K):
        hit, _ = _first_argmax_mask(jnp.where(chosen, -jnp.inf, cand), 0)
        hit = hit & ~chosen
        chosen = chosen | hit
    wsel = jnp.where(chosen, scores, 0.0)
    gates = wsel / jnp.sum(wsel, axis=0, keepdims=True) * ROUTED_SCALE
    gates_ref[...] = gates.T


def _router_call(logits_t, router_bias_col, tm=512):
    ne, t = logits_t.shape
    return pl.pallas_call(
        _router_kernel,
        grid=(t // tm,),
        in_specs=[
            pl.BlockSpec((ne, tm), lambda i: (0, i)),
            pl.BlockSpec((ne, 1), lambda i: (0, 0)),
        ],
        out_specs=pl.BlockSpec((tm, ne), lambda i: (i, 0)),
        out_shape=jax.ShapeDtypeStruct((t, ne), F32),
        compiler_params=pltpu.CompilerParams(dimension_semantics=("arbitrary",)),
        name="router",
    )(logits_t, router_bias_col)


def _moe_kernel(t_ref, gates_ref, wg_ref, wu_ref, wd_ref, o_ref):
    e = pl.program_id(1)
    t = t_ref[...]
    gates = gates_ref[...]
    lane = lax.broadcasted_iota(jnp.int32, gates.shape, 1)
    g = jnp.sum(jnp.where(lane == e, gates, 0.0), axis=1, keepdims=True)
    hid = _silu(_dot(t, wg_ref[0])) * _dot(t, wu_ref[0])
    contrib = _dot((hid * g).astype(BF16), wd_ref[0])

    @pl.when(e == 0)
    def _():
        o_ref[...] = contrib

    @pl.when(e != 0)
    def _():
        o_ref[...] += contrib


def _moe_call(t_bf, gates, wg_bf, wu_bf, wd_bf, tm=1024):
    t, d = t_bf.shape
    ne, _, df = wg_bf.shape
    return pl.pallas_call(
        _moe_kernel,
        grid=(t // tm, ne),
        in_specs=[
            pl.BlockSpec((tm, d), lambda i, e: (i, 0)),
            pl.BlockSpec((tm, ne), lambda i, e: (i, 0)),
            pl.BlockSpec((1, d, df), lambda i, e: (e, 0, 0)),
            pl.BlockSpec((1, d, df), lambda i, e: (e, 0, 0)),
            pl.BlockSpec((1, df, d), lambda i, e: (e, 0, 0)),
        ],
        out_specs=pl.BlockSpec((tm, d), lambda i, e: (i, 0)),
        out_shape=jax.ShapeDtypeStruct((t, d), F32),
        compiler_params=pltpu.CompilerParams(
            dimension_semantics=("arbitrary", "arbitrary"), vmem_limit_bytes=_vmem_limit(56 << 20)),
        name="moe",
    )(t_bf, gates, wg_bf, wu_bf, wd_bf)


def _final_kernel(t_ref, wsg_ref, wsu_ref, wsd_ref, f_ref, x1_ref, gpost_ref, gate_ref, o_ref):
    t = t_ref[...]
    hid = _silu(_dot(t, wsg_ref[...])) * _dot(t, wsu_ref[...])
    f = _dot(hid.astype(BF16), wsd_ref[...]) + f_ref[...]
    o_ref[...] = x1_ref[...] + gate_ref[0] * (_rms(f) * gpost_ref[...])


def _final_call(t_bf, wsg_bf, wsu_bf, wsd_bf, f, x1, g_post, mod3, seq, tm=512):
    t, d = x1.shape
    tiles_per_seq = seq // tm
    return pl.pallas_call(
        _final_kernel,
        grid=(t // tm,),
        in_specs=[
            pl.BlockSpec((tm, d), lambda i: (i, 0)),
            pl.BlockSpec(wsg_bf.shape, lambda i: (0, 0)),
            pl.BlockSpec(wsu_bf.shape, lambda i: (0, 0)),
            pl.BlockSpec(wsd_bf.shape, lambda i: (0, 0)),
            pl.BlockSpec((tm, d), lambda i: (i, 0)),
            pl.BlockSpec((tm, d), lambda i: (i, 0)),
            pl.BlockSpec((1, d), lambda i: (0, 0)),
            pl.BlockSpec((1, 1, d), lambda i: (i // tiles_per_seq, 0, 5)),
        ],
        out_specs=pl.BlockSpec((tm, d), lambda i: (i, 0)),
        out_shape=jax.ShapeDtypeStruct((t, d), F32),
        compiler_params=pltpu.CompilerParams(
            dimension_semantics=("arbitrary",), vmem_limit_bytes=_vmem_limit(48 << 20)),
        name="final",
    )(t_bf, wsg_bf, wsu_bf, wsd_bf, f, x1, g_post, mod3)


def kernel(x, c, ctx, c_ctx, w_mod, b_mod, g_pre_mix, g_post_mix, g_pre_ffn, g_post_ffn, w_in, conv_w, conv_b,
           rpb, g_conv_out, g_attn_out, w_out, w_router, router_bias, w_exp_gate, w_exp_up, w_exp_down,
           w_sh_gate, w_sh_up, w_sh_down):
    batch, seq, d = x.shape
    ctx_len = ctx.shape[1]
    assert w_mod.shape[0] == 1, "single-layer kernel"
    assert seq % (Q_ROWS * GRID_W) == 0 and seq // GRID_W >= K_ROWS
    d_attn = N_HEADS * HEAD_DIM
    kv_offset = w_in.shape[2] - 2 * d_attn
    row2 = lambda a: a.reshape(1, -1)

    x2 = x.reshape(batch * seq, d)
    ctx2 = ctx.reshape(batch * ctx_len, d)

    n_rows = -(-(batch + 1) // 8) * 8
    cvec = jnp.concatenate([c, c_ctx[None], jnp.zeros((n_rows - batch - 1, d), F32)], axis=0)
    mod = _mod_call(cvec, w_mod[0], row2(b_mod[0]))
    mod3 = mod.reshape(n_rows, 1, 6 * d)

    w_in_bf = w_in[0].astype(BF16)
    tm_in = 1024
    tiles_per_seq = seq // tm_in
    proj = _inproj_call(x2, row2(g_pre_mix[0]), mod3, w_in_bf, lambda i: i // tiles_per_seq,
                        tm_in, 1024, "in_proj")
    proj_ctx = _inproj_call(ctx2, row2(g_pre_mix[0]), mod3, w_in_bf[:, kv_offset:], lambda i: batch,
                            min(tm_in, ctx2.shape[0]), 1024, "in_proj_ctx")

    cos_t, sin_t = _rope_tables(seq)
    bias = _bias_table(rpb[0], seq // GRID_W)
    ya = _attn_call(proj, proj_ctx, cos_t, sin_t, bias, row2(g_attn_out[0]), batch, seq, ctx_len)
    yc = _conv_call(proj, conv_w[0], row2(conv_b[0]), row2(g_conv_out[0]), batch, seq)

    x1, t_bf, logits_t = _outproj_call(yc, ya, w_out[0].astype(BF16), x2, row2(g_post_mix[0]),
                                       row2(g_pre_ffn[0]), mod3, w_router[0].T, seq)
    gates = _router_call(logits_t, router_bias[0].reshape(-1, 1))
    f = _moe_call(t_bf, gates, w_exp_gate[0].astype(BF16), w_exp_up[0].astype(BF16),
                  w_exp_down[0].astype(BF16))
    out = _final_call(t_bf, w_sh_gate[0].astype(BF16), w_sh_up[0].astype(BF16), w_sh_down[0].astype(BF16),
                      f, x1, row2(g_post_ffn[0]), mod3, seq)
    return out.reshape(batch, seq, d)
```

```python
import functools

import numpy as np
import jax
import jax.numpy as jnp
from jax import lax
from jax.experimental import pallas as pl
from jax.experimental.pallas import tpu as pltpu

F32 = jnp.float32
BF16 = jnp.bfloat16
I32 = jnp.int32

GRID_W = 64
N_HEADS = 8
HEAD_DIM = 128
NA_ROWS = 8
NA_COLS = 16
ROPE_THETA = 10000.0
CONV_GROUP_DIM = 128
N_EXPERTS = 64
N_EXPERT_GROUPS = 8
TOPK_GROUPS = 4
TOP_K = 8
ROUTED_SCALE = 2.5
EPS = 1e-6
MASK_VALUE = -1e30

Q_ROWS = 4
K_ROWS = 12

V7X_VMEM_BYTES = 64 * 1024 * 1024
LANES = 128
SUBLANES = 8

EXPERT_ROW_TILE = 256
DISPATCH_TOKENS = 1024
COMBINE_TOKENS = 128
ISSUE_UNROLL = 8


def _vmem_limit(nbytes):
    return int(min(nbytes, V7X_VMEM_BYTES - 6 * 1024 * 1024))


def _rms(x):
    return x * lax.rsqrt(jnp.mean(x * x, axis=-1, keepdims=True) + EPS)


def _silu(x):
    return x * jax.nn.sigmoid(x)


def _dot(a, b):
    return jnp.dot(a, b, preferred_element_type=F32)


def _dot_nt(a, b, precision=None):
    return lax.dot_general(a, b, (((1,), (1,)), ((), ())), preferred_element_type=F32, precision=precision)


def _mod_kernel(c_ref, w_ref, b_ref, o_ref):
    a = _silu(c_ref[...]).astype(BF16)
    o_ref[...] = _dot(a, w_ref[...].astype(BF16)) + b_ref[...]


def _mod_call(cvec, w_mod, b_mod, tn=1024):
    rows, d = cvec.shape
    n = w_mod.shape[1]
    return pl.pallas_call(
        _mod_kernel,
        grid=(n // tn,),
        in_specs=[
            pl.BlockSpec((rows, d), lambda j: (0, 0)),
            pl.BlockSpec((d, tn), lambda j: (0, j)),
            pl.BlockSpec((1, tn), lambda j: (0, j)),
        ],
        out_specs=pl.BlockSpec((rows, tn), lambda j: (0, j)),
        out_shape=jax.ShapeDtypeStruct((rows, n), F32),
        compiler_params=pltpu.CompilerParams(
            dimension_semantics=("arbitrary",), vmem_limit_bytes=_vmem_limit(40 << 20)),
        name="mod",
    )(cvec, w_mod, b_mod)


def _inproj_kernel(x_ref, g_ref, sh_ref, sc_ref, w_ref, o_ref, h_ref):
    @pl.when(pl.program_id(1) == 0)
    def _():
        h = _rms(x_ref[...]) * g_ref[...]
        h = h * (1.0 + sc_ref[0]) + sh_ref[0]
        h_ref[...] = h.astype(BF16)

    o_ref[...] = _dot(h_ref[...], w_ref[...])


def _inproj_call(x2, g, mod3, w_bf, row_of_tile, tm, tn, name):
    t, d = x2.shape
    n = w_bf.shape[1]
    return pl.pallas_call(
        _inproj_kernel,
        grid=(t // tm, n // tn),
        in_specs=[
            pl.BlockSpec((tm, d), lambda i, j: (i, 0)),
            pl.BlockSpec((1, d), lambda i, j: (0, 0)),
            pl.BlockSpec((1, 1, d), lambda i, j: (row_of_tile(i), 0, 0)),
            pl.BlockSpec((1, 1, d), lambda i, j: (row_of_tile(i), 0, 1)),
            pl.BlockSpec((d, tn), lambda i, j: (0, j)),
        ],
        out_specs=pl.BlockSpec((tm, tn), lambda i, j: (i, j)),
        out_shape=jax.ShapeDtypeStruct((t, n), F32),
        scratch_shapes=[pltpu.VMEM((tm, d), BF16)],
        compiler_params=pltpu.CompilerParams(
            dimension_semantics=("arbitrary", "arbitrary"), vmem_limit_bytes=_vmem_limit(52 << 20)),
        name=name,
    )(x2, g, mod3, mod3, w_bf)


def _attn_kernel(q_ref, k_ref, v_ref, kc_ref, vc_ref, cos_ref, sin_ref, bias_ref, g_ref, o_ref,
                 qs_ref, ks_ref, vs_ref, kcs_ref, vcs_ref):
    cos = cos_ref[...]
    sin = sin_ref[...]
    lane = lax.broadcasted_iota(I32, cos.shape, 1)
    first_half = (lane % (HEAD_DIM // 2)) < (HEAD_DIM // 4)

    def rope(x):
        partner = jnp.where(first_half, pltpu.roll(x, HEAD_DIM - HEAD_DIM // 4, 1),
                            pltpu.roll(x, HEAD_DIM // 4, 1))
        return x * cos + partner * sin

    qs_ref[...] = rope(q_ref[...]).astype(BF16)
    ks_ref[...] = rope(k_ref[...]).astype(BF16)
    vs_ref[...] = v_ref[...].astype(BF16)
    kcs_ref[...] = kc_ref[...].astype(BF16)
    vcs_ref[...] = vc_ref[...].astype(BF16)

    scale = HEAD_DIM ** -0.5
    qb = Q_ROWS * GRID_W
    kb = K_ROWS * GRID_W
    n_blocks = q_ref.shape[0] // qb
    max_start = q_ref.shape[0] // GRID_W - K_ROWS

    def body(j, carry):
        q0 = pl.multiple_of(j * qb, qb)
        k_row = jnp.clip(j * Q_ROWS - NA_ROWS // 2, 0, max_start)
        k0 = pl.multiple_of(k_row * GRID_W, qb)
        q = qs_ref[pl.ds(q0, qb), :]
        s_loc = _dot_nt(q, ks_ref[pl.ds(k0, kb), :]) * scale + bias_ref[0, j]
        s_ctx = _dot_nt(q, kcs_ref[...]) * scale
        m = jnp.maximum(jnp.max(s_loc, axis=-1, keepdims=True), jnp.max(s_ctx, axis=-1, keepdims=True))
        p_loc = jnp.exp(s_loc - m)
        p_ctx = jnp.exp(s_ctx - m)
        denom = jnp.sum(p_loc, axis=-1, keepdims=True) + jnp.sum(p_ctx, axis=-1, keepdims=True)
        o = _dot(p_loc.astype(BF16), vs_ref[pl.ds(k0, kb), :]) + _dot(p_ctx.astype(BF16), vcs_ref[...])
        o = o / denom
        o_ref[pl.ds(q0, qb), :] = (_rms(o) * g_ref[...]).astype(BF16)
        return carry

    lax.fori_loop(0, n_blocks, body, 0)


def _attn_call(proj, proj_ctx, cos_t, sin_t, bias, g_attn, batch, seq, ctx_len):
    d_attn = N_HEADS * HEAD_DIM
    q_blk = (proj.shape[1] - 3 * d_attn) // HEAD_DIM
    n_blocks = seq // (Q_ROWS * GRID_W)
    qb, kb = Q_ROWS * GRID_W, K_ROWS * GRID_W
    return pl.pallas_call(
        _attn_kernel,
        grid=(N_HEADS, batch),
        in_specs=[
            pl.BlockSpec((seq, HEAD_DIM), lambda h, b: (b, q_blk + h)),
            pl.BlockSpec((seq, HEAD_DIM), lambda h, b: (b, q_blk + N_HEADS + h)),
            pl.BlockSpec((seq, HEAD_DIM), lambda h, b: (b, q_blk + 2 * N_HEADS + h)),
            pl.BlockSpec((ctx_len, HEAD_DIM), lambda h, b: (b, h)),
            pl.BlockSpec((ctx_len, HEAD_DIM), lambda h, b: (b, N_HEADS + h)),
            pl.BlockSpec((seq, HEAD_DIM), lambda h, b: (0, 0)),
            pl.BlockSpec((seq, HEAD_DIM), lambda h, b: (0, 0)),
            pl.BlockSpec((1, n_blocks, qb, kb), lambda h, b: (h, 0, 0, 0)),
            pl.BlockSpec((1, HEAD_DIM), lambda h, b: (0, h)),
        ],
        out_specs=pl.BlockSpec((seq, HEAD_DIM), lambda h, b: (b, h)),
        out_shape=jax.ShapeDtypeStruct((batch * seq, d_attn), BF16),
        scratch_shapes=[
            pltpu.VMEM((seq, HEAD_DIM), BF16),
            pltpu.VMEM((seq, HEAD_DIM), BF16),
            pltpu.VMEM((seq, HEAD_DIM), BF16),
            pltpu.VMEM((ctx_len, HEAD_DIM), BF16),
            pltpu.VMEM((ctx_len, HEAD_DIM), BF16),
        ],
        compiler_params=pltpu.CompilerParams(
            dimension_semantics=("arbitrary", "arbitrary"), vmem_limit_bytes=_vmem_limit(48 << 20)),
        name="attn",
    )(proj, proj, proj, proj_ctx, proj_ctx, cos_t, sin_t, bias, g_attn)


def _rope_tables(seq):
    t = jnp.arange(seq, dtype=I32)
    pos = jnp.stack([t // GRID_W, t % GRID_W], axis=-1).astype(F32)
    n_freq = HEAD_DIM // 4
    inv_freq = ROPE_THETA ** (-jnp.arange(n_freq, dtype=F32) / n_freq)
    ang = pos[:, :, None] * inv_freq
    cos, sin = jnp.cos(ang), jnp.sin(ang)
    cos_t = jnp.concatenate([cos, cos], axis=-1).reshape(seq, HEAD_DIM)
    sin_t = jnp.concatenate([-sin, sin], axis=-1).reshape(seq, HEAD_DIM)
    return cos_t, sin_t


def _bias_table(rpb, rows):
    w = GRID_W
    qc = np.arange(w)[:, None]
    kc = np.arange(w)[None, :]
    c0 = np.clip(qc - NA_COLS // 2, 0, w - NA_COLS)
    col_ok = (kc >= c0) & (kc < c0 + NA_COLS)
    col_off = np.clip(kc - qc + NA_COLS - 1, 0, 2 * NA_COLS - 2)
    toep = jnp.where(col_ok[None, None], rpb[:, :, col_off], MASK_VALUE)
    masked = jnp.full((rpb.shape[0], 1, w, w), MASK_VALUE, F32)
    toep = jnp.concatenate([toep, masked], axis=1)
    n_blocks = rows // Q_ROWS
    kh = min(NA_ROWS, rows)
    sel = np.zeros((n_blocks, Q_ROWS, K_ROWS), np.int32)
    for j in range(n_blocks):
        k_start = int(np.clip(j * Q_ROWS - NA_ROWS // 2, 0, rows - K_ROWS))
        for a in range(Q_ROWS):
            r = j * Q_ROWS + a
            r0 = int(np.clip(r - kh // 2, 0, rows - kh))
            for b in range(K_ROWS):
                kr = k_start + b
                sel[j, a, b] = kr - r + NA_ROWS - 1 if r0 <= kr < r0 + kh else 2 * NA_ROWS - 1
    blocks = toep[:, sel]
    blocks = jnp.transpose(blocks, (0, 1, 2, 4, 3, 5))
    return blocks.reshape(rpb.shape[0], n_blocks, Q_ROWS * w, K_ROWS * w)


def _conv_kernel(ub_ref, uc_ref, ux_ref, w_ref, b_ref, g_ref, o_ref):
    p = uc_ref[...] * ux_ref[...]
    seq = p.shape[0]
    row = lax.broadcasted_iota(I32, p.shape, 0)
    prev = jnp.where(row == 0, 0.0, pltpu.roll(p, 1, 0))
    nxt = jnp.where(row == seq - 1, 0.0, pltpu.roll(p, seq - 1, 0))
    w = w_ref[...]
    conv = prev * w[0:1] + p * w[1:2] + nxt * w[2:3] + b_ref[...]
    y = ub_ref[...] * conv
    g = g_ref[...]
    for c0 in range(0, y.shape[1], CONV_GROUP_DIM):
        sl = slice(c0, c0 + CONV_GROUP_DIM)
        o_ref[:, sl] = (_rms(y[:, sl]) * g[:, sl]).astype(BF16)


def _conv_call(proj, conv_w, conv_b, g_conv, batch, seq, cb=256):
    d_conv = conv_w.shape[1]
    nc = d_conv // cb
    return pl.pallas_call(
        _conv_kernel,
        grid=(batch, nc),
        in_specs=[
            pl.BlockSpec((seq, cb), lambda b, c: (b, c)),
            pl.BlockSpec((seq, cb), lambda b, c: (b, nc + c)),
            pl.BlockSpec((seq, cb), lambda b, c: (b, 2 * nc + c)),
            pl.BlockSpec((conv_w.shape[0], cb), lambda b, c: (0, c)),
            pl.BlockSpec((1, cb), lambda b, c: (0, c)),
            pl.BlockSpec((1, cb), lambda b, c: (0, c)),
        ],
        out_specs=pl.BlockSpec((seq, cb), lambda b, c: (b, c)),
        out_shape=jax.ShapeDtypeStruct((batch * seq, d_conv), BF16),
        compiler_params=pltpu.CompilerParams(
            dimension_semantics=("arbitrary", "arbitrary"), vmem_limit_bytes=_vmem_limit(48 << 20)),
        name="conv",
    )(proj, proj, proj, conv_w, conv_b, g_conv)


def _outproj_kernel(yc_ref, ya_ref, w_ref, x_ref, gpost_ref, gate_ref, gpre_ref, sh_ref, sc_ref, wr_ref,
                    x1_ref, t_ref, lg_ref):
    dc = yc_ref.shape[1]
    y = _dot(yc_ref[...], w_ref[0:dc, :]) + _dot(ya_ref[...], w_ref[dc:, :])
    x1 = x_ref[...] + gate_ref[0] * (_rms(y) * gpost_ref[...])
    x1_ref[...] = x1
    t = (_rms(x1) * gpre_ref[...]) * (1.0 + sc_ref[0]) + sh_ref[0]
    t_ref[...] = t
    lg_ref[...] = _dot_nt(wr_ref[...], t, precision=lax.Precision.HIGHEST)


def _outproj_call(yc, ya, w_out_bf, x2, g_post, g_pre, mod3, w_router_t, seq, tm=256):
    t, d = x2.shape
    dc = yc.shape[1]
    ne = w_router_t.shape[0]
    tiles_per_seq = seq // tm
    row = lambda i: i // tiles_per_seq
    return pl.pallas_call(
        _outproj_kernel,
        grid=(t // tm,),
        in_specs=[
            pl.BlockSpec((tm, dc), lambda i: (i, 0)),
            pl.BlockSpec((tm, ya.shape[1]), lambda i: (i, 0)),
            pl.BlockSpec(w_out_bf.shape, lambda i: (0, 0)),
            pl.BlockSpec((tm, d), lambda i: (i, 0)),
            pl.BlockSpec((1, d), lambda i: (0, 0)),
            pl.BlockSpec((1, 1, d), lambda i: (row(i), 0, 2)),
            pl.BlockSpec((1, d), lambda i: (0, 0)),
            pl.BlockSpec((1, 1, d), lambda i: (row(i), 0, 3)),
            pl.BlockSpec((1, 1, d), lambda i: (row(i), 0, 4)),
            pl.BlockSpec((ne, d), lambda i: (0, 0)),
        ],
        out_specs=[
            pl.BlockSpec((tm, d), lambda i: (i, 0)),
            pl.BlockSpec((tm, d), lambda i: (i, 0)),
            pl.BlockSpec((ne, tm), lambda i: (0, i)),
        ],
        out_shape=[
            jax.ShapeDtypeStruct((t, d), F32),
            jax.ShapeDtypeStruct((t, d), F32),
            jax.ShapeDtypeStruct((ne, t), F32),
        ],
        compiler_params=pltpu.CompilerParams(
            dimension_semantics=("arbitrary",), vmem_limit_bytes=_vmem_limit(52 << 20)),
        name="out_proj",
    )(yc, ya, w_out_bf, x2, g_post, mod3, g_pre, mod3, mod3, w_router_t)


def _first_argmax_mask(v, axis):
    m = jnp.max(v, axis=axis, keepdims=True)
    idx = lax.broadcasted_iota(I32, v.shape, axis)
    big = v.shape[axis]
    first = jnp.min(jnp.where(v == m, idx, big), axis=axis, keepdims=True)
    return idx == first, m


def _router_kernel(lg_ref, rb_ref, ek_ref, rk_ref, gk_ref, cnt_ref, carry_ref):
    ne, tm = lg_ref.shape
    per = ne // N_EXPERT_GROUPS

    @pl.when(pl.program_id(0) == 0)
    def _():
        carry_ref[...] = jnp.zeros_like(carry_ref)

    scores = jax.nn.sigmoid(lg_ref[...])
    sel = scores + rb_ref[...]
    s3 = sel.reshape(N_EXPERT_GROUPS, per, tm)
    hit1, m1 = _first_argmax_mask(s3, 1)
    m2 = jnp.max(jnp.where(hit1, -jnp.inf, s3), axis=1, keepdims=True)
    grp = (m1 + m2).reshape(N_EXPERT_GROUPS, tm)
    gmask = jnp.zeros(grp.shape, jnp.bool_)
    for _ in range(TOPK_GROUPS):
        hit, _ = _first_argmax_mask(jnp.where(gmask, -jnp.inf, grp), 0)
        gmask = gmask | hit
    emask = jnp.broadcast_to(gmask.reshape(N_EXPERT_GROUPS, 1, tm), s3.shape).reshape(ne, tm)
    cand = jnp.where(emask, sel, -jnp.inf)
    chosen = jnp.zeros(cand.shape, jnp.bool_)
    hits = []
    for _ in range(TOP_K):
        hit, _ = _first_argmax_mask(jnp.where(chosen, -jnp.inf, cand), 0)
        hit = hit & ~chosen
        hits.append(hit)
        chosen = chosen | hit
    wsel = jnp.where(chosen, scores, 0.0)
    gates = wsel / jnp.sum(wsel, axis=0, keepdims=True) * ROUTED_SCALE

    r_i = lax.broadcasted_iota(I32, (tm, tm), 0)
    c_i = lax.broadcasted_iota(I32, (tm, tm), 1)
    tri = jnp.where(r_i <= c_i, 1.0, 0.0).astype(BF16)
    csum = _dot(jnp.where(chosen, 1.0, 0.0).astype(BF16), tri)
    carry = carry_ref[:, 0:1]
    rank = carry + csum - 1.0
    eidx = lax.broadcasted_iota(I32, (ne, tm), 0).astype(F32)

    def pick(hit, v):
        return jnp.sum(jnp.where(hit, v, 0.0), axis=0, keepdims=True)

    ek_ref[...] = jnp.concatenate([pick(h, eidx) for h in hits], axis=0).astype(I32)
    rk_ref[...] = jnp.concatenate([pick(h, rank) for h in hits], axis=0).astype(I32)
    gk_rows = jnp.concatenate([pick(h, gates) for h in hits] + [jnp.zeros((LANES - TOP_K, tm), F32)], axis=0)
    gk_ref[...] = gk_rows.T
    new_carry = carry + csum[:, tm - 1:tm]
    carry_ref[...] = jnp.broadcast_to(new_carry, carry_ref.shape)
    cnt_ref[...] = jnp.broadcast_to(new_carry, cnt_ref.shape).astype(I32)


def _router_call(logits_t, router_bias_col, tm=512):
    ne, t = logits_t.shape
    return pl.pallas_call(
        _router_kernel,
        grid=(t // tm,),
        in_specs=[
            pl.BlockSpec((ne, tm), lambda i: (0, i)),
            pl.BlockSpec((ne, 1), lambda i: (0, 0)),
        ],
        out_specs=[
            pl.BlockSpec((TOP_K, tm), lambda i: (0, i)),
            pl.BlockSpec((TOP_K, tm), lambda i: (0, i)),
            pl.BlockSpec((tm, LANES), lambda i: (i, 0)),
            pl.BlockSpec((ne, LANES), lambda i: (0, 0)),
        ],
        out_shape=[
            jax.ShapeDtypeStruct((TOP_K, t), I32),
            jax.ShapeDtypeStruct((TOP_K, t), I32),
            jax.ShapeDtypeStruct((t, LANES), F32),
            jax.ShapeDtypeStruct((ne, LANES), I32),
        ],
        scratch_shapes=[pltpu.VMEM((ne, LANES), F32)],
        compiler_params=pltpu.CompilerParams(dimension_semantics=("arbitrary",)),
        name="router",
    )(logits_t, router_bias_col)


def _routing_tables(ek, rk, counts, n_tiles_max, tokens_per_step):
    tg = EXPERT_ROW_TILE
    padded = (counts + tg - 1) // tg * tg
    ends = jnp.cumsum(padded)
    offsets = ends - padded
    expert_ids = jnp.arange(counts.shape[0], dtype=I32)
    dest = rk + jnp.sum(jnp.where(ek[..., None] == expert_ids, offsets, 0), axis=-1)
    tile_ends = ends // tg
    n_valid = tile_ends[-1]
    tile_ids = jnp.arange(n_tiles_max, dtype=I32)
    tile_expert = jnp.sum(tile_ids[:, None] >= tile_ends[None, :], axis=1).astype(I32)
    last_expert = jnp.sum(n_valid - 1 >= tile_ends).astype(I32)
    tile_expert = jnp.where(tile_ids < n_valid, tile_expert, last_expert)
    k, t = dest.shape
    dest_steps = dest.reshape(k, t // tokens_per_step, tokens_per_step).transpose(1, 0, 2).reshape(-1)
    return dest, dest_steps, tile_expert, n_valid.reshape(1).astype(I32), (offsets + counts).astype(I32), \
        (padded - counts).astype(I32)


def _pad_pieces():
    return [1 << b for b in reversed(range(SUBLANES.bit_length() - 1, EXPERT_ROW_TILE.bit_length() - 1))]


def _dispatch_kernel(pad_start_ref, pad_len_ref, dest_hbm, t_hbm, xs_hbm, idx_smem, zero_ref,
                     sem_idx, sem_pad, sem_row, *, td):
    i = pl.program_id(0)
    n_idx = TOP_K * td
    idx_copy = pltpu.make_async_copy(dest_hbm.at[pl.ds(pl.multiple_of(i * n_idx, n_idx), n_idx)], idx_smem, sem_idx)
    idx_copy.start()

    @pl.when(i == 0)
    def _():
        zero_ref[...] = jnp.zeros_like(zero_ref)
        n_exp = pad_start_ref.shape[0]

        def walk(e, carry, wait):
            pos = pad_start_ref[e]
            n = pad_len_ref[e]
            head = n & (SUBLANES - 1)

            def head_row(r, c):
                cp = pltpu.make_async_copy(zero_ref.at[0], xs_hbm.at[pos + r], sem_pad)
                cp.wait() if wait else cp.start()
                return c

            lax.fori_loop(0, head, head_row, 0)
            pos = pos + head
            for piece in _pad_pieces():
                has = (n & piece) != 0

                @pl.when(has)
                def _():
                    cp = pltpu.make_async_copy(zero_ref.at[pl.ds(0, piece)],
                                               xs_hbm.at[pl.ds(pl.multiple_of(pos, SUBLANES), piece)], sem_pad)
                    cp.wait() if wait else cp.start()

                pos = pos + jnp.where(has, piece, 0)
            return carry

        tail_rows = zero_ref.shape[0]
        tail_start = pad_start_ref[n_exp - 1] + pad_len_ref[n_exp - 1]

        def tail(b, carry, wait):
            cp = pltpu.make_async_copy(
                zero_ref, xs_hbm.at[pl.ds(pl.multiple_of(tail_start + b * tail_rows, tail_rows), tail_rows)], sem_pad)
            cp.wait() if wait else cp.start()
            return carry

        n_tail = (xs_hbm.shape[0] - tail_start) // tail_rows
        lax.fori_loop(0, n_exp, functools.partial(walk, wait=False), 0)
        lax.fori_loop(0, n_tail, functools.partial(tail, wait=False), 0)
        lax.fori_loop(0, n_exp, functools.partial(walk, wait=True), 0)
        lax.fori_loop(0, n_tail, functools.partial(tail, wait=True), 0)

    idx_copy.wait()
    base = i * td

    def issue(tb, carry):
        for u in range(ISSUE_UNROLL):
            t = tb * ISSUE_UNROLL + u
            for k in range(TOP_K):
                pltpu.make_async_copy(t_hbm.at[base + t], xs_hbm.at[idx_smem[k * td + t]], sem_row).start()
        return carry

    lax.fori_loop(0, td // ISSUE_UNROLL, issue, 0)

    def drain(k, carry):
        pltpu.make_async_copy(t_hbm.at[pl.ds(0, td)], xs_hbm.at[pl.ds(0, td)], sem_row).wait()
        return carry

    lax.fori_loop(0, TOP_K, drain, 0)


def _dispatch_call(pad_start, pad_len, dest_steps, t_f32, n_rows):
    t, d = t_f32.shape
    td = DISPATCH_TOKENS
    grid_spec = pltpu.PrefetchScalarGridSpec(
        num_scalar_prefetch=2,
        grid=(t // td,),
        in_specs=[pl.BlockSpec(memory_space=pl.ANY), pl.BlockSpec(memory_space=pl.ANY)],
        out_specs=pl.BlockSpec(memory_space=pl.ANY),
        scratch_shapes=[
            pltpu.SMEM((TOP_K * td,), I32),
            pltpu.VMEM((EXPERT_ROW_TILE // 2, d), F32),
            pltpu.SemaphoreType.DMA(()),
            pltpu.SemaphoreType.DMA(()),
            pltpu.SemaphoreType.DMA(()),
        ],
    )
    return pl.pallas_call(
        functools.partial(_dispatch_kernel, td=td),
        grid_spec=grid_spec,
        out_shape=jax.ShapeDtypeStruct((n_rows, d), F32),
        compiler_params=pltpu.CompilerParams(dimension_semantics=("arbitrary",)),
        name="dispatch",
    )(pad_start, pad_len, dest_steps, t_f32)


def _experts_kernel(te_ref, nv_ref, xs_ref, wg_ref, wu_ref, wd_ref, y_ref, wg_s, wu_s, wd_s):
    i = pl.program_id(0)

    @pl.when(i < nv_ref[0])
    def _():
        @pl.when((i == 0) | (te_ref[i] != te_ref[jnp.maximum(i - 1, 0)]))
        def _():
            wg_s[...] = wg_ref[0].astype(BF16)
            wu_s[...] = wu_ref[0].astype(BF16)
            wd_s[...] = wd_ref[0].astype(BF16)

        x = xs_ref[...].astype(BF16)
        hid = _silu(_dot(x, wg_s[...])) * _dot(x, wu_s[...])
        y_ref[...] = _dot(hid.astype(BF16), wd_s[...])

    @pl.when(i >= nv_ref[0])
    def _():
        y_ref[...] = jnp.zeros_like(y_ref)


def _experts_call(tile_expert, n_valid, xs, w_gate, w_up, w_down):
    n_rows, d = xs.shape
    ne, _, df = w_gate.shape
    tg = EXPERT_ROW_TILE
    row_map = lambda i, te, nv: (jnp.minimum(i, nv[0] - 1), 0)
    grid_spec = pltpu.PrefetchScalarGridSpec(
        num_scalar_prefetch=2,
        grid=(n_rows // tg,),
        in_specs=[
            pl.BlockSpec((tg, d), row_map),
            pl.BlockSpec((1, d, df), lambda i, te, nv: (te[i], 0, 0)),
            pl.BlockSpec((1, d, df), lambda i, te, nv: (te[i], 0, 0)),
            pl.BlockSpec((1, df, d), lambda i, te, nv: (te[i], 0, 0)),
        ],
        out_specs=pl.BlockSpec((tg, d), lambda i, te, nv: (i, 0)),
        scratch_shapes=[
            pltpu.VMEM((d, df), BF16),
            pltpu.VMEM((d, df), BF16),
            pltpu.VMEM((df, d), BF16),
        ],
    )
    return pl.pallas_call(
        _experts_kernel,
        grid_spec=grid_spec,
        out_shape=jax.ShapeDtypeStruct((n_rows, d), F32),
        compiler_params=pltpu.CompilerParams(
            dimension_semantics=("arbitrary",), vmem_limit_bytes=_vmem_limit(56 << 20)),
        name="experts",
    )(tile_expert, n_valid, xs, w_gate, w_up, w_down)


def _combine_kernel(dest_hbm, y_hbm, gk_ref, t_ref, wsg_ref, wsu_ref, wsd_ref, x1_ref, gpost_ref, gate_ref, o_ref,
                    idx_smem, rows_ref, sem_idx, sem_row, *, tc):
    i = pl.program_id(0)
    n_idx = TOP_K * tc
    idx_copy = pltpu.make_async_copy(dest_hbm.at[pl.ds(pl.multiple_of(i * n_idx, n_idx), n_idx)], idx_smem, sem_idx)
    idx_copy.start()
    idx_copy.wait()

    def issue(tb, carry):
        for u in range(ISSUE_UNROLL):
            t = tb * ISSUE_UNROLL + u
            for k in range(TOP_K):
                pltpu.make_async_copy(y_hbm.at[idx_smem[k * tc + t]], rows_ref.at[k, t], sem_row).start()
        return carry

    lax.fori_loop(0, tc // ISSUE_UNROLL, issue, 0)

    t = t_ref[...].astype(BF16)
    hid = _silu(_dot(t, wsg_ref[...])) * _dot(t, wsu_ref[...])
    f = _dot(hid.astype(BF16), wsd_ref[...])

    for k in range(TOP_K):
        pltpu.make_async_copy(y_hbm.at[pl.ds(0, tc)], rows_ref.at[k], sem_row).wait()
    g = gk_ref[...]
    for k in range(TOP_K):
        f = f + g[:, k:k + 1] * rows_ref[k]
    o_ref[...] = x1_ref[...] + gate_ref[0] * (_rms(f) * gpost_ref[...])


def _combine_call(dest_steps, y, gk, t_f32, wsg_bf, wsu_bf, wsd_bf, x1, g_post, mod3, seq):
    t, d = x1.shape
    tc = COMBINE_TOKENS
    tiles_per_seq = seq // tc
    return pl.pallas_call(
        functools.partial(_combine_kernel, tc=tc),
        grid=(t // tc,),
        in_specs=[
            pl.BlockSpec(memory_space=pl.ANY),
            pl.BlockSpec(memory_space=pl.ANY),
            pl.BlockSpec((tc, LANES), lambda i: (i, 0)),
            pl.BlockSpec((tc, d), lambda i: (i, 0)),
            pl.BlockSpec(wsg_bf.shape, lambda i: (0, 0)),
            pl.BlockSpec(wsu_bf.shape, lambda i: (0, 0)),
            pl.BlockSpec(wsd_bf.shape, lambda i: (0, 0)),
            pl.BlockSpec((tc, d), lambda i: (i, 0)),
            pl.BlockSpec((1, d), lambda i: (0, 0)),
            pl.BlockSpec((1, 1, d), lambda i: (i // tiles_per_seq, 0, 5)),
        ],
        out_specs=pl.BlockSpec((tc, d), lambda i: (i, 0)),
        out_shape=jax.ShapeDtypeStruct((t, d), F32),
        scratch_shapes=[
            pltpu.SMEM((TOP_K * tc,), I32),
            pltpu.VMEM((TOP_K, tc, d), F32),
            pltpu.SemaphoreType.DMA(()),
            pltpu.SemaphoreType.DMA(()),
        ],
        compiler_params=pltpu.CompilerParams(
            dimension_semantics=("arbitrary",), vmem_limit_bytes=_vmem_limit(48 << 20)),
        name="combine",
    )(dest_steps, y, gk, t_f32, wsg_bf, wsu_bf, wsd_bf, x1, g_post, mod3)


def kernel(x, c, ctx, c_ctx, w_mod, b_mod, g_pre_mix, g_post_mix, g_pre_ffn, g_post_ffn, w_in, conv_w, conv_b,
           rpb, g_conv_out, g_attn_out, w_out, w_router, router_bias, w_exp_gate, w_exp_up, w_exp_down,
           w_sh_gate, w_sh_up, w_sh_down):
    batch, seq, d = x.shape
    ctx_len = ctx.shape[1]
    n_tok = batch * seq
    assert w_mod.shape[0] == 1, "single-layer kernel"
    assert seq % (Q_ROWS * GRID_W) == 0 and seq // GRID_W >= K_ROWS
    assert w_router.shape[2] == N_EXPERTS and n_tok % DISPATCH_TOKENS == 0 and seq % COMBINE_TOKENS == 0
    d_attn = N_HEADS * HEAD_DIM
    kv_offset = w_in.shape[2] - 2 * d_attn
    row2 = lambda a: a.reshape(1, -1)

    x2 = x.reshape(n_tok, d)
    ctx2 = ctx.reshape(batch * ctx_len, d)

    n_rows = -(-(batch + 1) // 8) * 8
    cvec = jnp.concatenate([c, c_ctx[None], jnp.zeros((n_rows - batch - 1, d), F32)], axis=0)
    mod = _mod_call(cvec, w_mod[0], row2(b_mod[0]))
    mod3 = mod.reshape(n_rows, 1, 6 * d)

    w_in_bf = w_in[0].astype(BF16)
    tm_in = 1024
    tiles_per_seq = seq // tm_in
    proj = _inproj_call(x2, row2(g_pre_mix[0]), mod3, w_in_bf, lambda i: i // tiles_per_seq,
                        tm_in, 1024, "in_proj")
    proj_ctx = _inproj_call(ctx2, row2(g_pre_mix[0]), mod3, w_in_bf[:, kv_offset:], lambda i: batch,
                            min(tm_in, ctx2.shape[0]), 1024, "in_proj_ctx")

    cos_t, sin_t = _rope_tables(seq)
    bias = _bias_table(rpb[0], seq // GRID_W)
    ya = _attn_call(proj, proj_ctx, cos_t, sin_t, bias, row2(g_attn_out[0]), batch, seq, ctx_len)
    yc = _conv_call(proj, conv_w[0], row2(conv_b[0]), row2(g_conv_out[0]), batch, seq)

    x1, t_f32, logits_t = _outproj_call(yc, ya, w_out[0].astype(BF16), x2, row2(g_post_mix[0]),
                                        row2(g_pre_ffn[0]), mod3, w_router[0].T, seq)
    ek, rk, gk, cnt = _router_call(logits_t, router_bias[0].reshape(-1, 1))

    n_tiles_max = n_tok * TOP_K // EXPERT_ROW_TILE + N_EXPERTS
    n_sorted = n_tiles_max * EXPERT_ROW_TILE
    _, dest_d, tile_expert, n_valid, pad_start, pad_len = _routing_tables(
        ek, rk, cnt[:, 0], n_tiles_max, DISPATCH_TOKENS)
    dest_c = dest_d.reshape(n_tok // DISPATCH_TOKENS, TOP_K, DISPATCH_TOKENS // COMBINE_TOKENS, COMBINE_TOKENS)
    dest_c = dest_c.transpose(0, 2, 1, 3).reshape(-1)
    xs = _dispatch_call(pad_start, pad_len, dest_d, t_f32, n_sorted)
    y = _experts_call(tile_expert, n_valid, xs, w_exp_gate[0], w_exp_up[0], w_exp_down[0])
    out = _combine_call(dest_c, y, gk, t_f32, w_sh_gate[0].astype(BF16), w_sh_up[0].astype(BF16),
                        w_sh_down[0].astype(BF16), x1, row2(g_post_ffn[0]), mod3, seq)
    return out.reshape(batch, seq, d)
```

```python
import functools

import numpy as np
import jax
import jax.numpy as jnp
from jax import lax
from jax.experimental import pallas as pl
from jax.experimental.pallas import tpu as pltpu

F32 = jnp.float32
BF16 = jnp.bfloat16
I32 = jnp.int32

GRID_W = 64
N_HEADS = 8
HEAD_DIM = 128
NA_ROWS = 8
NA_COLS = 16
ROPE_THETA = 10000.0
CONV_GROUP_DIM = 128
N_EXPERTS = 64
N_EXPERT_GROUPS = 8
TOPK_GROUPS = 4
TOP_K = 8
ROUTED_SCALE = 2.5
EPS = 1e-6
MASK_VALUE = -1e30

Q_ROWS = 4
K_ROWS = 12

V7X_VMEM_BYTES = 64 * 1024 * 1024
LANES = 128
SUBLANES = 8

EXPERT_ROW_TILE = 512
DISPATCH_TOKENS = 1024
COMBINE_TOKENS = 128
ISSUE_UNROLL = 8


def _vmem_limit(nbytes):
    return int(min(nbytes, V7X_VMEM_BYTES - 6 * 1024 * 1024))


def _rms(x):
    return x * lax.rsqrt(jnp.mean(x * x, axis=-1, keepdims=True) + EPS)


def _silu(x):
    return x * jax.nn.sigmoid(x)


def _dot(a, b):
    return jnp.dot(a, b, preferred_element_type=F32)


def _dot_nt(a, b, precision=None):
    return lax.dot_general(a, b, (((1,), (1,)), ((), ())), preferred_element_type=F32, precision=precision)


def _mod_kernel(c_ref, w_ref, b_ref, o_ref):
    a = _silu(c_ref[...]).astype(BF16)
    o_ref[...] = _dot(a, w_ref[...].astype(BF16)) + b_ref[...]


def _mod_call(cvec, w_mod, b_mod, tn=1024):
    rows, d = cvec.shape
    n = w_mod.shape[1]
    return pl.pallas_call(
        _mod_kernel,
        grid=(n // tn,),
        in_specs=[
            pl.BlockSpec((rows, d), lambda j: (0, 0)),
            pl.BlockSpec((d, tn), lambda j: (0, j)),
            pl.BlockSpec((1, tn), lambda j: (0, j)),
        ],
        out_specs=pl.BlockSpec((rows, tn), lambda j: (0, j)),
        out_shape=jax.ShapeDtypeStruct((rows, n), F32),
        compiler_params=pltpu.CompilerParams(
            dimension_semantics=("arbitrary",), vmem_limit_bytes=_vmem_limit(40 << 20)),
        name="mod",
    )(cvec, w_mod, b_mod)


def _inproj_kernel(x_ref, g_ref, sh_ref, sc_ref, w_ref, o_ref, h_ref):
    @pl.when(pl.program_id(1) == 0)
    def _():
        h = _rms(x_ref[...]) * g_ref[...]
        h = h * (1.0 + sc_ref[0]) + sh_ref[0]
        h_ref[...] = h.astype(BF16)

    o_ref[...] = _dot(h_ref[...], w_ref[...])


def _inproj_call(x2, g, mod3, w_bf, row_of_tile, tm, tn, name):
    t, d = x2.shape
    n = w_bf.shape[1]
    return pl.pallas_call(
        _inproj_kernel,
        grid=(t // tm, n // tn),
        in_specs=[
            pl.BlockSpec((tm, d), lambda i, j: (i, 0)),
            pl.BlockSpec((1, d), lambda i, j: (0, 0)),
            pl.BlockSpec((1, 1, d), lambda i, j: (row_of_tile(i), 0, 0)),
            pl.BlockSpec((1, 1, d), lambda i, j: (row_of_tile(i), 0, 1)),
            pl.BlockSpec((d, tn), lambda i, j: (0, j)),
        ],
        out_specs=pl.BlockSpec((tm, tn), lambda i, j: (i, j)),
        out_shape=jax.ShapeDtypeStruct((t, n), F32),
        scratch_shapes=[pltpu.VMEM((tm, d), BF16)],
        compiler_params=pltpu.CompilerParams(
            dimension_semantics=("arbitrary", "arbitrary"), vmem_limit_bytes=_vmem_limit(52 << 20)),
        name=name,
    )(x2, g, mod3, mod3, w_bf)


def _attn_kernel(q_ref, k_ref, v_ref, kc_ref, vc_ref, cos_ref, sin_ref, bias_ref, g_ref, o_ref,
                 qs_ref, ks_ref, vs_ref, kcs_ref, vcs_ref):
    cos = cos_ref[...]
    sin = sin_ref[...]
    lane = lax.broadcasted_iota(I32, cos.shape, 1)
    first_half = (lane % (HEAD_DIM // 2)) < (HEAD_DIM // 4)

    def rope(x):
        partner = jnp.where(first_half, pltpu.roll(x, HEAD_DIM - HEAD_DIM // 4, 1),
                            pltpu.roll(x, HEAD_DIM // 4, 1))
        return x * cos + partner * sin

    qs_ref[...] = rope(q_ref[...]).astype(BF16)
    ks_ref[...] = rope(k_ref[...]).astype(BF16)
    vs_ref[...] = v_ref[...].astype(BF16)
    kcs_ref[...] = kc_ref[...].astype(BF16)
    vcs_ref[...] = vc_ref[...].astype(BF16)

    scale = HEAD_DIM ** -0.5
    qb = Q_ROWS * GRID_W
    kb = K_ROWS * GRID_W
    n_blocks = q_ref.shape[0] // qb
    max_start = q_ref.shape[0] // GRID_W - K_ROWS

    def body(j, carry):
        q0 = pl.multiple_of(j * qb, qb)
        k_row = jnp.clip(j * Q_ROWS - NA_ROWS // 2, 0, max_start)
        k0 = pl.multiple_of(k_row * GRID_W, qb)
        q = qs_ref[pl.ds(q0, qb), :]
        s_loc = _dot_nt(q, ks_ref[pl.ds(k0, kb), :]) * scale + bias_ref[0, j]
        s_ctx = _dot_nt(q, kcs_ref[...]) * scale
        m = jnp.maximum(jnp.max(s_loc, axis=-1, keepdims=True), jnp.max(s_ctx, axis=-1, keepdims=True))
        p_loc = jnp.exp(s_loc - m)
        p_ctx = jnp.exp(s_ctx - m)
        denom = jnp.sum(p_loc, axis=-1, keepdims=True) + jnp.sum(p_ctx, axis=-1, keepdims=True)
        o = _dot(p_loc.astype(BF16), vs_ref[pl.ds(k0, kb), :]) + _dot(p_ctx.astype(BF16), vcs_ref[...])
        o = o / denom
        o_ref[pl.ds(q0, qb), :] = (_rms(o) * g_ref[...]).astype(BF16)
        return carry

    lax.fori_loop(0, n_blocks, body, 0)


def _attn_call(proj, proj_ctx, cos_t, sin_t, bias, g_attn, batch, seq, ctx_len):
    d_attn = N_HEADS * HEAD_DIM
    q_blk = (proj.shape[1] - 3 * d_attn) // HEAD_DIM
    n_blocks = seq // (Q_ROWS * GRID_W)
    qb, kb = Q_ROWS * GRID_W, K_ROWS * GRID_W
    return pl.pallas_call(
        _attn_kernel,
        grid=(N_HEADS, batch),
        in_specs=[
            pl.BlockSpec((seq, HEAD_DIM), lambda h, b: (b, q_blk + h)),
            pl.BlockSpec((seq, HEAD_DIM), lambda h, b: (b, q_blk + N_HEADS + h)),
            pl.BlockSpec((seq, HEAD_DIM), lambda h, b: (b, q_blk + 2 * N_HEADS + h)),
            pl.BlockSpec((ctx_len, HEAD_DIM), lambda h, b: (b, h)),
            pl.BlockSpec((ctx_len, HEAD_DIM), lambda h, b: (b, N_HEADS + h)),
            pl.BlockSpec((seq, HEAD_DIM), lambda h, b: (0, 0)),
            pl.BlockSpec((seq, HEAD_DIM), lambda h, b: (0, 0)),
            pl.BlockSpec((1, n_blocks, qb, kb), lambda h, b: (h, 0, 0, 0)),
            pl.BlockSpec((1, HEAD_DIM), lambda h, b: (0, h)),
        ],
        out_specs=pl.BlockSpec((seq, HEAD_DIM), lambda h, b: (b, h)),
        out_shape=jax.ShapeDtypeStruct((batch * seq, d_attn), BF16),
        scratch_shapes=[
            pltpu.VMEM((seq, HEAD_DIM), BF16),
            pltpu.VMEM((seq, HEAD_DIM), BF16),
            pltpu.VMEM((seq, HEAD_DIM), BF16),
            pltpu.VMEM((ctx_len, HEAD_DIM), BF16),
            pltpu.VMEM((ctx_len, HEAD_DIM), BF16),
        ],
        compiler_params=pltpu.CompilerParams(
            dimension_semantics=("arbitrary", "arbitrary"), vmem_limit_bytes=_vmem_limit(48 << 20)),
        name="attn",
    )(proj, proj, proj, proj_ctx, proj_ctx, cos_t, sin_t, bias, g_attn)


def _rope_tables(seq):
    t = jnp.arange(seq, dtype=I32)
    pos = jnp.stack([t // GRID_W, t % GRID_W], axis=-1).astype(F32)
    n_freq = HEAD_DIM // 4
    inv_freq = ROPE_THETA ** (-jnp.arange(n_freq, dtype=F32) / n_freq)
    ang = pos[:, :, None] * inv_freq
    cos, sin = jnp.cos(ang), jnp.sin(ang)
    cos_t = jnp.concatenate([cos, cos], axis=-1).reshape(seq, HEAD_DIM)
    sin_t = jnp.concatenate([-sin, sin], axis=-1).reshape(seq, HEAD_DIM)
    return cos_t, sin_t


def _bias_table(rpb, rows):
    w = GRID_W
    qc = np.arange(w)[:, None]
    kc = np.arange(w)[None, :]
    c0 = np.clip(qc - NA_COLS // 2, 0, w - NA_COLS)
    col_ok = (kc >= c0) & (kc < c0 + NA_COLS)
    col_off = np.clip(kc - qc + NA_COLS - 1, 0, 2 * NA_COLS - 2)
    toep = jnp.where(col_ok[None, None], rpb[:, :, col_off], MASK_VALUE)
    masked = jnp.full((rpb.shape[0], 1, w, w), MASK_VALUE, F32)
    toep = jnp.concatenate([toep, masked], axis=1)
    n_blocks = rows // Q_ROWS
    kh = min(NA_ROWS, rows)
    sel = np.zeros((n_blocks, Q_ROWS, K_ROWS), np.int32)
    for j in range(n_blocks):
        k_start = int(np.clip(j * Q_ROWS - NA_ROWS // 2, 0, rows - K_ROWS))
        for a in range(Q_ROWS):
            r = j * Q_ROWS + a
            r0 = int(np.clip(r - kh // 2, 0, rows - kh))
            for b in range(K_ROWS):
                kr = k_start + b
                sel[j, a, b] = kr - r + NA_ROWS - 1 if r0 <= kr < r0 + kh else 2 * NA_ROWS - 1
    blocks = toep[:, sel]
    blocks = jnp.transpose(blocks, (0, 1, 2, 4, 3, 5))
    return blocks.reshape(rpb.shape[0], n_blocks, Q_ROWS * w, K_ROWS * w)


def _conv_kernel(ub_ref, uc_ref, ux_ref, w_ref, b_ref, g_ref, o_ref):
    p = uc_ref[...] * ux_ref[...]
    seq = p.shape[0]
    row = lax.broadcasted_iota(I32, p.shape, 0)
    prev = jnp.where(row == 0, 0.0, pltpu.roll(p, 1, 0))
    nxt = jnp.where(row == seq - 1, 0.0, pltpu.roll(p, seq - 1, 0))
    w = w_ref[...]
    conv = prev * w[0:1] + p * w[1:2] + nxt * w[2:3] + b_ref[...]
    y = ub_ref[...] * conv
    g = g_ref[...]
    for c0 in range(0, y.shape[1], CONV_GROUP_DIM):
        sl = slice(c0, c0 + CONV_GROUP_DIM)
        o_ref[:, sl] = (_rms(y[:, sl]) * g[:, sl]).astype(BF16)


def _conv_call(proj, conv_w, conv_b, g_conv, batch, seq, cb=256):
    d_conv = conv_w.shape[1]
    nc = d_conv // cb
    return pl.pallas_call(
        _conv_kernel,
        grid=(batch, nc),
        in_specs=[
            pl.BlockSpec((seq, cb), lambda b, c: (b, c)),
            pl.BlockSpec((seq, cb), lambda b, c: (b, nc + c)),
            pl.BlockSpec((seq, cb), lambda b, c: (b, 2 * nc + c)),
            pl.BlockSpec((conv_w.shape[0], cb), lambda b, c: (0, c)),
            pl.BlockSpec((1, cb), lambda b, c: (0, c)),
            pl.BlockSpec((1, cb), lambda b, c: (0, c)),
        ],
        out_specs=pl.BlockSpec((seq, cb), lambda b, c: (b, c)),
        out_shape=jax.ShapeDtypeStruct((batch * seq, d_conv), BF16),
        compiler_params=pltpu.CompilerParams(
            dimension_semantics=("arbitrary", "arbitrary"), vmem_limit_bytes=_vmem_limit(48 << 20)),
        name="conv",
    )(proj, proj, proj, conv_w, conv_b, g_conv)


def _outproj_kernel(yc_ref, ya_ref, w_ref, x_ref, gpost_ref, gate_ref, gpre_ref, sh_ref, sc_ref, wr_ref,
                    x1_ref, t_ref, lg_ref):
    dc = yc_ref.shape[1]
    y = _dot(yc_ref[...], w_ref[0:dc, :]) + _dot(ya_ref[...], w_ref[dc:, :])
    x1 = x_ref[...] + gate_ref[0] * (_rms(y) * gpost_ref[...])
    x1_ref[...] = x1
    t = (_rms(x1) * gpre_ref[...]) * (1.0 + sc_ref[0]) + sh_ref[0]
    t_ref[...] = t
    wr = wr_ref[...]
    wr_hi = wr.astype(BF16)
    wr_lo = (wr - wr_hi.astype(F32)).astype(BF16)
    t_hi = t.astype(BF16)
    t_lo = (t - t_hi.astype(F32)).astype(BF16)
    logits = _dot_nt(t_hi, wr_hi) + (_dot_nt(t_lo, wr_hi) + _dot_nt(t_hi, wr_lo))
    lg_ref[...] = logits.T


def _outproj_call(yc, ya, w_out_bf, x2, g_post, g_pre, mod3, w_router_t, seq, tm=512):
    t, d = x2.shape
    dc = yc.shape[1]
    ne = w_router_t.shape[0]
    tiles_per_seq = seq // tm
    row = lambda i: i // tiles_per_seq
    return pl.pallas_call(
        _outproj_kernel,
        grid=(t // tm,),
        in_specs=[
            pl.BlockSpec((tm, dc), lambda i: (i, 0)),
            pl.BlockSpec((tm, ya.shape[1]), lambda i: (i, 0)),
            pl.BlockSpec(w_out_bf.shape, lambda i: (0, 0), pipeline_mode=pl.Buffered(1)),
            pl.BlockSpec((tm, d), lambda i: (i, 0)),
            pl.BlockSpec((1, d), lambda i: (0, 0)),
            pl.BlockSpec((1, 1, d), lambda i: (row(i), 0, 2)),
            pl.BlockSpec((1, d), lambda i: (0, 0)),
            pl.BlockSpec((1, 1, d), lambda i: (row(i), 0, 3)),
            pl.BlockSpec((1, 1, d), lambda i: (row(i), 0, 4)),
            pl.BlockSpec((ne, d), lambda i: (0, 0)),
        ],
        out_specs=[
            pl.BlockSpec((tm, d), lambda i: (i, 0)),
            pl.BlockSpec((tm, d), lambda i: (i, 0)),
            pl.BlockSpec((ne, tm), lambda i: (0, i)),
        ],
        out_shape=[
            jax.ShapeDtypeStruct((t, d), F32),
            jax.ShapeDtypeStruct((t, d), F32),
            jax.ShapeDtypeStruct((ne, t), F32),
        ],
        compiler_params=pltpu.CompilerParams(
            dimension_semantics=("arbitrary",), vmem_limit_bytes=_vmem_limit(57 << 20)),
        name="out_proj",
    )(yc, ya, w_out_bf, x2, g_post, mod3, g_pre, mod3, mod3, w_router_t)


def _first_argmax_mask(v, axis):
    m = jnp.max(v, axis=axis, keepdims=True)
    idx = lax.broadcasted_iota(I32, v.shape, axis)
    big = v.shape[axis]
    first = jnp.min(jnp.where(v == m, idx, big), axis=axis, keepdims=True)
    return idx == first, m


def _router_kernel(lg_ref, rb_ref, ek_ref, rk_ref, gk_ref, cnt_ref, carry_ref):
    ne, tm = lg_ref.shape
    per = ne // N_EXPERT_GROUPS

    @pl.when(pl.program_id(0) == 0)
    def _():
        carry_ref[...] = jnp.zeros_like(carry_ref)

    scores = jax.nn.sigmoid(lg_ref[...])
    sel = scores + rb_ref[...]
    s3 = sel.reshape(N_EXPERT_GROUPS, per, tm)
    hit1, m1 = _first_argmax_mask(s3, 1)
    m2 = jnp.max(jnp.where(hit1, -jnp.inf, s3), axis=1, keepdims=True)
    grp = (m1 + m2).reshape(N_EXPERT_GROUPS, tm)
    gmask = jnp.zeros(grp.shape, jnp.bool_)
    for _ in range(TOPK_GROUPS):
        hit, _ = _first_argmax_mask(jnp.where(gmask, -jnp.inf, grp), 0)
        gmask = gmask | hit
    emask = jnp.broadcast_to(gmask.reshape(N_EXPERT_GROUPS, 1, tm), s3.shape).reshape(ne, tm)
    cand = jnp.where(emask, sel, -jnp.inf)
    chosen = jnp.zeros(cand.shape, jnp.bool_)
    hits = []
    for _ in range(TOP_K):
        hit, _ = _first_argmax_mask(jnp.where(chosen, -jnp.inf, cand), 0)
        hit = hit & ~chosen
        hits.append(hit)
        chosen = chosen | hit
    wsel = jnp.where(chosen, scores, 0.0)
    gates = wsel / jnp.sum(wsel, axis=0, keepdims=True) * ROUTED_SCALE

    r_i = lax.broadcasted_iota(I32, (tm, tm), 0)
    c_i = lax.broadcasted_iota(I32, (tm, tm), 1)
    tri = jnp.where(r_i <= c_i, 1.0, 0.0).astype(BF16)
    csum = _dot(jnp.where(chosen, 1.0, 0.0).astype(BF16), tri)
    carry = carry_ref[:, 0:1]
    rank = carry + csum - 1.0
    eidx = lax.broadcasted_iota(I32, (ne, tm), 0).astype(F32)

    def pick(hit, v):
        return jnp.sum(jnp.where(hit, v, 0.0), axis=0, keepdims=True)

    ek_ref[...] = jnp.concatenate([pick(h, eidx) for h in hits], axis=0).astype(I32)
    rk_ref[...] = jnp.concatenate([pick(h, rank) for h in hits], axis=0).astype(I32)
    gk_rows = jnp.concatenate([pick(h, gates) for h in hits] + [jnp.zeros((LANES - TOP_K, tm), F32)], axis=0)
    gk_ref[...] = gk_rows.T
    new_carry = carry + csum[:, tm - 1:tm]
    carry_ref[...] = jnp.broadcast_to(new_carry, carry_ref.shape)
    cnt_ref[...] = jnp.broadcast_to(new_carry, cnt_ref.shape).astype(I32)


def _router_call(logits_t, router_bias_col, tm=512):
    ne, t = logits_t.shape
    return pl.pallas_call(
        _router_kernel,
        grid=(t // tm,),
        in_specs=[
            pl.BlockSpec((ne, tm), lambda i: (0, i)),
            pl.BlockSpec((ne, 1), lambda i: (0, 0)),
        ],
        out_specs=[
            pl.BlockSpec((TOP_K, tm), lambda i: (0, i)),
            pl.BlockSpec((TOP_K, tm), lambda i: (0, i)),
            pl.BlockSpec((tm, LANES), lambda i: (i, 0)),
            pl.BlockSpec((ne, LANES), lambda i: (0, 0)),
        ],
        out_shape=[
            jax.ShapeDtypeStruct((TOP_K, t), I32),
            jax.ShapeDtypeStruct((TOP_K, t), I32),
            jax.ShapeDtypeStruct((t, LANES), F32),
            jax.ShapeDtypeStruct((ne, LANES), I32),
        ],
        scratch_shapes=[pltpu.VMEM((ne, LANES), F32)],
        compiler_params=pltpu.CompilerParams(dimension_semantics=("arbitrary",)),
        name="router",
    )(logits_t, router_bias_col)


def _routing_tables(ek, rk, counts, n_tiles_max, tokens_per_step):
    tg = EXPERT_ROW_TILE
    padded = (counts + tg - 1) // tg * tg
    ends = jnp.cumsum(padded)
    offsets = ends - padded
    expert_ids = jnp.arange(counts.shape[0], dtype=I32)
    dest = rk + jnp.sum(jnp.where(ek[..., None] == expert_ids, offsets, 0), axis=-1)
    tile_ends = ends // tg
    n_valid = tile_ends[-1]
    tile_ids = jnp.arange(n_tiles_max, dtype=I32)
    tile_expert = jnp.sum(tile_ids[:, None] >= tile_ends[None, :], axis=1).astype(I32)
    last_expert = jnp.sum(n_valid - 1 >= tile_ends).astype(I32)
    tile_expert = jnp.where(tile_ids < n_valid, tile_expert, last_expert)
    k, t = dest.shape
    dest_steps = dest.reshape(k, t // tokens_per_step, tokens_per_step).transpose(1, 0, 2).reshape(-1)
    return dest, dest_steps, tile_expert, n_valid.reshape(1).astype(I32), (offsets + counts).astype(I32), \
        (padded - counts).astype(I32)


def _pad_pieces():
    return [1 << b for b in reversed(range(SUBLANES.bit_length() - 1, EXPERT_ROW_TILE.bit_length() - 1))]


def _dispatch_kernel(pad_start_ref, pad_len_ref, dest_hbm, t_ref, xs_hbm, idx_smem, zero_ref,
                     sem_idx, sem_pad, sem_row, *, td):
    i = pl.program_id(0)
    n_idx = TOP_K * td
    idx_copy = pltpu.make_async_copy(dest_hbm.at[pl.ds(pl.multiple_of(i * n_idx, n_idx), n_idx)], idx_smem, sem_idx)
    idx_copy.start()

    @pl.when(i == 0)
    def _():
        zero_ref[...] = jnp.zeros_like(zero_ref)
        n_exp = pad_start_ref.shape[0]

        def walk(e, carry, wait):
            pos = pad_start_ref[e]
            n = pad_len_ref[e]
            head = n & (SUBLANES - 1)

            def head_row(r, c):
                cp = pltpu.make_async_copy(zero_ref.at[0], xs_hbm.at[pos + r], sem_pad)
                cp.wait() if wait else cp.start()
                return c

            lax.fori_loop(0, head, head_row, 0)
            pos = pos + head
            for piece in _pad_pieces():
                has = (n & piece) != 0

                @pl.when(has)
                def _():
                    cp = pltpu.make_async_copy(zero_ref.at[pl.ds(0, piece)],
                                               xs_hbm.at[pl.ds(pl.multiple_of(pos, SUBLANES), piece)], sem_pad)
                    cp.wait() if wait else cp.start()

                pos = pos + jnp.where(has, piece, 0)
            return carry

        tail_rows = zero_ref.shape[0]
        tail_start = pad_start_ref[n_exp - 1] + pad_len_ref[n_exp - 1]

        def tail(b, carry, wait):
            cp = pltpu.make_async_copy(
                zero_ref, xs_hbm.at[pl.ds(pl.multiple_of(tail_start + b * tail_rows, tail_rows), tail_rows)], sem_pad)
            cp.wait() if wait else cp.start()
            return carry

        n_tail = (xs_hbm.shape[0] - tail_start) // tail_rows
        lax.fori_loop(0, n_exp, functools.partial(walk, wait=False), 0)
        lax.fori_loop(0, n_tail, functools.partial(tail, wait=False), 0)
        lax.fori_loop(0, n_exp, functools.partial(walk, wait=True), 0)
        lax.fori_loop(0, n_tail, functools.partial(tail, wait=True), 0)

    idx_copy.wait()

    def issue(tb, carry):
        for u in range(ISSUE_UNROLL):
            t = tb * ISSUE_UNROLL + u
            for k in range(TOP_K):
                pltpu.make_async_copy(t_ref.at[t], xs_hbm.at[idx_smem[k * td + t]], sem_row).start()
        return carry

    lax.fori_loop(0, td // ISSUE_UNROLL, issue, 0)

    def drain(k, carry):
        pltpu.make_async_copy(t_ref, xs_hbm.at[pl.ds(0, td)], sem_row).wait()
        return carry

    lax.fori_loop(0, TOP_K, drain, 0)


def _dispatch_call(pad_start, pad_len, dest_steps, t_f32, n_rows):
    t, d = t_f32.shape
    td = DISPATCH_TOKENS
    grid_spec = pltpu.PrefetchScalarGridSpec(
        num_scalar_prefetch=2,
        grid=(t // td,),
        in_specs=[pl.BlockSpec(memory_space=pl.ANY), pl.BlockSpec((td, d), lambda i, ps, pn: (i, 0))],
        out_specs=pl.BlockSpec(memory_space=pl.ANY),
        scratch_shapes=[
            pltpu.SMEM((TOP_K * td,), I32),
            pltpu.VMEM((EXPERT_ROW_TILE // 2, d), F32),
            pltpu.SemaphoreType.DMA(()),
            pltpu.SemaphoreType.DMA(()),
            pltpu.SemaphoreType.DMA(()),
        ],
    )
    return pl.pallas_call(
        functools.partial(_dispatch_kernel, td=td),
        grid_spec=grid_spec,
        out_shape=jax.ShapeDtypeStruct((n_rows, d), F32),
        compiler_params=pltpu.CompilerParams(dimension_semantics=("arbitrary",)),
        name="dispatch",
    )(pad_start, pad_len, dest_steps, t_f32)


def _experts_kernel(te_ref, nv_ref, xs_ref, wg_ref, wu_ref, wd_ref, y_ref, wg_s, wu_s, wd_s):
    i = pl.program_id(0)

    @pl.when(i < nv_ref[0])
    def _():
        @pl.when((i == 0) | (te_ref[i] != te_ref[jnp.maximum(i - 1, 0)]))
        def _():
            wg_s[...] = wg_ref[0].astype(BF16)
            wu_s[...] = wu_ref[0].astype(BF16)
            wd_s[...] = wd_ref[0].astype(BF16)

        x = xs_ref[...].astype(BF16)
        hid = _silu(_dot(x, wg_s[...])) * _dot(x, wu_s[...])
        y_ref[...] = _dot(hid.astype(BF16), wd_s[...])

    @pl.when(i >= nv_ref[0])
    def _():
        y_ref[...] = jnp.zeros_like(y_ref)


def _experts_call(tile_expert, n_valid, xs, w_gate, w_up, w_down):
    n_rows, d = xs.shape
    ne, _, df = w_gate.shape
    tg = EXPERT_ROW_TILE
    row_map = lambda i, te, nv: (jnp.minimum(i, nv[0] - 1), 0)
    grid_spec = pltpu.PrefetchScalarGridSpec(
        num_scalar_prefetch=2,
        grid=(n_rows // tg,),
        in_specs=[
            pl.BlockSpec((tg, d), row_map),
            pl.BlockSpec((1, d, df), lambda i, te, nv: (te[i], 0, 0)),
            pl.BlockSpec((1, d, df), lambda i, te, nv: (te[i], 0, 0)),
            pl.BlockSpec((1, df, d), lambda i, te, nv: (te[i], 0, 0)),
        ],
        out_specs=pl.BlockSpec((tg, d), lambda i, te, nv: (i, 0)),
        scratch_shapes=[
            pltpu.VMEM((d, df), BF16),
            pltpu.VMEM((d, df), BF16),
            pltpu.VMEM((df, d), BF16),
        ],
    )
    return pl.pallas_call(
        _experts_kernel,
        grid_spec=grid_spec,
        out_shape=jax.ShapeDtypeStruct((n_rows, d), F32),
        compiler_params=pltpu.CompilerParams(
            dimension_semantics=("arbitrary",), vmem_limit_bytes=_vmem_limit(56 << 20)),
        name="experts",
    )(tile_expert, n_valid, xs, w_gate, w_up, w_down)


def _combine_kernel(dest_hbm, y_hbm, gk_ref, t_ref, wsg_ref, wsu_ref, wsd_ref, x1_ref, gpost_ref, gate_ref, o_ref,
                    idx_smem, rows_ref, sem_idx, sem_row, *, tc):
    i = pl.program_id(0)
    n_idx = TOP_K * tc
    idx_copy = pltpu.make_async_copy(dest_hbm.at[pl.ds(pl.multiple_of(i * n_idx, n_idx), n_idx)], idx_smem, sem_idx)
    idx_copy.start()
    idx_copy.wait()

    def issue(tb, carry):
        for u in range(ISSUE_UNROLL):
            t = tb * ISSUE_UNROLL + u
            for k in range(TOP_K):
                pltpu.make_async_copy(y_hbm.at[idx_smem[k * tc + t]], rows_ref.at[k, t], sem_row).start()
        return carry

    lax.fori_loop(0, tc // ISSUE_UNROLL, issue, 0)

    t = t_ref[...].astype(BF16)
    hid = _silu(_dot(t, wsg_ref[...])) * _dot(t, wsu_ref[...])
    f = _dot(hid.astype(BF16), wsd_ref[...])

    for k in range(TOP_K):
        pltpu.make_async_copy(y_hbm.at[pl.ds(0, tc)], rows_ref.at[k], sem_row).wait()
    g = gk_ref[...]
    for k in range(TOP_K):
        f = f + g[:, k:k + 1] * rows_ref[k]
    o_ref[...] = x1_ref[...] + gate_ref[0] * (_rms(f) * gpost_ref[...])


def _combine_call(dest_steps, y, gk, t_f32, wsg_bf, wsu_bf, wsd_bf, x1, g_post, mod3, seq):
    t, d = x1.shape
    tc = COMBINE_TOKENS
    tiles_per_seq = seq // tc
    return pl.pallas_call(
        functools.partial(_combine_kernel, tc=tc),
        grid=(t // tc,),
        in_specs=[
            pl.BlockSpec(memory_space=pl.ANY),
            pl.BlockSpec(memory_space=pl.ANY),
            pl.BlockSpec((tc, LANES), lambda i: (i, 0)),
            pl.BlockSpec((tc, d), lambda i: (i, 0)),
            pl.BlockSpec(wsg_bf.shape, lambda i: (0, 0)),
            pl.BlockSpec(wsu_bf.shape, lambda i: (0, 0)),
            pl.BlockSpec(wsd_bf.shape, lambda i: (0, 0)),
            pl.BlockSpec((tc, d), lambda i: (i, 0)),
            pl.BlockSpec((1, d), lambda i: (0, 0)),
            pl.BlockSpec((1, 1, d), lambda i: (i // tiles_per_seq, 0, 5)),
        ],
        out_specs=pl.BlockSpec((tc, d), lambda i: (i, 0)),
        out_shape=jax.ShapeDtypeStruct((t, d), F32),
        scratch_shapes=[
            pltpu.SMEM((TOP_K * tc,), I32),
            pltpu.VMEM((TOP_K, tc, d), F32),
            pltpu.SemaphoreType.DMA(()),
            pltpu.SemaphoreType.DMA(()),
        ],
        compiler_params=pltpu.CompilerParams(
            dimension_semantics=("arbitrary",), vmem_limit_bytes=_vmem_limit(48 << 20)),
        name="combine",
    )(dest_steps, y, gk, t_f32, wsg_bf, wsu_bf, wsd_bf, x1, g_post, mod3)


def kernel(x, c, ctx, c_ctx, w_mod, b_mod, g_pre_mix, g_post_mix, g_pre_ffn, g_post_ffn, w_in, conv_w, conv_b,
           rpb, g_conv_out, g_attn_out, w_out, w_router, router_bias, w_exp_gate, w_exp_up, w_exp_down,
           w_sh_gate, w_sh_up, w_sh_down):
    batch, seq, d = x.shape
    ctx_len = ctx.shape[1]
    n_tok = batch * seq
    assert w_mod.shape[0] == 1, "single-layer kernel"
    assert seq % (Q_ROWS * GRID_W) == 0 and seq // GRID_W >= K_ROWS
    assert w_router.shape[2] == N_EXPERTS and n_tok % DISPATCH_TOKENS == 0 and seq % COMBINE_TOKENS == 0
    d_attn = N_HEADS * HEAD_DIM
    kv_offset = w_in.shape[2] - 2 * d_attn
    row2 = lambda a: a.reshape(1, -1)

    x2 = x.reshape(n_tok, d)
    ctx2 = ctx.reshape(batch * ctx_len, d)

    n_rows = -(-(batch + 1) // 8) * 8
    cvec = jnp.concatenate([c, c_ctx[None], jnp.zeros((n_rows - batch - 1, d), F32)], axis=0)
    mod = _mod_call(cvec, w_mod[0], row2(b_mod[0]))
    mod3 = mod.reshape(n_rows, 1, 6 * d)

    w_in_bf = w_in[0].astype(BF16)
    tm_in = 1024
    tiles_per_seq = seq // tm_in
    proj = _inproj_call(x2, row2(g_pre_mix[0]), mod3, w_in_bf, lambda i: i // tiles_per_seq,
                        tm_in, 1024, "in_proj")
    proj_ctx = _inproj_call(ctx2, row2(g_pre_mix[0]), mod3, w_in_bf[:, kv_offset:], lambda i: batch,
                            min(tm_in, ctx2.shape[0]), 1024, "in_proj_ctx")

    cos_t, sin_t = _rope_tables(seq)
    bias = _bias_table(rpb[0], seq // GRID_W)
    ya = _attn_call(proj, proj_ctx, cos_t, sin_t, bias, row2(g_attn_out[0]), batch, seq, ctx_len)
    yc = _conv_call(proj, conv_w[0], row2(conv_b[0]), row2(g_conv_out[0]), batch, seq)

    x1, t_f32, logits_t = _outproj_call(yc, ya, w_out[0].astype(BF16), x2, row2(g_post_mix[0]),
                                        row2(g_pre_ffn[0]), mod3, w_router[0].T, seq)
    ek, rk, gk, cnt = _router_call(logits_t, router_bias[0].reshape(-1, 1))

    n_tiles_max = n_tok * TOP_K // EXPERT_ROW_TILE + N_EXPERTS
    n_sorted = n_tiles_max * EXPERT_ROW_TILE
    _, dest_d, tile_expert, n_valid, pad_start, pad_len = _routing_tables(
        ek, rk, cnt[:, 0], n_tiles_max, DISPATCH_TOKENS)
    dest_c = dest_d.reshape(n_tok // DISPATCH_TOKENS, TOP_K, DISPATCH_TOKENS // COMBINE_TOKENS, COMBINE_TOKENS)
    dest_c = dest_c.transpose(0, 2, 1, 3).reshape(-1)
    xs = _dispatch_call(pad_start, pad_len, dest_d, t_f32, n_sorted)
    y = _experts_call(tile_expert, n_valid, xs, w_exp_gate[0], w_exp_up[0], w_exp_down[0])
    out = _combine_call(dest_c, y, gk, t_f32, w_sh_gate[0].astype(BF16), w_sh_up[0].astype(BF16),
                        w_sh_down[0].astype(BF16), x1, row2(g_post_ffn[0]), mod3, seq)
    return out.reshape(batch, seq, d)
```

```python
import functools

import numpy as np
import jax
import jax.numpy as jnp
from jax import lax
from jax.experimental import pallas as pl
from jax.experimental.pallas import tpu as pltpu

F32 = jnp.float32
BF16 = jnp.bfloat16
I32 = jnp.int32

GRID_W = 64
N_HEADS = 8
HEAD_DIM = 128
NA_ROWS = 8
NA_COLS = 16
ROPE_THETA = 10000.0
CONV_GROUP_DIM = 128
N_EXPERTS = 64
N_EXPERT_GROUPS = 8
TOPK_GROUPS = 4
TOP_K = 8
ROUTED_SCALE = 2.5
EPS = 1e-6
MASK_VALUE = -1e30

Q_ROWS = 4
K_ROWS = 12

V7X_VMEM_BYTES = 64 * 1024 * 1024
LANES = 128
SUBLANES = 8

EXPERT_ROW_TILE = 512
COMBINE_TOKENS = 128
IDX_ALIGN = 1024
IDX_WINDOW = 2 * IDX_ALIGN
ISSUE_UNROLL = 8


def _vmem_limit(nbytes):
    return int(min(nbytes, V7X_VMEM_BYTES - 6 * 1024 * 1024))


def _rms(x):
    return x * lax.rsqrt(jnp.mean(x * x, axis=-1, keepdims=True) + EPS)


def _silu(x):
    return x * jax.nn.sigmoid(x)


def _dot(a, b):
    return jnp.dot(a, b, preferred_element_type=F32)


def _dot_nt(a, b, precision=None):
    return lax.dot_general(a, b, (((1,), (1,)), ((), ())), preferred_element_type=F32, precision=precision)


def _mod_kernel(c_ref, w_ref, b_ref, o_ref):
    a = _silu(c_ref[...]).astype(BF16)
    o_ref[...] = _dot(a, w_ref[...].astype(BF16)) + b_ref[...]


def _mod_call(cvec, w_mod, b_mod, tn=1024):
    rows, d = cvec.shape
    n = w_mod.shape[1]
    return pl.pallas_call(
        _mod_kernel,
        grid=(n // tn,),
        in_specs=[
            pl.BlockSpec((rows, d), lambda j: (0, 0)),
            pl.BlockSpec((d, tn), lambda j: (0, j)),
            pl.BlockSpec((1, tn), lambda j: (0, j)),
        ],
        out_specs=pl.BlockSpec((rows, tn), lambda j: (0, j)),
        out_shape=jax.ShapeDtypeStruct((rows, n), F32),
        compiler_params=pltpu.CompilerParams(
            dimension_semantics=("arbitrary",), vmem_limit_bytes=_vmem_limit(40 << 20)),
        name="mod",
    )(cvec, w_mod, b_mod)


def _inproj_kernel(x_ref, g_ref, sh_ref, sc_ref, w_ref, o_ref, h_ref):
    @pl.when(pl.program_id(1) == 0)
    def _():
        h = _rms(x_ref[...]) * g_ref[...]
        h = h * (1.0 + sc_ref[0]) + sh_ref[0]
        h_ref[...] = h.astype(BF16)

    o_ref[...] = _dot(h_ref[...], w_ref[...])


def _inproj_call(x2, g, mod3, w_bf, row_of_tile, tm, tn, name):
    t, d = x2.shape
    n = w_bf.shape[1]
    return pl.pallas_call(
        _inproj_kernel,
        grid=(t // tm, n // tn),
        in_specs=[
            pl.BlockSpec((tm, d), lambda i, j: (i, 0)),
            pl.BlockSpec((1, d), lambda i, j: (0, 0)),
            pl.BlockSpec((1, 1, d), lambda i, j: (row_of_tile(i), 0, 0)),
            pl.BlockSpec((1, 1, d), lambda i, j: (row_of_tile(i), 0, 1)),
            pl.BlockSpec((d, tn), lambda i, j: (0, j)),
        ],
        out_specs=pl.BlockSpec((tm, tn), lambda i, j: (i, j)),
        out_shape=jax.ShapeDtypeStruct((t, n), F32),
        scratch_shapes=[pltpu.VMEM((tm, d), BF16)],
        compiler_params=pltpu.CompilerParams(
            dimension_semantics=("arbitrary", "arbitrary"), vmem_limit_bytes=_vmem_limit(52 << 20)),
        name=name,
    )(x2, g, mod3, mod3, w_bf)


def _attn_kernel(q_ref, k_ref, v_ref, kc_ref, vc_ref, cos_ref, sin_ref, bias_ref, g_ref, o_ref,
                 qs_ref, ks_ref, vs_ref, kcs_ref, vcs_ref):
    cos = cos_ref[...]
    sin = sin_ref[...]

    def rope(x):
        return x * cos + pltpu.roll(x, HEAD_DIM // 2, 1) * sin

    qs_ref[...] = rope(q_ref[...]).astype(BF16)
    ks_ref[...] = rope(k_ref[...]).astype(BF16)
    vs_ref[...] = v_ref[...].astype(BF16)
    kcs_ref[...] = kc_ref[...].astype(BF16)
    vcs_ref[...] = vc_ref[...].astype(BF16)

    scale = HEAD_DIM ** -0.5
    qb = Q_ROWS * GRID_W
    kb = K_ROWS * GRID_W
    n_blocks = q_ref.shape[0] // qb
    max_start = q_ref.shape[0] // GRID_W - K_ROWS

    def body(j, carry):
        q0 = pl.multiple_of(j * qb, qb)
        k_row = jnp.clip(j * Q_ROWS - NA_ROWS // 2, 0, max_start)
        k0 = pl.multiple_of(k_row * GRID_W, qb)
        q = qs_ref[pl.ds(q0, qb), :]
        s_loc = _dot_nt(q, ks_ref[pl.ds(k0, kb), :]) * scale + bias_ref[0, j]
        s_ctx = _dot_nt(q, kcs_ref[...]) * scale
        m = jnp.maximum(jnp.max(s_loc, axis=-1, keepdims=True), jnp.max(s_ctx, axis=-1, keepdims=True))
        p_loc = jnp.exp(s_loc - m)
        p_ctx = jnp.exp(s_ctx - m)
        denom = jnp.sum(p_loc, axis=-1, keepdims=True) + jnp.sum(p_ctx, axis=-1, keepdims=True)
        o = _dot(p_loc.astype(BF16), vs_ref[pl.ds(k0, kb), :]) + _dot(p_ctx.astype(BF16), vcs_ref[...])
        o = o / denom
        o_ref[pl.ds(q0, qb), :] = (_rms(o) * g_ref[...]).astype(BF16)
        return carry

    lax.fori_loop(0, n_blocks, body, 0)


def _attn_call(proj, proj_ctx, cos_t, sin_t, bias, g_attn, batch, seq, ctx_len):
    d_attn = N_HEADS * HEAD_DIM
    q_blk = (proj.shape[1] - 3 * d_attn) // HEAD_DIM
    n_blocks = seq // (Q_ROWS * GRID_W)
    qb, kb = Q_ROWS * GRID_W, K_ROWS * GRID_W
    return pl.pallas_call(
        _attn_kernel,
        grid=(N_HEADS, batch),
        in_specs=[
            pl.BlockSpec((seq, HEAD_DIM), lambda h, b: (b, q_blk + h)),
            pl.BlockSpec((seq, HEAD_DIM), lambda h, b: (b, q_blk + N_HEADS + h)),
            pl.BlockSpec((seq, HEAD_DIM), lambda h, b: (b, q_blk + 2 * N_HEADS + h)),
            pl.BlockSpec((ctx_len, HEAD_DIM), lambda h, b: (b, h)),
            pl.BlockSpec((ctx_len, HEAD_DIM), lambda h, b: (b, N_HEADS + h)),
            pl.BlockSpec((seq, HEAD_DIM), lambda h, b: (0, 0)),
            pl.BlockSpec((seq, HEAD_DIM), lambda h, b: (0, 0)),
            pl.BlockSpec((1, n_blocks, qb, kb), lambda h, b: (h, 0, 0, 0)),
            pl.BlockSpec((1, HEAD_DIM), lambda h, b: (0, h)),
        ],
        out_specs=pl.BlockSpec((seq, HEAD_DIM), lambda h, b: (b, h)),
        out_shape=jax.ShapeDtypeStruct((batch * seq, d_attn), BF16),
        scratch_shapes=[
            pltpu.VMEM((seq, HEAD_DIM), BF16),
            pltpu.VMEM((seq, HEAD_DIM), BF16),
            pltpu.VMEM((seq, HEAD_DIM), BF16),
            pltpu.VMEM((ctx_len, HEAD_DIM), BF16),
            pltpu.VMEM((ctx_len, HEAD_DIM), BF16),
        ],
        compiler_params=pltpu.CompilerParams(
            dimension_semantics=("arbitrary", "arbitrary"), vmem_limit_bytes=_vmem_limit(48 << 20)),
        name="attn",
    )(proj, proj, proj, proj_ctx, proj_ctx, cos_t, sin_t, bias, g_attn)


def _rope_tables(seq):
    t = jnp.arange(seq, dtype=I32)
    pos = jnp.stack([t // GRID_W, t % GRID_W], axis=-1).astype(F32)
    n_freq = HEAD_DIM // 4
    inv_freq = ROPE_THETA ** (-jnp.arange(n_freq, dtype=F32) / n_freq)
    ang = pos[:, :, None] * inv_freq
    cos, sin = jnp.cos(ang).reshape(seq, 2 * n_freq), jnp.sin(ang).reshape(seq, 2 * n_freq)
    cos_t = jnp.concatenate([cos, cos], axis=-1)
    sin_t = jnp.concatenate([-sin, sin], axis=-1)
    return cos_t, sin_t


def _rope_order(w_in, n_lead_cols):
    d_attn = N_HEADS * HEAD_DIM
    lead, qk, v = w_in[:, :n_lead_cols], w_in[:, n_lead_cols:n_lead_cols + 2 * d_attn], w_in[:, n_lead_cols + 2 * d_attn:]
    qk = qk.reshape(qk.shape[0], 2 * N_HEADS, 2, 2, HEAD_DIM // 4).transpose(0, 1, 3, 2, 4).reshape(qk.shape)
    return jnp.concatenate([lead, qk, v], axis=1)


def _bias_table(rpb, rows):
    w = GRID_W
    qc = np.arange(w)[:, None]
    kc = np.arange(w)[None, :]
    c0 = np.clip(qc - NA_COLS // 2, 0, w - NA_COLS)
    col_ok = (kc >= c0) & (kc < c0 + NA_COLS)
    col_off = np.clip(kc - qc + NA_COLS - 1, 0, 2 * NA_COLS - 2)
    toep = jnp.where(col_ok[None, None], rpb[:, :, col_off], MASK_VALUE)
    masked = jnp.full((rpb.shape[0], 1, w, w), MASK_VALUE, F32)
    toep = jnp.concatenate([toep, masked], axis=1)
    n_blocks = rows // Q_ROWS
    kh = min(NA_ROWS, rows)
    sel = np.zeros((n_blocks, Q_ROWS, K_ROWS), np.int32)
    for j in range(n_blocks):
        k_start = int(np.clip(j * Q_ROWS - NA_ROWS // 2, 0, rows - K_ROWS))
        for a in range(Q_ROWS):
            r = j * Q_ROWS + a
            r0 = int(np.clip(r - kh // 2, 0, rows - kh))
            for b in range(K_ROWS):
                kr = k_start + b
                sel[j, a, b] = kr - r + NA_ROWS - 1 if r0 <= kr < r0 + kh else 2 * NA_ROWS - 1
    blocks = toep[:, sel]
    blocks = jnp.transpose(blocks, (0, 1, 2, 4, 3, 5))
    return blocks.reshape(rpb.shape[0], n_blocks, Q_ROWS * w, K_ROWS * w)


def _conv_kernel(ub_ref, uc_ref, ux_ref, w_ref, b_ref, g_ref, o_ref):
    p = uc_ref[...] * ux_ref[...]
    seq = p.shape[0]
    row = lax.broadcasted_iota(I32, p.shape, 0)
    prev = jnp.where(row == 0, 0.0, pltpu.roll(p, 1, 0))
    nxt = jnp.where(row == seq - 1, 0.0, pltpu.roll(p, seq - 1, 0))
    w = w_ref[...]
    conv = prev * w[0:1] + p * w[1:2] + nxt * w[2:3] + b_ref[...]
    y = ub_ref[...] * conv
    g = g_ref[...]
    for c0 in range(0, y.shape[1], CONV_GROUP_DIM):
        sl = slice(c0, c0 + CONV_GROUP_DIM)
        o_ref[:, sl] = (_rms(y[:, sl]) * g[:, sl]).astype(BF16)


def _conv_call(proj, conv_w, conv_b, g_conv, batch, seq, cb=256):
    d_conv = conv_w.shape[1]
    nc = d_conv // cb
    return pl.pallas_call(
        _conv_kernel,
        grid=(batch, nc),
        in_specs=[
            pl.BlockSpec((seq, cb), lambda b, c: (b, c)),
            pl.BlockSpec((seq, cb), lambda b, c: (b, nc + c)),
            pl.BlockSpec((seq, cb), lambda b, c: (b, 2 * nc + c)),
            pl.BlockSpec((conv_w.shape[0], cb), lambda b, c: (0, c)),
            pl.BlockSpec((1, cb), lambda b, c: (0, c)),
            pl.BlockSpec((1, cb), lambda b, c: (0, c)),
        ],
        out_specs=pl.BlockSpec((seq, cb), lambda b, c: (b, c)),
        out_shape=jax.ShapeDtypeStruct((batch * seq, d_conv), BF16),
        compiler_params=pltpu.CompilerParams(
            dimension_semantics=("arbitrary", "arbitrary"), vmem_limit_bytes=_vmem_limit(48 << 20)),
        name="conv",
    )(proj, proj, proj, conv_w, conv_b, g_conv)


def _outproj_kernel(yc_ref, ya_ref, w_ref, x_ref, gpost_ref, gate_ref, gpre_ref, sh_ref, sc_ref, wr_ref,
                    x1_ref, t_ref, lg_ref):
    dc = yc_ref.shape[1]
    y = _dot(yc_ref[...], w_ref[0:dc, :]) + _dot(ya_ref[...], w_ref[dc:, :])
    x1 = x_ref[...] + gate_ref[0] * (_rms(y) * gpost_ref[...])
    x1_ref[...] = x1
    t = (_rms(x1) * gpre_ref[...]) * (1.0 + sc_ref[0]) + sh_ref[0]
    t_ref[...] = t
    wr = wr_ref[...]
    wr_hi = wr.astype(BF16)
    wr_lo = (wr - wr_hi.astype(F32)).astype(BF16)
    t_hi = t.astype(BF16)
    t_lo = (t - t_hi.astype(F32)).astype(BF16)
    logits = _dot_nt(t_hi, wr_hi) + (_dot_nt(t_lo, wr_hi) + _dot_nt(t_hi, wr_lo))
    lg_ref[...] = logits.T


def _outproj_call(yc, ya, w_out_bf, x2, g_post, g_pre, mod3, w_router_t, seq, tm=512):
    t, d = x2.shape
    dc = yc.shape[1]
    ne = w_router_t.shape[0]
    tiles_per_seq = seq // tm
    row = lambda i: i // tiles_per_seq
    return pl.pallas_call(
        _outproj_kernel,
        grid=(t // tm,),
        in_specs=[
            pl.BlockSpec((tm, dc), lambda i: (i, 0)),
            pl.BlockSpec((tm, ya.shape[1]), lambda i: (i, 0)),
            pl.BlockSpec(w_out_bf.shape, lambda i: (0, 0), pipeline_mode=pl.Buffered(1)),
            pl.BlockSpec((tm, d), lambda i: (i, 0)),
            pl.BlockSpec((1, d), lambda i: (0, 0)),
            pl.BlockSpec((1, 1, d), lambda i: (row(i), 0, 2)),
            pl.BlockSpec((1, d), lambda i: (0, 0)),
            pl.BlockSpec((1, 1, d), lambda i: (row(i), 0, 3)),
            pl.BlockSpec((1, 1, d), lambda i: (row(i), 0, 4)),
            pl.BlockSpec((ne, d), lambda i: (0, 0)),
        ],
        out_specs=[
            pl.BlockSpec((tm, d), lambda i: (i, 0)),
            pl.BlockSpec((tm, d), lambda i: (i, 0)),
            pl.BlockSpec((ne, tm), lambda i: (0, i)),
        ],
        out_shape=[
            jax.ShapeDtypeStruct((t, d), F32),
            jax.ShapeDtypeStruct((t, d), F32),
            jax.ShapeDtypeStruct((ne, t), F32),
        ],
        compiler_params=pltpu.CompilerParams(
            dimension_semantics=("arbitrary",), vmem_limit_bytes=_vmem_limit(57 << 20)),
        name="out_proj",
    )(yc, ya, w_out_bf, x2, g_post, mod3, g_pre, mod3, mod3, w_router_t)


def _first_argmax_mask(v, axis):
    m = jnp.max(v, axis=axis, keepdims=True)
    idx = lax.broadcasted_iota(I32, v.shape, axis)
    big = v.shape[axis]
    first = jnp.min(jnp.where(v == m, idx, big), axis=axis, keepdims=True)
    return idx == first, m


def _router_kernel(lg_ref, rb_ref, ek_ref, rk_ref, gk_ref, cnt_ref, carry_ref):
    ne, tm = lg_ref.shape
    per = ne // N_EXPERT_GROUPS

    @pl.when(pl.program_id(0) == 0)
    def _():
        carry_ref[...] = jnp.zeros_like(carry_ref)

    scores = jax.nn.sigmoid(lg_ref[...])
    sel = scores + rb_ref[...]
    s3 = sel.reshape(N_EXPERT_GROUPS, per, tm)
    hit1, m1 = _first_argmax_mask(s3, 1)
    m2 = jnp.max(jnp.where(hit1, -jnp.inf, s3), axis=1, keepdims=True)
    grp = (m1 + m2).reshape(N_EXPERT_GROUPS, tm)
    gmask = jnp.zeros(grp.shape, jnp.bool_)
    for _ in range(TOPK_GROUPS):
        hit, _ = _first_argmax_mask(jnp.where(gmask, -jnp.inf, grp), 0)
        gmask = gmask | hit
    emask = jnp.broadcast_to(gmask.reshape(N_EXPERT_GROUPS, 1, tm), s3.shape).reshape(ne, tm)
    cand = jnp.where(emask, sel, -jnp.inf)
    chosen = jnp.zeros(cand.shape, jnp.bool_)
    hits = []
    for _ in range(TOP_K):
        hit, _ = _first_argmax_mask(jnp.where(chosen, -jnp.inf, cand), 0)
        hit = hit & ~chosen
        hits.append(hit)
        chosen = chosen | hit
    wsel = jnp.where(chosen, scores, 0.0)
    gates = wsel / jnp.sum(wsel, axis=0, keepdims=True) * ROUTED_SCALE

    r_i = lax.broadcasted_iota(I32, (tm, tm), 0)
    c_i = lax.broadcasted_iota(I32, (tm, tm), 1)
    tri = jnp.where(r_i <= c_i, 1.0, 0.0).astype(BF16)
    csum = _dot(jnp.where(chosen, 1.0, 0.0).astype(BF16), tri)
    carry = carry_ref[:, 0:1]
    rank = carry + csum - 1.0
    eidx = lax.broadcasted_iota(I32, (ne, tm), 0).astype(F32)

    def pick(hit, v):
        return jnp.sum(jnp.where(hit, v, 0.0), axis=0, keepdims=True)

    ek_ref[...] = jnp.concatenate([pick(h, eidx) for h in hits], axis=0).astype(I32)
    rk_ref[...] = jnp.concatenate([pick(h, rank) for h in hits], axis=0).astype(I32)
    gk_rows = jnp.concatenate([pick(h, gates) for h in hits] + [jnp.zeros((LANES - TOP_K, tm), F32)], axis=0)
    gk_ref[...] = gk_rows.T
    new_carry = carry + csum[:, tm - 1:tm]
    carry_ref[...] = jnp.broadcast_to(new_carry, carry_ref.shape)
    cnt_ref[...] = jnp.broadcast_to(new_carry, cnt_ref.shape).astype(I32)


def _router_call(logits_t, router_bias_col, tm=512):
    ne, t = logits_t.shape
    return pl.pallas_call(
        _router_kernel,
        grid=(t // tm,),
        in_specs=[
            pl.BlockSpec((ne, tm), lambda i: (0, i)),
            pl.BlockSpec((ne, 1), lambda i: (0, 0)),
        ],
        out_specs=[
            pl.BlockSpec((TOP_K, tm), lambda i: (0, i)),
            pl.BlockSpec((TOP_K, tm), lambda i: (0, i)),
            pl.BlockSpec((tm, LANES), lambda i: (i, 0)),
            pl.BlockSpec((ne, LANES), lambda i: (0, 0)),
        ],
        out_shape=[
            jax.ShapeDtypeStruct((TOP_K, t), I32),
            jax.ShapeDtypeStruct((TOP_K, t), I32),
            jax.ShapeDtypeStruct((t, LANES), F32),
            jax.ShapeDtypeStruct((ne, LANES), I32),
        ],
        scratch_shapes=[pltpu.VMEM((ne, LANES), F32)],
        compiler_params=pltpu.CompilerParams(dimension_semantics=("arbitrary",)),
        name="router",
    )(logits_t, router_bias_col)


def _routing_tables(ek, rk, counts, n_tiles_max, tokens_per_step):
    tg = EXPERT_ROW_TILE
    k, t = ek.shape
    expert_ids = jnp.arange(counts.shape[0], dtype=I32)
    padded = (counts + tg - 1) // tg * tg
    ends = jnp.cumsum(padded)
    offsets = ends - padded
    compact_offsets = jnp.cumsum(counts) - counts
    one_hot = ek[..., None] == expert_ids
    dest = rk + jnp.sum(jnp.where(one_hot, offsets, 0), axis=-1)
    compact = rk + jnp.sum(jnp.where(one_hot, compact_offsets, 0), axis=-1)
    tok = jnp.broadcast_to(jnp.arange(t, dtype=I32), (k, t))
    tok_sorted = lax.sort((compact.reshape(-1), tok.reshape(-1)), num_keys=1)[1]
    tok_sorted = jnp.concatenate([tok_sorted, jnp.zeros((IDX_WINDOW,), I32)])
    tile_ends = ends // tg
    n_valid = tile_ends[-1]
    tile_ids = jnp.arange(n_tiles_max, dtype=I32)
    tile_expert = jnp.sum(tile_ids[:, None] >= tile_ends[None, :], axis=1).astype(I32)
    last_expert = jnp.sum(n_valid - 1 >= tile_ends).astype(I32)
    tile_expert = jnp.where(tile_ids < n_valid, tile_expert, last_expert)
    tile_start = compact_offsets[tile_expert] + (tile_ids * tg - offsets[tile_expert])
    tile_start = jnp.where(tile_ids < n_valid, tile_start, 0).astype(I32)
    dest_steps = dest.reshape(k, t // tokens_per_step, tokens_per_step).transpose(1, 0, 2).reshape(-1)
    return dest_steps, tok_sorted, tile_expert, n_valid.reshape(1).astype(I32), tile_start


def _experts_kernel(te_ref, nv_ref, ts_ref, tok_hbm, t_hbm, wg_ref, wu_ref, wd_ref, y_ref,
                    idx0, idx1, x0, x1, wg_s, wu_s, wd_s, sem_idx, sem_row):
    i = pl.program_id(0)
    nv = nv_ref[0]
    tg = y_ref.shape[0]
    idx_bufs, x_bufs = (idx0, idx1), (x0, x1)

    def idx_copy(tile, slot):
        window = pl.multiple_of(lax.shift_left(lax.shift_right_logical(ts_ref[tile], 10), 10), IDX_ALIGN)
        return pltpu.make_async_copy(tok_hbm.at[pl.ds(window, IDX_WINDOW)], idx_bufs[slot], sem_idx.at[slot])

    def row_copy(tile, slot, r):
        base = ts_ref[tile] & (IDX_ALIGN - 1)
        return pltpu.make_async_copy(t_hbm.at[idx_bufs[slot][base + r]], x_bufs[slot].at[r], sem_row.at[slot])

    def wait_rows(slot):
        pltpu.make_async_copy(t_hbm.at[pl.ds(0, tg)], x_bufs[slot], sem_row.at[slot]).wait()

    def step(slot):
        nslot = 1 - slot
        nxt = jnp.minimum(i + 1, nv - 1)
        nxt2 = jnp.minimum(i + 2, nv - 1)

        if slot == 0:
            @pl.when(i == 0)
            def _():
                idx_copy(0, 0).start()
                idx_copy(0, 0).wait()

                def first_rows(r, c):
                    row_copy(0, 0, r).start()
                    return c

                lax.fori_loop(0, tg, first_rows, 0)
                idx_copy(nxt, 1).start()

        idx_copy(nxt, nslot).wait()
        idx_copy(nxt2, slot).start()

        @pl.when((i == 0) | (te_ref[i] != te_ref[jnp.maximum(i - 1, 0)]))
        def _():
            wg_s[...] = wg_ref[0].astype(BF16)
            wu_s[...] = wu_ref[0].astype(BF16)
            wd_s[...] = wd_ref[0].astype(BF16)

        wait_rows(slot)
        x = x_bufs[slot][...].astype(BF16)
        hid = _silu(_dot(x, wg_s[...])) * _dot(x, wu_s[...])
        y_ref[...] = _dot(hid.astype(BF16), wd_s[...])
        for r in range(tg):
            row_copy(nxt, nslot, r).start(priority=r % 2)

        @pl.when(i == nv - 1)
        def _():
            wait_rows(nslot)
            idx_copy(nxt2, slot).wait()

    @pl.when((i < nv) & ((i & 1) == 0))
    def _():
        step(0)

    @pl.when((i < nv) & ((i & 1) == 1))
    def _():
        step(1)

    @pl.when(i >= nv)
    def _():
        y_ref[...] = jnp.zeros_like(y_ref)


def _experts_call(tile_expert, n_valid, tile_start, tok_sorted, t_f32, w_gate, w_up, w_down):
    _, d = t_f32.shape
    ne, _, df = w_gate.shape
    tg = EXPERT_ROW_TILE
    n_tiles = tile_expert.shape[0]
    grid_spec = pltpu.PrefetchScalarGridSpec(
        num_scalar_prefetch=3,
        grid=(n_tiles,),
        in_specs=[
            pl.BlockSpec(memory_space=pl.ANY),
            pl.BlockSpec(memory_space=pl.ANY),
            pl.BlockSpec((1, d, df), lambda i, te, nv, ts: (te[i], 0, 0)),
            pl.BlockSpec((1, d, df), lambda i, te, nv, ts: (te[i], 0, 0)),
            pl.BlockSpec((1, df, d), lambda i, te, nv, ts: (te[i], 0, 0)),
        ],
        out_specs=pl.BlockSpec((tg, d), lambda i, te, nv, ts: (i, 0)),
        scratch_shapes=[
            pltpu.SMEM((IDX_WINDOW,), I32),
            pltpu.SMEM((IDX_WINDOW,), I32),
            pltpu.VMEM((tg, d), F32),
            pltpu.VMEM((tg, d), F32),
            pltpu.VMEM((d, df), BF16),
            pltpu.VMEM((d, df), BF16),
            pltpu.VMEM((df, d), BF16),
            pltpu.SemaphoreType.DMA((2,)),
            pltpu.SemaphoreType.DMA((2,)),
        ],
    )
    return pl.pallas_call(
        _experts_kernel,
        grid_spec=grid_spec,
        out_shape=jax.ShapeDtypeStruct((n_tiles * tg, d), F32),
        compiler_params=pltpu.CompilerParams(
            dimension_semantics=("arbitrary",), vmem_limit_bytes=_vmem_limit(57 << 20)),
        name="experts",
    )(tile_expert, n_valid, tile_start, tok_sorted, t_f32, w_gate, w_up, w_down)


def _combine_kernel(dest_hbm, y_hbm, gk_ref, t_ref, wsg_ref, wsu_ref, wsd_ref, x1_ref, gpost_ref, gate_ref, o_ref,
                    idx_smem, rows_ref, sem_idx, sem_row, *, tc):
    i = pl.program_id(0)
    n_idx = TOP_K * tc
    idx_copy = pltpu.make_async_copy(dest_hbm.at[pl.ds(pl.multiple_of(i * n_idx, n_idx), n_idx)], idx_smem, sem_idx)
    idx_copy.start()
    idx_copy.wait()

    def issue(tb, carry):
        for u in range(ISSUE_UNROLL):
            t = tb * ISSUE_UNROLL + u
            for k in range(TOP_K):
                pltpu.make_async_copy(y_hbm.at[idx_smem[k * tc + t]], rows_ref.at[k, t], sem_row).start(
                    priority=k % 2)
        return carry

    lax.fori_loop(0, tc // ISSUE_UNROLL, issue, 0)

    t = t_ref[...].astype(BF16)
    hid = _silu(_dot(t, wsg_ref[...])) * _dot(t, wsu_ref[...])
    f = _dot(hid.astype(BF16), wsd_ref[...])

    for k in range(TOP_K):
        pltpu.make_async_copy(y_hbm.at[pl.ds(0, tc)], rows_ref.at[k], sem_row).wait()
    g = gk_ref[...]
    for k in range(TOP_K):
        f = f + g[:, k:k + 1] * rows_ref[k]
    o_ref[...] = x1_ref[...] + gate_ref[0] * (_rms(f) * gpost_ref[...])


def _combine_call(dest_steps, y, gk, t_f32, wsg_bf, wsu_bf, wsd_bf, x1, g_post, mod3, seq):
    t, d = x1.shape
    tc = COMBINE_TOKENS
    tiles_per_seq = seq // tc
    return pl.pallas_call(
        functools.partial(_combine_kernel, tc=tc),
        grid=(t // tc,),
        in_specs=[
            pl.BlockSpec(memory_space=pl.ANY),
            pl.BlockSpec(memory_space=pl.ANY),
            pl.BlockSpec((tc, LANES), lambda i: (i, 0)),
            pl.BlockSpec((tc, d), lambda i: (i, 0)),
            pl.BlockSpec(wsg_bf.shape, lambda i: (0, 0)),
            pl.BlockSpec(wsu_bf.shape, lambda i: (0, 0)),
            pl.BlockSpec(wsd_bf.shape, lambda i: (0, 0)),
            pl.BlockSpec((tc, d), lambda i: (i, 0)),
            pl.BlockSpec((1, d), lambda i: (0, 0)),
            pl.BlockSpec((1, 1, d), lambda i: (i // tiles_per_seq, 0, 5)),
        ],
        out_specs=pl.BlockSpec((tc, d), lambda i: (i, 0)),
        out_shape=jax.ShapeDtypeStruct((t, d), F32),
        scratch_shapes=[
            pltpu.SMEM((TOP_K * tc,), I32),
            pltpu.VMEM((TOP_K, tc, d), F32),
            pltpu.SemaphoreType.DMA(()),
            pltpu.SemaphoreType.DMA(()),
        ],
        compiler_params=pltpu.CompilerParams(
            dimension_semantics=("arbitrary",), vmem_limit_bytes=_vmem_limit(48 << 20)),
        name="combine",
    )(dest_steps, y, gk, t_f32, wsg_bf, wsu_bf, wsd_bf, x1, g_post, mod3)


def kernel(x, c, ctx, c_ctx, w_mod, b_mod, g_pre_mix, g_post_mix, g_pre_ffn, g_post_ffn, w_in, conv_w, conv_b,
           rpb, g_conv_out, g_attn_out, w_out, w_router, router_bias, w_exp_gate, w_exp_up, w_exp_down,
           w_sh_gate, w_sh_up, w_sh_down):
    batch, seq, d = x.shape
    ctx_len = ctx.shape[1]
    n_tok = batch * seq
    assert w_mod.shape[0] == 1, "single-layer kernel"
    assert seq % (Q_ROWS * GRID_W) == 0 and seq // GRID_W >= K_ROWS
    assert w_router.shape[2] == N_EXPERTS and seq % COMBINE_TOKENS == 0
    assert EXPERT_ROW_TILE <= IDX_ALIGN and n_tok * TOP_K % EXPERT_ROW_TILE == 0
    d_attn = N_HEADS * HEAD_DIM
    kv_offset = w_in.shape[2] - 2 * d_attn
    row2 = lambda a: a.reshape(1, -1)

    x2 = x.reshape(n_tok, d)
    ctx2 = ctx.reshape(batch * ctx_len, d)

    n_rows = -(-(batch + 1) // 8) * 8
    cvec = jnp.concatenate([c, c_ctx[None], jnp.zeros((n_rows - batch - 1, d), F32)], axis=0)
    mod = _mod_call(cvec, w_mod[0], row2(b_mod[0]))
    mod3 = mod.reshape(n_rows, 1, 6 * d)

    w_in_bf = _rope_order(w_in[0].astype(BF16), w_in.shape[2] - 3 * d_attn)
    tm_in = 1024
    tiles_per_seq = seq // tm_in
    proj = _inproj_call(x2, row2(g_pre_mix[0]), mod3, w_in_bf, lambda i: i // tiles_per_seq,
                        tm_in, 1024, "in_proj")
    proj_ctx = _inproj_call(ctx2, row2(g_pre_mix[0]), mod3, w_in_bf[:, kv_offset:], lambda i: batch,
                            min(tm_in, ctx2.shape[0]), 1024, "in_proj_ctx")

    cos_t, sin_t = _rope_tables(seq)
    bias = _bias_table(rpb[0], seq // GRID_W)
    ya = _attn_call(proj, proj_ctx, cos_t, sin_t, bias, row2(g_attn_out[0]), batch, seq, ctx_len)
    yc = _conv_call(proj, conv_w[0], row2(conv_b[0]), row2(g_conv_out[0]), batch, seq)

    x1, t_f32, logits_t = _outproj_call(yc, ya, w_out[0].astype(BF16), x2, row2(g_post_mix[0]),
                                        row2(g_pre_ffn[0]), mod3, w_router[0].T, seq)
    ek, rk, gk, cnt = _router_call(logits_t, router_bias[0].reshape(-1, 1))

    n_tiles_max = n_tok * TOP_K // EXPERT_ROW_TILE + N_EXPERTS
    dest_c, tok_sorted, tile_expert, n_valid, tile_start = _routing_tables(
        ek, rk, cnt[:, 0], n_tiles_max, COMBINE_TOKENS)
    y = _experts_call(tile_expert, n_valid, tile_start, tok_sorted, t_f32,
                      w_exp_gate[0], w_exp_up[0], w_exp_down[0])
    out = _combine_call(dest_c, y, gk, t_f32, w_sh_gate[0].astype(BF16), w_sh_up[0].astype(BF16),
                        w_sh_down[0].astype(BF16), x1, row2(g_post_ffn[0]), mod3, seq)
    return out.reshape(batch, seq, d)
```

```python
import functools

import numpy as np
import jax
import jax.numpy as jnp
from jax import lax
from jax.experimental import pallas as pl
from jax.experimental.pallas import tpu as pltpu

F32 = jnp.float32
BF16 = jnp.bfloat16
I32 = jnp.int32

GRID_W = 64
N_HEADS = 8
HEAD_DIM = 128
NA_ROWS = 8
NA_COLS = 16
ROPE_THETA = 10000.0
CONV_GROUP_DIM = 128
N_EXPERTS = 64
N_EXPERT_GROUPS = 8
TOPK_GROUPS = 4
TOP_K = 8
ROUTED_SCALE = 2.5
EPS = 1e-6
MASK_VALUE = -1e30

Q_ROWS = 4
K_ROWS = 12

V7X_VMEM_BYTES = 64 * 1024 * 1024
LANES = 128
SUBLANES = 8

EXPERT_ROW_TILE = 256
ROW_BUFFERS = 4
ROW_LOOKAHEAD = ROW_BUFFERS - 1
COMBINE_TOKENS = 128
IDX_ALIGN = 1024
IDX_WINDOW = 2 * IDX_ALIGN
ISSUE_UNROLL = 8


def _vmem_limit(nbytes):
    return int(min(nbytes, V7X_VMEM_BYTES - 6 * 1024 * 1024))


def _rms(x):
    return x * lax.rsqrt(jnp.mean(x * x, axis=-1, keepdims=True) + EPS)


def _silu(x):
    return x * jax.nn.sigmoid(x)


def _dot(a, b):
    return jnp.dot(a, b, preferred_element_type=F32)


def _dot_nt(a, b, precision=None):
    return lax.dot_general(a, b, (((1,), (1,)), ((), ())), preferred_element_type=F32, precision=precision)


def _mod_kernel(c_ref, w_ref, b_ref, o_ref):
    a = _silu(c_ref[...]).astype(BF16)
    o_ref[...] = _dot(a, w_ref[...].astype(BF16)) + b_ref[...]


def _mod_call(cvec, w_mod, b_mod, tn=1024):
    rows, d = cvec.shape
    n = w_mod.shape[1]
    return pl.pallas_call(
        _mod_kernel,
        grid=(n // tn,),
        in_specs=[
            pl.BlockSpec((rows, d), lambda j: (0, 0)),
            pl.BlockSpec((d, tn), lambda j: (0, j)),
            pl.BlockSpec((1, tn), lambda j: (0, j)),
        ],
        out_specs=pl.BlockSpec((rows, tn), lambda j: (0, j)),
        out_shape=jax.ShapeDtypeStruct((rows, n), F32),
        compiler_params=pltpu.CompilerParams(
            dimension_semantics=("arbitrary",), vmem_limit_bytes=_vmem_limit(40 << 20)),
        name="mod",
    )(cvec, w_mod, b_mod)


def _inproj_kernel(x_ref, g_ref, sh_ref, sc_ref, w_ref, o_ref, h_ref):
    @pl.when(pl.program_id(1) == 0)
    def _():
        h = _rms(x_ref[...]) * g_ref[...]
        h = h * (1.0 + sc_ref[0]) + sh_ref[0]
        h_ref[...] = h.astype(BF16)

    o_ref[...] = _dot(h_ref[...], w_ref[...])


def _inproj_call(x2, g, mod3, w_bf, row_of_tile, tm, tn, name):
    t, d = x2.shape
    n = w_bf.shape[1]
    return pl.pallas_call(
        _inproj_kernel,
        grid=(t // tm, n // tn),
        in_specs=[
            pl.BlockSpec((tm, d), lambda i, j: (i, 0)),
            pl.BlockSpec((1, d), lambda i, j: (0, 0)),
            pl.BlockSpec((1, 1, d), lambda i, j: (row_of_tile(i), 0, 0)),
            pl.BlockSpec((1, 1, d), lambda i, j: (row_of_tile(i), 0, 1)),
            pl.BlockSpec((d, tn), lambda i, j: (0, j)),
        ],
        out_specs=pl.BlockSpec((tm, tn), lambda i, j: (i, j)),
        out_shape=jax.ShapeDtypeStruct((t, n), F32),
        scratch_shapes=[pltpu.VMEM((tm, d), BF16)],
        compiler_params=pltpu.CompilerParams(
            dimension_semantics=("arbitrary", "arbitrary"), vmem_limit_bytes=_vmem_limit(52 << 20)),
        name=name,
    )(x2, g, mod3, mod3, w_bf)


def _attn_kernel(q_ref, k_ref, v_ref, kc_ref, vc_ref, cos_ref, sin_ref, bias_ref, g_ref, o_ref,
                 qs_ref, ks_ref, vs_ref, kcs_ref, vcs_ref, *, block_type):
    cos = cos_ref[...]
    sin = sin_ref[...]

    def rope(x):
        return x * cos + pltpu.roll(x, HEAD_DIM // 2, 1) * sin

    qs_ref[...] = rope(q_ref[...]).astype(BF16)
    ks_ref[...] = rope(k_ref[...]).astype(BF16)
    vs_ref[...] = v_ref[...].astype(BF16)
    kcs_ref[...] = kc_ref[...].astype(BF16)
    vcs_ref[...] = vc_ref[...].astype(BF16)

    scale = HEAD_DIM ** -0.5
    qb = Q_ROWS * GRID_W
    kb = K_ROWS * GRID_W
    n_blocks = q_ref.shape[0] // qb
    max_start = q_ref.shape[0] // GRID_W - K_ROWS

    def body(j, carry):
        q0 = pl.multiple_of(j * qb, qb)
        k_row = jnp.clip(j * Q_ROWS - NA_ROWS // 2, 0, max_start)
        k0 = pl.multiple_of(k_row * GRID_W, qb)
        q = qs_ref[pl.ds(q0, qb), :]
        btype = 0
        for jj, tt in enumerate(block_type):
            btype = jnp.where(j == jj, tt, btype)
        s_loc = _dot_nt(q, ks_ref[pl.ds(k0, kb), :]) * scale + bias_ref[0, btype]
        s_ctx = _dot_nt(q, kcs_ref[...]) * scale
        m = jnp.maximum(jnp.max(s_loc, axis=-1, keepdims=True), jnp.max(s_ctx, axis=-1, keepdims=True))
        p_loc = jnp.exp(s_loc - m)
        p_ctx = jnp.exp(s_ctx - m)
        denom = jnp.sum(p_loc, axis=-1, keepdims=True) + jnp.sum(p_ctx, axis=-1, keepdims=True)
        o = _dot(p_loc.astype(BF16), vs_ref[pl.ds(k0, kb), :]) + _dot(p_ctx.astype(BF16), vcs_ref[...])
        o = o / denom
        o_ref[pl.ds(q0, qb), :] = (_rms(o) * g_ref[...]).astype(BF16)
        return carry

    lax.fori_loop(0, n_blocks, body, 0)


def _attn_call(proj, proj_ctx, cos_t, sin_t, bias, g_attn, batch, seq, ctx_len):
    d_attn = N_HEADS * HEAD_DIM
    q_blk = (proj.shape[1] - 3 * d_attn) // HEAD_DIM
    n_types = bias.shape[1]
    _, block_type = _block_row_offsets(seq // GRID_W)
    qb, kb = Q_ROWS * GRID_W, K_ROWS * GRID_W
    return pl.pallas_call(
        functools.partial(_attn_kernel, block_type=block_type),
        grid=(N_HEADS, batch),
        in_specs=[
            pl.BlockSpec((seq, HEAD_DIM), lambda h, b: (b, q_blk + h)),
            pl.BlockSpec((seq, HEAD_DIM), lambda h, b: (b, q_blk + N_HEADS + h)),
            pl.BlockSpec((seq, HEAD_DIM), lambda h, b: (b, q_blk + 2 * N_HEADS + h)),
            pl.BlockSpec((ctx_len, HEAD_DIM), lambda h, b: (b, h)),
            pl.BlockSpec((ctx_len, HEAD_DIM), lambda h, b: (b, N_HEADS + h)),
            pl.BlockSpec((seq, HEAD_DIM), lambda h, b: (0, 0)),
            pl.BlockSpec((seq, HEAD_DIM), lambda h, b: (0, 0)),
            pl.BlockSpec((1, n_types, qb, kb), lambda h, b: (h, 0, 0, 0)),
            pl.BlockSpec((1, HEAD_DIM), lambda h, b: (0, h)),
        ],
        out_specs=pl.BlockSpec((seq, HEAD_DIM), lambda h, b: (b, h)),
        out_shape=jax.ShapeDtypeStruct((batch * seq, d_attn), BF16),
        scratch_shapes=[
            pltpu.VMEM((seq, HEAD_DIM), BF16),
            pltpu.VMEM((seq, HEAD_DIM), BF16),
            pltpu.VMEM((seq, HEAD_DIM), BF16),
            pltpu.VMEM((ctx_len, HEAD_DIM), BF16),
            pltpu.VMEM((ctx_len, HEAD_DIM), BF16),
        ],
        compiler_params=pltpu.CompilerParams(
            dimension_semantics=("arbitrary", "arbitrary"), vmem_limit_bytes=_vmem_limit(48 << 20)),
        name="attn",
    )(proj, proj, proj, proj_ctx, proj_ctx, cos_t, sin_t, bias, g_attn)


def _rope_tables(seq):
    t = jnp.arange(seq, dtype=I32)
    pos = jnp.stack([t // GRID_W, t % GRID_W], axis=-1).astype(F32)
    n_freq = HEAD_DIM // 4
    inv_freq = ROPE_THETA ** (-jnp.arange(n_freq, dtype=F32) / n_freq)
    ang = pos[:, :, None] * inv_freq
    cos, sin = jnp.cos(ang).reshape(seq, 2 * n_freq), jnp.sin(ang).reshape(seq, 2 * n_freq)
    cos_t = jnp.concatenate([cos, cos], axis=-1)
    sin_t = jnp.concatenate([-sin, sin], axis=-1)
    return cos_t, sin_t


def _rope_order(w_in, n_lead_cols):
    d_attn = N_HEADS * HEAD_DIM
    lead, qk, v = w_in[:, :n_lead_cols], w_in[:, n_lead_cols:n_lead_cols + 2 * d_attn], w_in[:, n_lead_cols + 2 * d_attn:]
    qk = qk.reshape(qk.shape[0], 2 * N_HEADS, 2, 2, HEAD_DIM // 4).transpose(0, 1, 3, 2, 4).reshape(qk.shape)
    return jnp.concatenate([lead, qk, v], axis=1)


def _block_row_offsets(rows):
    n_blocks = rows // Q_ROWS
    kh = min(NA_ROWS, rows)
    sel = np.zeros((n_blocks, Q_ROWS, K_ROWS), np.int32)
    for j in range(n_blocks):
        k_start = int(np.clip(j * Q_ROWS - NA_ROWS // 2, 0, rows - K_ROWS))
        for a in range(Q_ROWS):
            r = j * Q_ROWS + a
            r0 = int(np.clip(r - kh // 2, 0, rows - kh))
            for b in range(K_ROWS):
                kr = k_start + b
                sel[j, a, b] = kr - r + NA_ROWS - 1 if r0 <= kr < r0 + kh else 2 * NA_ROWS - 1
    uniq, block_type = np.unique(sel.reshape(n_blocks, -1), axis=0, return_inverse=True)
    return uniq.reshape(-1, Q_ROWS, K_ROWS), tuple(int(v) for v in np.ravel(block_type))


def _bias_table(rpb, rows):
    w = GRID_W
    qc = np.arange(w)[:, None]
    kc = np.arange(w)[None, :]
    c0 = np.clip(qc - NA_COLS // 2, 0, w - NA_COLS)
    col_ok = (kc >= c0) & (kc < c0 + NA_COLS)
    col_off = np.clip(kc - qc + NA_COLS - 1, 0, 2 * NA_COLS - 2)
    toep = jnp.where(col_ok[None, None], rpb[:, :, col_off], MASK_VALUE)
    masked = jnp.full((rpb.shape[0], 1, w, w), MASK_VALUE, F32)
    toep = jnp.concatenate([toep, masked], axis=1)
    sel, _ = _block_row_offsets(rows)
    blocks = toep[:, sel]
    blocks = jnp.transpose(blocks, (0, 1, 2, 4, 3, 5))
    return blocks.reshape(rpb.shape[0], sel.shape[0], Q_ROWS * w, K_ROWS * w)


def _conv_kernel(ub_ref, uc_ref, ux_ref, w_ref, b_ref, g_ref, o_ref):
    p = uc_ref[...] * ux_ref[...]
    seq = p.shape[0]
    row = lax.broadcasted_iota(I32, p.shape, 0)
    prev = jnp.where(row == 0, 0.0, pltpu.roll(p, 1, 0))
    nxt = jnp.where(row == seq - 1, 0.0, pltpu.roll(p, seq - 1, 0))
    w = w_ref[...]
    conv = prev * w[0:1] + p * w[1:2] + nxt * w[2:3] + b_ref[...]
    y = ub_ref[...] * conv
    g = g_ref[...]
    for c0 in range(0, y.shape[1], CONV_GROUP_DIM):
        sl = slice(c0, c0 + CONV_GROUP_DIM)
        o_ref[:, sl] = (_rms(y[:, sl]) * g[:, sl]).astype(BF16)


def _conv_call(proj, conv_w, conv_b, g_conv, batch, seq, cb=256):
    d_conv = conv_w.shape[1]
    nc = d_conv // cb
    return pl.pallas_call(
        _conv_kernel,
        grid=(batch, nc),
        in_specs=[
            pl.BlockSpec((seq, cb), lambda b, c: (b, c)),
            pl.BlockSpec((seq, cb), lambda b, c: (b, nc + c)),
            pl.BlockSpec((seq, cb), lambda b, c: (b, 2 * nc + c)),
            pl.BlockSpec((conv_w.shape[0], cb), lambda b, c: (0, c)),
            pl.BlockSpec((1, cb), lambda b, c: (0, c)),
            pl.BlockSpec((1, cb), lambda b, c: (0, c)),
        ],
        out_specs=pl.BlockSpec((seq, cb), lambda b, c: (b, c)),
        out_shape=jax.ShapeDtypeStruct((batch * seq, d_conv), BF16),
        compiler_params=pltpu.CompilerParams(
            dimension_semantics=("arbitrary", "arbitrary"), vmem_limit_bytes=_vmem_limit(48 << 20)),
        name="conv",
    )(proj, proj, proj, conv_w, conv_b, g_conv)


def _outproj_kernel(yc_ref, ya_ref, w_ref, x_ref, gpost_ref, gate_ref, gpre_ref, sh_ref, sc_ref, wr_ref,
                    x1_ref, t_ref, lg_ref):
    dc = yc_ref.shape[1]
    y = _dot(yc_ref[...], w_ref[0:dc, :]) + _dot(ya_ref[...], w_ref[dc:, :])
    x1 = x_ref[...] + gate_ref[0] * (_rms(y) * gpost_ref[...])
    x1_ref[...] = x1
    t = (_rms(x1) * gpre_ref[...]) * (1.0 + sc_ref[0]) + sh_ref[0]
    t_ref[...] = t
    wr = wr_ref[...]
    wr_hi = wr.astype(BF16)
    wr_lo = (wr - wr_hi.astype(F32)).astype(BF16)
    t_hi = t.astype(BF16)
    t_lo = (t - t_hi.astype(F32)).astype(BF16)
    logits = _dot_nt(t_hi, wr_hi) + (_dot_nt(t_lo, wr_hi) + _dot_nt(t_hi, wr_lo))
    lg_ref[...] = logits.T


def _outproj_call(yc, ya, w_out_bf, x2, g_post, g_pre, mod3, w_router_t, seq, tm=512):
    t, d = x2.shape
    dc = yc.shape[1]
    ne = w_router_t.shape[0]
    tiles_per_seq = seq // tm
    row = lambda i: i // tiles_per_seq
    return pl.pallas_call(
        _outproj_kernel,
        grid=(t // tm,),
        in_specs=[
            pl.BlockSpec((tm, dc), lambda i: (i, 0)),
            pl.BlockSpec((tm, ya.shape[1]), lambda i: (i, 0)),
            pl.BlockSpec(w_out_bf.shape, lambda i: (0, 0), pipeline_mode=pl.Buffered(1)),
            pl.BlockSpec((tm, d), lambda i: (i, 0)),
            pl.BlockSpec((1, d), lambda i: (0, 0)),
            pl.BlockSpec((1, 1, d), lambda i: (row(i), 0, 2)),
            pl.BlockSpec((1, d), lambda i: (0, 0)),
            pl.BlockSpec((1, 1, d), lambda i: (row(i), 0, 3)),
            pl.BlockSpec((1, 1, d), lambda i: (row(i), 0, 4)),
            pl.BlockSpec((ne, d), lambda i: (0, 0)),
        ],
        out_specs=[
            pl.BlockSpec((tm, d), lambda i: (i, 0)),
            pl.BlockSpec((tm, d), lambda i: (i, 0)),
            pl.BlockSpec((ne, tm), lambda i: (0, i)),
        ],
        out_shape=[
            jax.ShapeDtypeStruct((t, d), F32),
            jax.ShapeDtypeStruct((t, d), F32),
            jax.ShapeDtypeStruct((ne, t), F32),
        ],
        compiler_params=pltpu.CompilerParams(
            dimension_semantics=("arbitrary",), vmem_limit_bytes=_vmem_limit(57 << 20)),
        name="out_proj",
    )(yc, ya, w_out_bf, x2, g_post, mod3, g_pre, mod3, mod3, w_router_t)


def _first_argmax_mask(v, axis):
    m = jnp.max(v, axis=axis, keepdims=True)
    idx = lax.broadcasted_iota(I32, v.shape, axis)
    big = v.shape[axis]
    first = jnp.min(jnp.where(v == m, idx, big), axis=axis, keepdims=True)
    return idx == first, m


def _router_kernel(lg_ref, rb_ref, ek_ref, rk_ref, gk_ref, cnt_ref, carry_ref):
    ne, tm = lg_ref.shape
    per = ne // N_EXPERT_GROUPS

    @pl.when(pl.program_id(0) == 0)
    def _():
        carry_ref[...] = jnp.zeros_like(carry_ref)

    scores = jax.nn.sigmoid(lg_ref[...])
    sel = scores + rb_ref[...]
    s3 = sel.reshape(N_EXPERT_GROUPS, per, tm)
    hit1, m1 = _first_argmax_mask(s3, 1)
    m2 = jnp.max(jnp.where(hit1, -jnp.inf, s3), axis=1, keepdims=True)
    grp = (m1 + m2).reshape(N_EXPERT_GROUPS, tm)
    gmask = jnp.zeros(grp.shape, jnp.bool_)
    for _ in range(TOPK_GROUPS):
        hit, _ = _first_argmax_mask(jnp.where(gmask, -jnp.inf, grp), 0)
        gmask = gmask | hit
    emask = jnp.broadcast_to(gmask.reshape(N_EXPERT_GROUPS, 1, tm), s3.shape).reshape(ne, tm)
    cand = jnp.where(emask, sel, -jnp.inf)
    chosen = jnp.zeros(cand.shape, jnp.bool_)
    hits = []
    for _ in range(TOP_K):
        hit, _ = _first_argmax_mask(jnp.where(chosen, -jnp.inf, cand), 0)
        hit = hit & ~chosen
        hits.append(hit)
        chosen = chosen | hit
    wsel = jnp.where(chosen, scores, 0.0)
    gates = wsel / jnp.sum(wsel, axis=0, keepdims=True) * ROUTED_SCALE

    r_i = lax.broadcasted_iota(I32, (tm, tm), 0)
    c_i = lax.broadcasted_iota(I32, (tm, tm), 1)
    tri = jnp.where(r_i <= c_i, 1.0, 0.0).astype(BF16)
    csum = _dot(jnp.where(chosen, 1.0, 0.0).astype(BF16), tri)
    carry = carry_ref[:, 0:1]
    rank = carry + csum - 1.0
    eidx = lax.broadcasted_iota(I32, (ne, tm), 0).astype(F32)

    def pick(hit, v):
        return jnp.sum(jnp.where(hit, v, 0.0), axis=0, keepdims=True)

    ek_ref[...] = jnp.concatenate([pick(h, eidx) for h in hits], axis=0).astype(I32)
    rk_ref[...] = jnp.concatenate([pick(h, rank) for h in hits], axis=0).astype(I32)
    gk_rows = jnp.concatenate([pick(h, gates) for h in hits] + [jnp.zeros((LANES - TOP_K, tm), F32)], axis=0)
    gk_ref[...] = gk_rows.T
    new_carry = carry + csum[:, tm - 1:tm]
    carry_ref[...] = jnp.broadcast_to(new_carry, carry_ref.shape)
    cnt_ref[...] = jnp.broadcast_to(new_carry, cnt_ref.shape).astype(I32)


def _router_call(logits_t, router_bias_col, tm=512):
    ne, t = logits_t.shape
    return pl.pallas_call(
        _router_kernel,
        grid=(t // tm,),
        in_specs=[
            pl.BlockSpec((ne, tm), lambda i: (0, i)),
            pl.BlockSpec((ne, 1), lambda i: (0, 0)),
        ],
        out_specs=[
            pl.BlockSpec((TOP_K, tm), lambda i: (0, i)),
            pl.BlockSpec((TOP_K, tm), lambda i: (0, i)),
            pl.BlockSpec((tm, LANES), lambda i: (i, 0)),
            pl.BlockSpec((ne, LANES), lambda i: (0, 0)),
        ],
        out_shape=[
            jax.ShapeDtypeStruct((TOP_K, t), I32),
            jax.ShapeDtypeStruct((TOP_K, t), I32),
            jax.ShapeDtypeStruct((t, LANES), F32),
            jax.ShapeDtypeStruct((ne, LANES), I32),
        ],
        scratch_shapes=[pltpu.VMEM((ne, LANES), F32)],
        compiler_params=pltpu.CompilerParams(dimension_semantics=("arbitrary",)),
        name="router",
    )(logits_t, router_bias_col)


def _routing_tables(ek, rk, counts, n_tiles_max, tokens_per_step):
    tg = EXPERT_ROW_TILE
    k, t = ek.shape
    expert_ids = jnp.arange(counts.shape[0], dtype=I32)
    padded = (counts + tg - 1) // tg * tg
    ends = jnp.cumsum(padded)
    offsets = ends - padded
    compact_offsets = jnp.cumsum(counts) - counts
    dest = rk + jnp.sum(jnp.where(ek[..., None] == expert_ids, offsets, 0), axis=-1)
    tok = jnp.arange(t, dtype=I32)
    tok_sorted = lax.rem(jnp.sort((ek * t + tok).reshape(-1)), t)
    tok_sorted = jnp.concatenate([tok_sorted, jnp.zeros((IDX_WINDOW,), I32)])
    tile_ends = ends // tg
    n_valid = tile_ends[-1]
    tile_ids = jnp.arange(n_tiles_max, dtype=I32)
    tile_expert = jnp.sum(tile_ids[:, None] >= tile_ends[None, :], axis=1).astype(I32)
    last_expert = jnp.sum(n_valid - 1 >= tile_ends).astype(I32)
    tile_expert = jnp.where(tile_ids < n_valid, tile_expert, last_expert)
    tile_start = compact_offsets[tile_expert] + (tile_ids * tg - offsets[tile_expert])
    tile_start = jnp.where(tile_ids < n_valid, tile_start, 0).astype(I32)
    dest_steps = dest.reshape(k, t // tokens_per_step, tokens_per_step).transpose(1, 0, 2).reshape(-1)
    return dest_steps, tok_sorted, tile_expert, n_valid.reshape(1).astype(I32), tile_start


def _experts_kernel(te_ref, nv_ref, ts_ref, tok_hbm, t_hbm, wg_ref, wu_ref, wd_ref, y_ref, *scratch):
    idx_bufs = scratch[:2]
    x_bufs = scratch[2:2 + ROW_BUFFERS]
    wg_s, wu_s, wd_s, sem_idx, sem_row = scratch[2 + ROW_BUFFERS:]
    i = pl.program_id(0)
    nv = nv_ref[0]
    tg = y_ref.shape[0]
    ahead = ROW_LOOKAHEAD

    def tile_of(j):
        return jnp.minimum(j, nv - 1)

    def idx_copy(j, j_static):
        window = pl.multiple_of(lax.shift_left(lax.shift_right_logical(ts_ref[tile_of(j)], 10), 10), IDX_ALIGN)
        slot = j_static % 2
        return pltpu.make_async_copy(tok_hbm.at[pl.ds(window, IDX_WINDOW)], idx_bufs[slot], sem_idx.at[slot])

    def row_copy(j, j_static, r):
        base = ts_ref[tile_of(j)] & (IDX_ALIGN - 1)
        slot = j_static % ROW_BUFFERS
        return pltpu.make_async_copy(t_hbm.at[idx_bufs[j_static % 2][base + r]], x_bufs[slot].at[r],
                                     sem_row.at[slot])

    def wait_rows(slot):
        pltpu.make_async_copy(t_hbm.at[pl.ds(0, tg)], x_bufs[slot], sem_row.at[slot]).wait()

    def step(phase):
        if phase == 0:
            @pl.when(i == 0)
            def _():
                for j in range(ahead):
                    idx_copy(j, j).start()
                    idx_copy(j, j).wait()

                    def first_rows(r, c, j=j):
                        row_copy(j, j, r).start()
                        return c

                    lax.fori_loop(0, tg, first_rows, 0)
                idx_copy(ahead, ahead).start()

        idx_copy(i + ahead, phase + ahead).wait()
        idx_copy(i + ahead + 1, phase + ahead + 1).start()

        @pl.when((i == 0) | (te_ref[i] != te_ref[jnp.maximum(i - 1, 0)]))
        def _():
            wg_s[...] = wg_ref[0].astype(BF16)
            wu_s[...] = wu_ref[0].astype(BF16)
            wd_s[...] = wd_ref[0].astype(BF16)

        wait_rows(phase)
        x = x_bufs[phase][...].astype(BF16)
        hid = _silu(_dot(x, wg_s[...])) * _dot(x, wu_s[...])
        y_ref[...] = _dot(hid.astype(BF16), wd_s[...])
        for r in range(tg):
            row_copy(i + ahead, phase + ahead, r).start()

        @pl.when(i == nv - 1)
        def _():
            for other in range(1, ROW_BUFFERS):
                wait_rows((phase + other) % ROW_BUFFERS)
            idx_copy(i + ahead + 1, phase + ahead + 1).wait()

    for phase in range(ROW_BUFFERS):
        @pl.when((i < nv) & ((i & (ROW_BUFFERS - 1)) == phase))
        def _(phase=phase):
            step(phase)

    @pl.when(i >= nv)
    def _():
        y_ref[...] = jnp.zeros_like(y_ref)


def _experts_call(tile_expert, n_valid, tile_start, tok_sorted, t_f32, w_gate, w_up, w_down):
    _, d = t_f32.shape
    ne, _, df = w_gate.shape
    tg = EXPERT_ROW_TILE
    n_tiles = tile_expert.shape[0]
    grid_spec = pltpu.PrefetchScalarGridSpec(
        num_scalar_prefetch=3,
        grid=(n_tiles,),
        in_specs=[
            pl.BlockSpec(memory_space=pl.ANY),
            pl.BlockSpec(memory_space=pl.ANY),
            pl.BlockSpec((1, d, df), lambda i, te, nv, ts: (te[i], 0, 0)),
            pl.BlockSpec((1, d, df), lambda i, te, nv, ts: (te[i], 0, 0)),
            pl.BlockSpec((1, df, d), lambda i, te, nv, ts: (te[i], 0, 0)),
        ],
        out_specs=pl.BlockSpec((tg, d), lambda i, te, nv, ts: (i, 0)),
        scratch_shapes=(
            [pltpu.SMEM((IDX_WINDOW,), I32)] * 2
            + [pltpu.VMEM((tg, d), F32)] * ROW_BUFFERS
            + [pltpu.VMEM((d, df), BF16), pltpu.VMEM((d, df), BF16), pltpu.VMEM((df, d), BF16),
               pltpu.SemaphoreType.DMA((2,)), pltpu.SemaphoreType.DMA((ROW_BUFFERS,))]),
    )
    return pl.pallas_call(
        _experts_kernel,
        grid_spec=grid_spec,
        out_shape=jax.ShapeDtypeStruct((n_tiles * tg, d), F32),
        compiler_params=pltpu.CompilerParams(
            dimension_semantics=("arbitrary",), vmem_limit_bytes=_vmem_limit(57 << 20)),
        name="experts",
    )(tile_expert, n_valid, tile_start, tok_sorted, t_f32, w_gate, w_up, w_down)


def _combine_kernel(dest_hbm, y_hbm, gk_ref, t_ref, wsg_ref, wsu_ref, wsd_ref, x1_ref, gpost_ref, gate_ref, o_ref,
                    idx_smem, rows_ref, sem_idx, sem_row, *, tc):
    i = pl.program_id(0)
    n_idx = TOP_K * tc
    idx_copy = pltpu.make_async_copy(dest_hbm.at[pl.ds(pl.multiple_of(i * n_idx, n_idx), n_idx)], idx_smem, sem_idx)
    idx_copy.start()
    idx_copy.wait()

    def issue(tb, carry):
        for u in range(ISSUE_UNROLL):
            t = tb * ISSUE_UNROLL + u
            for k in range(TOP_K):
                pltpu.make_async_copy(y_hbm.at[idx_smem[k * tc + t]], rows_ref.at[k, t], sem_row).start(
                    priority=k % 2)
        return carry

    lax.fori_loop(0, tc // ISSUE_UNROLL, issue, 0)

    t = t_ref[...].astype(BF16)
    hid = _silu(_dot(t, wsg_ref[...])) * _dot(t, wsu_ref[...])
    f = _dot(hid.astype(BF16), wsd_ref[...])

    for k in range(TOP_K):
        pltpu.make_async_copy(y_hbm.at[pl.ds(0, tc)], rows_ref.at[k], sem_row).wait()
    g = gk_ref[...]
    for k in range(TOP_K):
        f = f + g[:, k:k + 1] * rows_ref[k]
    o_ref[...] = x1_ref[...] + gate_ref[0] * (_rms(f) * gpost_ref[...])


def _combine_call(dest_steps, y, gk, t_f32, wsg_bf, wsu_bf, wsd_bf, x1, g_post, mod3, seq):
    t, d = x1.shape
    tc = COMBINE_TOKENS
    tiles_per_seq = seq // tc
    return pl.pallas_call(
        functools.partial(_combine_kernel, tc=tc),
        grid=(t // tc,),
        in_specs=[
            pl.BlockSpec(memory_space=pl.ANY),
            pl.BlockSpec(memory_space=pl.ANY),
            pl.BlockSpec((tc, LANES), lambda i: (i, 0)),
            pl.BlockSpec((tc, d), lambda i: (i, 0)),
            pl.BlockSpec(wsg_bf.shape, lambda i: (0, 0)),
            pl.BlockSpec(wsu_bf.shape, lambda i: (0, 0)),
            pl.BlockSpec(wsd_bf.shape, lambda i: (0, 0)),
            pl.BlockSpec((tc, d), lambda i: (i, 0)),
            pl.BlockSpec((1, d), lambda i: (0, 0)),
            pl.BlockSpec((1, 1, d), lambda i: (i // tiles_per_seq, 0, 5)),
        ],
        out_specs=pl.BlockSpec((tc, d), lambda i: (i, 0)),
        out_shape=jax.ShapeDtypeStruct((t, d), F32),
        scratch_shapes=[
            pltpu.SMEM((TOP_K * tc,), I32),
            pltpu.VMEM((TOP_K, tc, d), F32),
            pltpu.SemaphoreType.DMA(()),
            pltpu.SemaphoreType.DMA(()),
        ],
        compiler_params=pltpu.CompilerParams(
            dimension_semantics=("arbitrary",), vmem_limit_bytes=_vmem_limit(48 << 20)),
        name="combine",
    )(dest_steps, y, gk, t_f32, wsg_bf, wsu_bf, wsd_bf, x1, g_post, mod3)


def kernel(x, c, ctx, c_ctx, w_mod, b_mod, g_pre_mix, g_post_mix, g_pre_ffn, g_post_ffn, w_in, conv_w, conv_b,
           rpb, g_conv_out, g_attn_out, w_out, w_router, router_bias, w_exp_gate, w_exp_up, w_exp_down,
           w_sh_gate, w_sh_up, w_sh_down):
    batch, seq, d = x.shape
    ctx_len = ctx.shape[1]
    n_tok = batch * seq
    assert w_mod.shape[0] == 1, "single-layer kernel"
    assert seq % (Q_ROWS * GRID_W) == 0 and seq // GRID_W >= K_ROWS
    assert w_router.shape[2] == N_EXPERTS and seq % COMBINE_TOKENS == 0
    assert EXPERT_ROW_TILE <= IDX_ALIGN and n_tok * TOP_K % EXPERT_ROW_TILE == 0
    d_attn = N_HEADS * HEAD_DIM
    kv_offset = w_in.shape[2] - 2 * d_attn
    row2 = lambda a: a.reshape(1, -1)

    x2 = x.reshape(n_tok, d)
    ctx2 = ctx.reshape(batch * ctx_len, d)

    n_rows = -(-(batch + 1) // 8) * 8
    cvec = jnp.concatenate([c, c_ctx[None], jnp.zeros((n_rows - batch - 1, d), F32)], axis=0)
    mod = _mod_call(cvec, w_mod[0], row2(b_mod[0]))
    mod3 = mod.reshape(n_rows, 1, 6 * d)

    w_in_bf = _rope_order(w_in[0].astype(BF16), w_in.shape[2] - 3 * d_attn)
    tm_in = 1024
    tiles_per_seq = seq // tm_in
    proj = _inproj_call(x2, row2(g_pre_mix[0]), mod3, w_in_bf, lambda i: i // tiles_per_seq,
                        tm_in, 1024, "in_proj")
    proj_ctx = _inproj_call(ctx2, row2(g_pre_mix[0]), mod3, w_in_bf[:, kv_offset:], lambda i: batch,
                            min(tm_in, ctx2.shape[0]), 1024, "in_proj_ctx")

    cos_t, sin_t = _rope_tables(seq)
    bias = _bias_table(rpb[0], seq // GRID_W)
    ya = _attn_call(proj, proj_ctx, cos_t, sin_t, bias, row2(g_attn_out[0]), batch, seq, ctx_len)
    yc = _conv_call(proj, conv_w[0], row2(conv_b[0]), row2(g_conv_out[0]), batch, seq)

    x1, t_f32, logits_t = _outproj_call(yc, ya, w_out[0].astype(BF16), x2, row2(g_post_mix[0]),
                                        row2(g_pre_ffn[0]), mod3, w_router[0].T, seq)
    ek, rk, gk, cnt = _router_call(logits_t, router_bias[0].reshape(-1, 1))

    n_tiles_max = n_tok * TOP_K // EXPERT_ROW_TILE + N_EXPERTS
    dest_c, tok_sorted, tile_expert, n_valid, tile_start = _routing_tables(
        ek, rk, cnt[:, 0], n_tiles_max, COMBINE_TOKENS)
    y = _experts_call(tile_expert, n_valid, tile_start, tok_sorted, t_f32,
                      w_exp_gate[0], w_exp_up[0], w_exp_down[0])
    out = _combine_call(dest_c, y, gk, t_f32, w_sh_gate[0].astype(BF16), w_sh_up[0].astype(BF16),
                        w_sh_down[0].astype(BF16), x1, row2(g_post_ffn[0]), mod3, seq)
    return out.reshape(batch, seq, d)
```

```python
import functools

import numpy as np
import jax
import jax.numpy as jnp
from jax import lax
from jax.experimental import pallas as pl
from jax.experimental.pallas import tpu as pltpu

F32 = jnp.float32
BF16 = jnp.bfloat16
I32 = jnp.int32

GRID_W = 64
N_HEADS = 8
HEAD_DIM = 128
NA_ROWS = 8
NA_COLS = 16
ROPE_THETA = 10000.0
CONV_GROUP_DIM = 128
N_EXPERTS = 64
N_EXPERT_GROUPS = 8
TOPK_GROUPS = 4
TOP_K = 8
ROUTED_SCALE = 2.5
EPS = 1e-6
MASK_VALUE = -1e30

Q_ROWS = 4
K_ROWS = 12
ATTN_BLOCKS_PER_ITER = 2

V7X_VMEM_BYTES = 64 * 1024 * 1024
LANES = 128
SUBLANES = 8

EXPERT_ROW_TILE = 512
ROW_BUFFERS = 2
ROW_LOOKAHEAD = ROW_BUFFERS - 1
COMBINE_TOKENS = 128
IDX_ALIGN = 1024
IDX_WINDOW = 2 * IDX_ALIGN
ISSUE_UNROLL = 8


def _vmem_limit(nbytes):
    return int(min(nbytes, V7X_VMEM_BYTES - 6 * 1024 * 1024))


def _rms(x):
    return x * lax.rsqrt(jnp.mean(x * x, axis=-1, keepdims=True) + EPS)


def _silu(x):
    return x * jax.nn.sigmoid(x)


def _dot(a, b):
    return jnp.dot(a, b, preferred_element_type=F32)


def _dot_nt(a, b, precision=None):
    return lax.dot_general(a, b, (((1,), (1,)), ((), ())), preferred_element_type=F32, precision=precision)


def _mod_kernel(c_ref, w_ref, b_ref, o_ref):
    a = _silu(c_ref[...]).astype(BF16)
    o_ref[...] = _dot(a, w_ref[...].astype(BF16)) + b_ref[...]


def _mod_call(cvec, w_mod, b_mod, tn=1024):
    rows, d = cvec.shape
    n = w_mod.shape[1]
    return pl.pallas_call(
        _mod_kernel,
        grid=(n // tn,),
        in_specs=[
            pl.BlockSpec((rows, d), lambda j: (0, 0)),
            pl.BlockSpec((d, tn), lambda j: (0, j)),
            pl.BlockSpec((1, tn), lambda j: (0, j)),
        ],
        out_specs=pl.BlockSpec((rows, tn), lambda j: (0, j)),
        out_shape=jax.ShapeDtypeStruct((rows, n), F32),
        compiler_params=pltpu.CompilerParams(
            dimension_semantics=("arbitrary",), vmem_limit_bytes=_vmem_limit(40 << 20)),
        name="mod",
    )(cvec, w_mod, b_mod)


def _inproj_kernel(x_ref, g_ref, sh_ref, sc_ref, w_ref, o_ref, h_ref):
    @pl.when(pl.program_id(1) == 0)
    def _():
        h = _rms(x_ref[...]) * g_ref[...]
        h = h * (1.0 + sc_ref[0]) + sh_ref[0]
        h_ref[...] = h.astype(BF16)

    o_ref[...] = _dot(h_ref[...], w_ref[...])


def _inproj_call(x2, g, mod3, w_bf, row_of_tile, tm, tn, name, first_col=0):
    t, d = x2.shape
    n = w_bf.shape[1] - first_col
    col0 = first_col // tn
    assert first_col % tn == 0
    return pl.pallas_call(
        _inproj_kernel,
        grid=(t // tm, n // tn),
        in_specs=[
            pl.BlockSpec((tm, d), lambda i, j: (i, 0)),
            pl.BlockSpec((1, d), lambda i, j: (0, 0)),
            pl.BlockSpec((1, 1, d), lambda i, j: (row_of_tile(i), 0, 0)),
            pl.BlockSpec((1, 1, d), lambda i, j: (row_of_tile(i), 0, 1)),
            pl.BlockSpec((d, tn), lambda i, j: (0, col0 + j)),
        ],
        out_specs=pl.BlockSpec((tm, tn), lambda i, j: (i, j)),
        out_shape=jax.ShapeDtypeStruct((t, n), F32),
        scratch_shapes=[pltpu.VMEM((tm, d), BF16)],
        compiler_params=pltpu.CompilerParams(
            dimension_semantics=("arbitrary", "arbitrary"), vmem_limit_bytes=_vmem_limit(52 << 20)),
        name=name,
    )(x2, g, mod3, mod3, w_bf)


def _attn_kernel(q_ref, k_ref, v_ref, kc_ref, vc_ref, cos_ref, sin_ref, bias_ref, g_ref, o_ref,
                 qs_ref, ks_ref, vs_ref, kcs_ref, vcs_ref, *, block_type):
    cos = cos_ref[...]
    sin = sin_ref[...]

    def rope(x):
        return x * cos + pltpu.roll(x, HEAD_DIM // 2, 1) * sin

    qs_ref[...] = rope(q_ref[...]).astype(BF16)
    ks_ref[...] = rope(k_ref[...]).astype(BF16)
    vs_ref[...] = v_ref[...].astype(BF16)
    kcs_ref[...] = kc_ref[...].astype(BF16)
    vcs_ref[...] = vc_ref[...].astype(BF16)

    scale = HEAD_DIM ** -0.5
    qb = Q_ROWS * GRID_W
    kb = K_ROWS * GRID_W
    n_blocks = q_ref.shape[0] // qb
    max_start = q_ref.shape[0] // GRID_W - K_ROWS

    def block(j):
        q0 = pl.multiple_of(j * qb, qb)
        k_row = jnp.clip(j * Q_ROWS - NA_ROWS // 2, 0, max_start)
        k0 = pl.multiple_of(k_row * GRID_W, qb)
        q = qs_ref[pl.ds(q0, qb), :]
        btype = 0
        for jj, tt in enumerate(block_type):
            btype = jnp.where(j == jj, tt, btype)
        s_loc = _dot_nt(q, ks_ref[pl.ds(k0, kb), :]) * scale + bias_ref[0, btype]
        s_ctx = _dot_nt(q, kcs_ref[...]) * scale
        m = jnp.maximum(jnp.max(s_loc, axis=-1, keepdims=True), jnp.max(s_ctx, axis=-1, keepdims=True))
        p_loc = jnp.exp(s_loc - m)
        p_ctx = jnp.exp(s_ctx - m)
        denom = jnp.sum(p_loc, axis=-1, keepdims=True) + jnp.sum(p_ctx, axis=-1, keepdims=True)
        o = _dot(p_loc.astype(BF16), vs_ref[pl.ds(k0, kb), :]) + _dot(p_ctx.astype(BF16), vcs_ref[...])
        o = o / denom
        o_ref[pl.ds(q0, qb), :] = (_rms(o) * g_ref[...]).astype(BF16)

    def body(p, carry):
        for u in range(ATTN_BLOCKS_PER_ITER):
            block(p * ATTN_BLOCKS_PER_ITER + u)
        return carry

    lax.fori_loop(0, n_blocks // ATTN_BLOCKS_PER_ITER, body, 0)


def _attn_call(proj, proj_ctx, cos_t, sin_t, bias, g_attn, batch, seq, ctx_len):
    d_attn = N_HEADS * HEAD_DIM
    q_blk = (proj.shape[1] - 3 * d_attn) // HEAD_DIM
    n_types = bias.shape[1]
    _, block_type = _block_row_offsets(seq // GRID_W)
    qb, kb = Q_ROWS * GRID_W, K_ROWS * GRID_W
    return pl.pallas_call(
        functools.partial(_attn_kernel, block_type=block_type),
        grid=(N_HEADS, batch),
        in_specs=[
            pl.BlockSpec((seq, HEAD_DIM), lambda h, b: (b, q_blk + h)),
            pl.BlockSpec((seq, HEAD_DIM), lambda h, b: (b, q_blk + N_HEADS + h)),
            pl.BlockSpec((seq, HEAD_DIM), lambda h, b: (b, q_blk + 2 * N_HEADS + h)),
            pl.BlockSpec((ctx_len, HEAD_DIM), lambda h, b: (b, h)),
            pl.BlockSpec((ctx_len, HEAD_DIM), lambda h, b: (b, N_HEADS + h)),
            pl.BlockSpec((seq, HEAD_DIM), lambda h, b: (0, 0)),
            pl.BlockSpec((seq, HEAD_DIM), lambda h, b: (0, 0)),
            pl.BlockSpec((1, n_types, qb, kb), lambda h, b: (h, 0, 0, 0)),
            pl.BlockSpec((1, HEAD_DIM), lambda h, b: (0, h)),
        ],
        out_specs=pl.BlockSpec((seq, HEAD_DIM), lambda h, b: (b, h)),
        out_shape=jax.ShapeDtypeStruct((batch * seq, d_attn), BF16),
        scratch_shapes=[
            pltpu.VMEM((seq, HEAD_DIM), BF16),
            pltpu.VMEM((seq, HEAD_DIM), BF16),
            pltpu.VMEM((seq, HEAD_DIM), BF16),
            pltpu.VMEM((ctx_len, HEAD_DIM), BF16),
            pltpu.VMEM((ctx_len, HEAD_DIM), BF16),
        ],
        compiler_params=pltpu.CompilerParams(
            dimension_semantics=("arbitrary", "arbitrary"), vmem_limit_bytes=_vmem_limit(48 << 20)),
        name="attn",
    )(proj, proj, proj, proj_ctx, proj_ctx, cos_t, sin_t, bias, g_attn)


def _rope_tables(seq):
    t = jnp.arange(seq, dtype=I32)
    pos = jnp.stack([t // GRID_W, t % GRID_W], axis=-1).astype(F32)
    n_freq = HEAD_DIM // 4
    inv_freq = ROPE_THETA ** (-jnp.arange(n_freq, dtype=F32) / n_freq)
    ang = pos[:, :, None] * inv_freq
    cos, sin = jnp.cos(ang).reshape(seq, 2 * n_freq), jnp.sin(ang).reshape(seq, 2 * n_freq)
    cos_t = jnp.concatenate([cos, cos], axis=-1)
    sin_t = jnp.concatenate([-sin, sin], axis=-1)
    return cos_t, sin_t


def _rope_order(w_in, n_lead_cols):
    d_attn = N_HEADS * HEAD_DIM
    lead, qk, v = w_in[:, :n_lead_cols], w_in[:, n_lead_cols:n_lead_cols + 2 * d_attn], w_in[:, n_lead_cols + 2 * d_attn:]
    qk = qk.reshape(qk.shape[0], 2 * N_HEADS, 2, 2, HEAD_DIM // 4).transpose(0, 1, 3, 2, 4).reshape(qk.shape)
    return jnp.concatenate([lead, qk, v], axis=1)


def _block_row_offsets(rows):
    n_blocks = rows // Q_ROWS
    kh = min(NA_ROWS, rows)
    sel = np.zeros((n_blocks, Q_ROWS, K_ROWS), np.int32)
    for j in range(n_blocks):
        k_start = int(np.clip(j * Q_ROWS - NA_ROWS // 2, 0, rows - K_ROWS))
        for a in range(Q_ROWS):
            r = j * Q_ROWS + a
            r0 = int(np.clip(r - kh // 2, 0, rows - kh))
            for b in range(K_ROWS):
                kr = k_start + b
                sel[j, a, b] = kr - r + NA_ROWS - 1 if r0 <= kr < r0 + kh else 2 * NA_ROWS - 1
    uniq, block_type = np.unique(sel.reshape(n_blocks, -1), axis=0, return_inverse=True)
    return uniq.reshape(-1, Q_ROWS, K_ROWS), tuple(int(v) for v in np.ravel(block_type))


def _bias_table(rpb, rows):
    w = GRID_W
    qc = np.arange(w)[:, None]
    kc = np.arange(w)[None, :]
    c0 = np.clip(qc - NA_COLS // 2, 0, w - NA_COLS)
    col_ok = (kc >= c0) & (kc < c0 + NA_COLS)
    col_off = np.clip(kc - qc + NA_COLS - 1, 0, 2 * NA_COLS - 2)
    toep = jnp.where(col_ok[None, None], rpb[:, :, col_off], MASK_VALUE)
    masked = jnp.full((rpb.shape[0], 1, w, w), MASK_VALUE, F32)
    toep = jnp.concatenate([toep, masked], axis=1)
    sel, _ = _block_row_offsets(rows)
    blocks = toep[:, sel]
    blocks = jnp.transpose(blocks, (0, 1, 2, 4, 3, 5))
    return blocks.reshape(rpb.shape[0], sel.shape[0], Q_ROWS * w, K_ROWS * w)


def _conv_kernel(ub_ref, uc_ref, ux_ref, w_ref, b_ref, g_ref, o_ref):
    p = uc_ref[...] * ux_ref[...]
    seq = p.shape[0]
    row = lax.broadcasted_iota(I32, p.shape, 0)
    prev = jnp.where(row == 0, 0.0, pltpu.roll(p, 1, 0))
    nxt = jnp.where(row == seq - 1, 0.0, pltpu.roll(p, seq - 1, 0))
    w = w_ref[...]
    conv = prev * w[0:1] + p * w[1:2] + nxt * w[2:3] + b_ref[...]
    y = ub_ref[...] * conv
    g = g_ref[...]
    for c0 in range(0, y.shape[1], CONV_GROUP_DIM):
        sl = slice(c0, c0 + CONV_GROUP_DIM)
        o_ref[:, sl] = (_rms(y[:, sl]) * g[:, sl]).astype(BF16)


def _conv_call(proj, conv_w, conv_b, g_conv, batch, seq, cb=256):
    d_conv = conv_w.shape[1]
    nc = d_conv // cb
    return pl.pallas_call(
        _conv_kernel,
        grid=(batch, nc),
        in_specs=[
            pl.BlockSpec((seq, cb), lambda b, c: (b, c)),
            pl.BlockSpec((seq, cb), lambda b, c: (b, nc + c)),
            pl.BlockSpec((seq, cb), lambda b, c: (b, 2 * nc + c)),
            pl.BlockSpec((conv_w.shape[0], cb), lambda b, c: (0, c)),
            pl.BlockSpec((1, cb), lambda b, c: (0, c)),
            pl.BlockSpec((1, cb), lambda b, c: (0, c)),
        ],
        out_specs=pl.BlockSpec((seq, cb), lambda b, c: (b, c)),
        out_shape=jax.ShapeDtypeStruct((batch * seq, d_conv), BF16),
        compiler_params=pltpu.CompilerParams(
            dimension_semantics=("arbitrary", "arbitrary"), vmem_limit_bytes=_vmem_limit(48 << 20)),
        name="conv",
    )(proj, proj, proj, conv_w, conv_b, g_conv)


def _outproj_kernel(yc_ref, ya_ref, w_ref, x_ref, gpost_ref, gate_ref, gpre_ref, sh_ref, sc_ref, wr_ref,
                    x1_ref, t_ref, lg_ref):
    dc = yc_ref.shape[1]
    y = _dot(yc_ref[...], w_ref[0:dc, :]) + _dot(ya_ref[...], w_ref[dc:, :])
    x1 = x_ref[...] + gate_ref[0] * (_rms(y) * gpost_ref[...])
    x1_ref[...] = x1
    t = (_rms(x1) * gpre_ref[...]) * (1.0 + sc_ref[0]) + sh_ref[0]
    t_ref[...] = t
    ne = wr_ref.shape[0]
    wr = wr_ref[...]
    wr_hi = wr.astype(BF16)
    wr_lo = (wr - wr_hi.astype(F32)).astype(BF16)
    t_hi = t.astype(BF16)
    t_lo = (t - t_hi.astype(F32)).astype(BF16)
    both = _dot_nt(t_hi, jnp.concatenate([wr_hi, wr_lo], axis=0))
    logits = both[:, :ne] + (_dot_nt(t_lo, wr_hi) + both[:, ne:])
    lg_ref[...] = logits.T


def _outproj_call(yc, ya, w_out_bf, x2, g_post, g_pre, mod3, w_router_t, seq, tm=512):
    t, d = x2.shape
    dc = yc.shape[1]
    ne = w_router_t.shape[0]
    tiles_per_seq = seq // tm
    row = lambda i: i // tiles_per_seq
    return pl.pallas_call(
        _outproj_kernel,
        grid=(t // tm,),
        in_specs=[
            pl.BlockSpec((tm, dc), lambda i: (i, 0)),
            pl.BlockSpec((tm, ya.shape[1]), lambda i: (i, 0)),
            pl.BlockSpec(w_out_bf.shape, lambda i: (0, 0), pipeline_mode=pl.Buffered(1)),
            pl.BlockSpec((tm, d), lambda i: (i, 0)),
            pl.BlockSpec((1, d), lambda i: (0, 0)),
            pl.BlockSpec((1, 1, d), lambda i: (row(i), 0, 2)),
            pl.BlockSpec((1, d), lambda i: (0, 0)),
            pl.BlockSpec((1, 1, d), lambda i: (row(i), 0, 3)),
            pl.BlockSpec((1, 1, d), lambda i: (row(i), 0, 4)),
            pl.BlockSpec((ne, d), lambda i: (0, 0)),
        ],
        out_specs=[
            pl.BlockSpec((tm, d), lambda i: (i, 0)),
            pl.BlockSpec((tm, d), lambda i: (i, 0)),
            pl.BlockSpec((ne, tm), lambda i: (0, i)),
        ],
        out_shape=[
            jax.ShapeDtypeStruct((t, d), F32),
            jax.ShapeDtypeStruct((t, d), F32),
            jax.ShapeDtypeStruct((ne, t), F32),
        ],
        compiler_params=pltpu.CompilerParams(
            dimension_semantics=("arbitrary",), vmem_limit_bytes=_vmem_limit(57 << 20)),
        name="out_proj",
    )(yc, ya, w_out_bf, x2, g_post, mod3, g_pre, mod3, mod3, w_router_t)


def _first_argmax_mask(v, axis):
    m = jnp.max(v, axis=axis, keepdims=True)
    idx = lax.broadcasted_iota(I32, v.shape, axis)
    big = v.shape[axis]
    first = jnp.min(jnp.where(v == m, idx, big), axis=axis, keepdims=True)
    return idx == first, m


def _router_kernel(lg_ref, rb_ref, ek_ref, rk_ref, gk_ref, cnt_ref, carry_ref):
    ne, tm = lg_ref.shape
    per = ne // N_EXPERT_GROUPS

    @pl.when(pl.program_id(0) == 0)
    def _():
        carry_ref[...] = jnp.zeros_like(carry_ref)

    scores = jax.nn.sigmoid(lg_ref[...])
    sel = scores + rb_ref[...]
    s3 = sel.reshape(N_EXPERT_GROUPS, per, tm)
    hit1, m1 = _first_argmax_mask(s3, 1)
    m2 = jnp.max(jnp.where(hit1, -jnp.inf, s3), axis=1, keepdims=True)
    grp = (m1 + m2).reshape(N_EXPERT_GROUPS, tm)
    gmask = jnp.zeros(grp.shape, jnp.bool_)
    for _ in range(TOPK_GROUPS):
        hit, _ = _first_argmax_mask(jnp.where(gmask, -jnp.inf, grp), 0)
        gmask = gmask | hit
    emask = jnp.broadcast_to(gmask.reshape(N_EXPERT_GROUPS, 1, tm), s3.shape).reshape(ne, tm)
    cand = jnp.where(emask, sel, -jnp.inf)
    chosen = jnp.zeros(cand.shape, jnp.bool_)
    hits = []
    for _ in range(TOP_K):
        hit, _ = _first_argmax_mask(jnp.where(chosen, -jnp.inf, cand), 0)
        hit = hit & ~chosen
        hits.append(hit)
        chosen = chosen | hit
    wsel = jnp.where(chosen, scores, 0.0)
    gates = wsel / jnp.sum(wsel, axis=0, keepdims=True) * ROUTED_SCALE

    r_i = lax.broadcasted_iota(I32, (tm, tm), 0)
    c_i = lax.broadcasted_iota(I32, (tm, tm), 1)
    tri = jnp.where(r_i <= c_i, 1.0, 0.0).astype(BF16)
    csum = _dot(jnp.where(chosen, 1.0, 0.0).astype(BF16), tri)
    carry = carry_ref[:, 0:1]
    rank = carry + csum - 1.0
    eidx = lax.broadcasted_iota(I32, (ne, tm), 0).astype(F32)

    def pick(hit, v):
        return jnp.sum(jnp.where(hit, v, 0.0), axis=0, keepdims=True)

    ek_ref[...] = jnp.concatenate([pick(h, eidx) for h in hits], axis=0).astype(I32)
    rk_ref[...] = jnp.concatenate([pick(h, rank) for h in hits], axis=0).astype(I32)
    gk_rows = jnp.concatenate([pick(h, gates) for h in hits] + [jnp.zeros((LANES - TOP_K, tm), F32)], axis=0)
    gk_ref[...] = gk_rows.T
    new_carry = carry + csum[:, tm - 1:tm]
    carry_ref[...] = jnp.broadcast_to(new_carry, carry_ref.shape)
    cnt_ref[...] = jnp.broadcast_to(new_carry, cnt_ref.shape).astype(I32)


def _router_call(logits_t, router_bias_col, tm=512):
    ne, t = logits_t.shape
    return pl.pallas_call(
        _router_kernel,
        grid=(t // tm,),
        in_specs=[
            pl.BlockSpec((ne, tm), lambda i: (0, i)),
            pl.BlockSpec((ne, 1), lambda i: (0, 0)),
        ],
        out_specs=[
            pl.BlockSpec((TOP_K, tm), lambda i: (0, i)),
            pl.BlockSpec((TOP_K, tm), lambda i: (0, i)),
            pl.BlockSpec((tm, LANES), lambda i: (i, 0)),
            pl.BlockSpec((ne, LANES), lambda i: (0, 0)),
        ],
        out_shape=[
            jax.ShapeDtypeStruct((TOP_K, t), I32),
            jax.ShapeDtypeStruct((TOP_K, t), I32),
            jax.ShapeDtypeStruct((t, LANES), F32),
            jax.ShapeDtypeStruct((ne, LANES), I32),
        ],
        scratch_shapes=[pltpu.VMEM((ne, LANES), F32)],
        compiler_params=pltpu.CompilerParams(dimension_semantics=("arbitrary",)),
        name="router",
    )(logits_t, router_bias_col)


def _routing_tables(ek, rk, counts, n_tiles_max, tokens_per_step):
    tg = EXPERT_ROW_TILE
    k, t = ek.shape
    padded = (counts + tg - 1) // tg * tg
    ends = jnp.cumsum(padded)
    offsets = ends - padded
    compact_offsets = jnp.cumsum(counts) - counts
    dest = rk
    for e in range(counts.shape[0]):
        dest = dest + jnp.where(ek == e, offsets[e], 0)
    tok = jnp.arange(t, dtype=I32)
    tok_sorted = lax.rem(jnp.sort((ek * t + tok).reshape(-1)), t)
    tok_sorted = jnp.concatenate([tok_sorted, jnp.zeros((IDX_WINDOW,), I32)])
    tile_ends = ends // tg
    n_valid = tile_ends[-1]
    tile_ids = jnp.arange(n_tiles_max, dtype=I32)
    tile_expert = jnp.sum(tile_ids[:, None] >= tile_ends[None, :], axis=1).astype(I32)
    last_expert = jnp.sum(n_valid - 1 >= tile_ends).astype(I32)
    tile_expert = jnp.where(tile_ids < n_valid, tile_expert, last_expert)
    tile_start = compact_offsets[tile_expert] + (tile_ids * tg - offsets[tile_expert])
    tile_start = jnp.where(tile_ids < n_valid, tile_start, 0).astype(I32)
    dest_steps = dest.reshape(k, t // tokens_per_step, tokens_per_step).transpose(1, 0, 2).reshape(-1)
    return dest_steps, tok_sorted, tile_expert, n_valid.reshape(1).astype(I32), tile_start


def _experts_kernel(te_ref, nv_ref, ts_ref, tok_hbm, t_hbm, wg_ref, wu_ref, wd_ref, y_ref, *scratch):
    idx_bufs = scratch[:2]
    x_bufs = scratch[2:2 + ROW_BUFFERS]
    wg_s, wu_s, wd_s, sem_idx, sem_row = scratch[2 + ROW_BUFFERS:]
    i = pl.program_id(0)
    nv = nv_ref[0]
    tg = y_ref.shape[0]
    ahead = ROW_LOOKAHEAD

    def tile_of(j):
        return jnp.minimum(j, nv - 1)

    def idx_copy(j, j_static):
        window = pl.multiple_of(lax.shift_left(lax.shift_right_logical(ts_ref[tile_of(j)], 10), 10), IDX_ALIGN)
        slot = j_static % 2
        return pltpu.make_async_copy(tok_hbm.at[pl.ds(window, IDX_WINDOW)], idx_bufs[slot], sem_idx.at[slot])

    def row_copy(j, j_static, r):
        base = ts_ref[tile_of(j)] & (IDX_ALIGN - 1)
        slot = j_static % ROW_BUFFERS
        return pltpu.make_async_copy(t_hbm.at[idx_bufs[j_static % 2][base + r]], x_bufs[slot].at[r],
                                     sem_row.at[slot])

    def wait_rows(slot):
        pltpu.make_async_copy(t_hbm.at[pl.ds(0, tg)], x_bufs[slot], sem_row.at[slot]).wait()

    def step(phase):
        if phase == 0:
            @pl.when(i == 0)
            def _():
                for j in range(ahead):
                    idx_copy(j, j).start()
                    idx_copy(j, j).wait()

                    def first_rows(r, c, j=j):
                        row_copy(j, j, r).start()
                        return c

                    lax.fori_loop(0, tg, first_rows, 0)
                idx_copy(ahead, ahead).start()

        idx_copy(i + ahead, phase + ahead).wait()
        idx_copy(i + ahead + 1, phase + ahead + 1).start()

        @pl.when((i == 0) | (te_ref[i] != te_ref[jnp.maximum(i - 1, 0)]))
        def _():
            wg_s[...] = wg_ref[0].astype(BF16)
            wu_s[...] = wu_ref[0].astype(BF16)
            wd_s[...] = wd_ref[0].astype(BF16)

        wait_rows(phase)
        x = x_bufs[phase][...].astype(BF16)
        hid = _silu(_dot(x, wg_s[...])) * _dot(x, wu_s[...])
        y_ref[...] = _dot(hid.astype(BF16), wd_s[...])
        for r in range(tg):
            row_copy(i + ahead, phase + ahead, r).start()

        @pl.when(i == nv - 1)
        def _():
            for other in range(1, ROW_BUFFERS):
                wait_rows((phase + other) % ROW_BUFFERS)
            idx_copy(i + ahead + 1, phase + ahead + 1).wait()

    for phase in range(ROW_BUFFERS):
        @pl.when((i < nv) & ((i & (ROW_BUFFERS - 1)) == phase))
        def _(phase=phase):
            step(phase)

    @pl.when(i >= nv)
    def _():
        y_ref[...] = jnp.zeros_like(y_ref)


def _experts_call(tile_expert, n_valid, tile_start, tok_sorted, t_f32, w_gate, w_up, w_down):
    _, d = t_f32.shape
    ne, _, df = w_gate.shape
    tg = EXPERT_ROW_TILE
    n_tiles = tile_expert.shape[0]
    grid_spec = pltpu.PrefetchScalarGridSpec(
        num_scalar_prefetch=3,
        grid=(n_tiles,),
        in_specs=[
            pl.BlockSpec(memory_space=pl.ANY),
            pl.BlockSpec(memory_space=pl.ANY),
            pl.BlockSpec((1, d, df), lambda i, te, nv, ts: (te[i], 0, 0)),
            pl.BlockSpec((1, d, df), lambda i, te, nv, ts: (te[i], 0, 0)),
            pl.BlockSpec((1, df, d), lambda i, te, nv, ts: (te[i], 0, 0)),
        ],
        out_specs=pl.BlockSpec((tg, d), lambda i, te, nv, ts: (i, 0)),
        scratch_shapes=(
            [pltpu.SMEM((IDX_WINDOW,), I32)] * 2
            + [pltpu.VMEM((tg, d), F32)] * ROW_BUFFERS
            + [pltpu.VMEM((d, df), BF16), pltpu.VMEM((d, df), BF16), pltpu.VMEM((df, d), BF16),
               pltpu.SemaphoreType.DMA((2,)), pltpu.SemaphoreType.DMA((ROW_BUFFERS,))]),
    )
    return pl.pallas_call(
        _experts_kernel,
        grid_spec=grid_spec,
        out_shape=jax.ShapeDtypeStruct((n_tiles * tg, d), F32),
        compiler_params=pltpu.CompilerParams(
            dimension_semantics=("arbitrary",), vmem_limit_bytes=_vmem_limit(57 << 20)),
        name="experts",
    )(tile_expert, n_valid, tile_start, tok_sorted, t_f32, w_gate, w_up, w_down)


def _combine_kernel(dest_hbm, y_hbm, gk_ref, t_ref, wsg_ref, wsu_ref, wsd_ref, x1_ref, gpost_ref, gate_ref, o_ref,
                    idx_smem, rows_ref, sem_idx, sem_row, *, tc):
    i = pl.program_id(0)
    n_idx = TOP_K * tc
    idx_copy = pltpu.make_async_copy(dest_hbm.at[pl.ds(pl.multiple_of(i * n_idx, n_idx), n_idx)], idx_smem, sem_idx)
    idx_copy.start()
    idx_copy.wait()

    def issue(tb, carry):
        for u in range(ISSUE_UNROLL):
            t = tb * ISSUE_UNROLL + u
            for k in range(TOP_K):
                pltpu.make_async_copy(y_hbm.at[idx_smem[k * tc + t]], rows_ref.at[k, t], sem_row).start(
                    priority=k % 2)
        return carry

    lax.fori_loop(0, tc // ISSUE_UNROLL, issue, 0)

    t = t_ref[...].astype(BF16)
    hid = _silu(_dot(t, wsg_ref[...])) * _dot(t, wsu_ref[...])
    f = _dot(hid.astype(BF16), wsd_ref[...])

    for k in range(TOP_K):
        pltpu.make_async_copy(y_hbm.at[pl.ds(0, tc)], rows_ref.at[k], sem_row).wait()
    g = gk_ref[...]
    for k in range(TOP_K):
        f = f + g[:, k:k + 1] * rows_ref[k]
    o_ref[...] = x1_ref[...] + gate_ref[0] * (_rms(f) * gpost_ref[...])


def _combine_call(dest_steps, y, gk, t_f32, wsg_bf, wsu_bf, wsd_bf, x1, g_post, mod3, seq):
    t, d = x1.shape
    tc = COMBINE_TOKENS
    tiles_per_seq = seq // tc
    return pl.pallas_call(
        functools.partial(_combine_kernel, tc=tc),
        grid=(t // tc,),
        in_specs=[
            pl.BlockSpec(memory_space=pl.ANY),
            pl.BlockSpec(memory_space=pl.ANY),
            pl.BlockSpec((tc, LANES), lambda i: (i, 0)),
            pl.BlockSpec((tc, d), lambda i: (i, 0)),
            pl.BlockSpec(wsg_bf.shape, lambda i: (0, 0)),
            pl.BlockSpec(wsu_bf.shape, lambda i: (0, 0)),
            pl.BlockSpec(wsd_bf.shape, lambda i: (0, 0)),
            pl.BlockSpec((tc, d), lambda i: (i, 0)),
            pl.BlockSpec((1, d), lambda i: (0, 0)),
            pl.BlockSpec((1, 1, d), lambda i: (i // tiles_per_seq, 0, 5)),
        ],
        out_specs=pl.BlockSpec((tc, d), lambda i: (i, 0)),
        out_shape=jax.ShapeDtypeStruct((t, d), F32),
        scratch_shapes=[
            pltpu.SMEM((TOP_K * tc,), I32),
            pltpu.VMEM((TOP_K, tc, d), F32),
            pltpu.SemaphoreType.DMA(()),
            pltpu.SemaphoreType.DMA(()),
        ],
        compiler_params=pltpu.CompilerParams(
            dimension_semantics=("arbitrary",), vmem_limit_bytes=_vmem_limit(48 << 20)),
        name="combine",
    )(dest_steps, y, gk, t_f32, wsg_bf, wsu_bf, wsd_bf, x1, g_post, mod3)


def kernel(x, c, ctx, c_ctx, w_mod, b_mod, g_pre_mix, g_post_mix, g_pre_ffn, g_post_ffn, w_in, conv_w, conv_b,
           rpb, g_conv_out, g_attn_out, w_out, w_router, router_bias, w_exp_gate, w_exp_up, w_exp_down,
           w_sh_gate, w_sh_up, w_sh_down):
    batch, seq, d = x.shape
    ctx_len = ctx.shape[1]
    n_tok = batch * seq
    assert w_mod.shape[0] == 1, "single-layer kernel"
    assert seq % (Q_ROWS * GRID_W) == 0 and seq // GRID_W >= K_ROWS
    assert w_router.shape[2] == N_EXPERTS and seq % COMBINE_TOKENS == 0
    assert EXPERT_ROW_TILE <= IDX_ALIGN and n_tok * TOP_K % EXPERT_ROW_TILE == 0
    d_attn = N_HEADS * HEAD_DIM
    kv_offset = w_in.shape[2] - 2 * d_attn
    row2 = lambda a: a.reshape(1, -1)

    x2 = x.reshape(n_tok, d)
    ctx2 = ctx.reshape(batch * ctx_len, d)

    n_rows = -(-(batch + 1) // 8) * 8
    cvec = jnp.concatenate([c, c_ctx[None], jnp.zeros((n_rows - batch - 1, d), F32)], axis=0)
    mod = _mod_call(cvec, w_mod[0], row2(b_mod[0]))
    mod3 = mod.reshape(n_rows, 1, 6 * d)

    w_in_bf = _rope_order(w_in[0].astype(BF16), w_in.shape[2] - 3 * d_attn)
    tm_in = 1024
    tiles_per_seq = seq // tm_in
    proj = _inproj_call(x2, row2(g_pre_mix[0]), mod3, w_in_bf, lambda i: i // tiles_per_seq,
                        tm_in, 1024, "in_proj")
    proj_ctx = _inproj_call(ctx2, row2(g_pre_mix[0]), mod3, w_in_bf, lambda i: batch,
                            min(tm_in, ctx2.shape[0]), 1024, "in_proj_ctx", first_col=kv_offset)

    cos_t, sin_t = _rope_tables(seq)
    bias = _bias_table(rpb[0], seq // GRID_W)
    ya = _attn_call(proj, proj_ctx, cos_t, sin_t, bias, row2(g_attn_out[0]), batch, seq, ctx_len)
    yc = _conv_call(proj, conv_w[0], row2(conv_b[0]), row2(g_conv_out[0]), batch, seq)

    x1, t_f32, logits_t = _outproj_call(yc, ya, w_out[0].astype(BF16), x2, row2(g_post_mix[0]),
                                        row2(g_pre_ffn[0]), mod3, w_router[0].T, seq)
    ek, rk, gk, cnt = _router_call(logits_t, router_bias[0].reshape(-1, 1))

    n_tiles_max = n_tok * TOP_K // EXPERT_ROW_TILE + N_EXPERTS
    dest_c, tok_sorted, tile_expert, n_valid, tile_start = _routing_tables(
        ek, rk, cnt[:, 0], n_tiles_max, COMBINE_TOKENS)
    y = _experts_call(tile_expert, n_valid, tile_start, tok_sorted, t_f32,
                      w_exp_gate[0], w_exp_up[0], w_exp_down[0])
    out = _combine_call(dest_c, y, gk, t_f32, w_sh_gate[0].astype(BF16), w_sh_up[0].astype(BF16),
                        w_sh_down[0].astype(BF16), x1, row2(g_post_ffn[0]), mod3, seq)
    return out.reshape(batch, seq, d)
```

```python
import functools

import numpy as np
import jax
import jax.numpy as jnp
from jax import lax
from jax.experimental import pallas as pl
from jax.experimental.pallas import tpu as pltpu

F32 = jnp.float32
BF16 = jnp.bfloat16
I32 = jnp.int32

GRID_W = 64
N_HEADS = 8
HEAD_DIM = 128
NA_ROWS = 8
NA_COLS = 16
ROPE_THETA = 10000.0
CONV_GROUP_DIM = 128
N_EXPERTS = 64
N_EXPERT_GROUPS = 8
TOPK_GROUPS = 4
TOP_K = 8
ROUTED_SCALE = 2.5
EPS = 1e-6
MASK_VALUE = -1e30

Q_ROWS = 4
K_ROWS = 12
ATTN_BLOCKS_PER_ITER = 2

V7X_VMEM_BYTES = 64 * 1024 * 1024
LANES = 128

EXPERT_ROW_TILE = 512
ROW_BUFFERS = 2
ROW_LOOKAHEAD = ROW_BUFFERS - 1
COMBINE_TOKENS = 128
IDX_ALIGN = 1024
IDX_WINDOW = 2 * IDX_ALIGN
ISSUE_UNROLL = 8


def _vmem_limit(nbytes):
    return int(min(nbytes, V7X_VMEM_BYTES - 6 * 1024 * 1024))


def _rms(x):
    return x * lax.rsqrt(jnp.mean(x * x, axis=-1, keepdims=True) + EPS)


def _silu(x):
    return x * jax.nn.sigmoid(x)


def _dot(a, b):
    return jnp.dot(a, b, preferred_element_type=F32)


def _dot_nt(a, b, precision=None):
    return lax.dot_general(a, b, (((1,), (1,)), ((), ())), preferred_element_type=F32, precision=precision)


def _mod_kernel(c_ref, w_ref, b_ref, o_ref):
    a = _silu(c_ref[...]).astype(BF16)
    o_ref[...] = _dot(a, w_ref[...].astype(BF16)) + b_ref[...]


def _mod_call(cvec, w_mod, b_mod, tn=1024):
    rows, d = cvec.shape
    n = w_mod.shape[1]
    return pl.pallas_call(
        _mod_kernel,
        grid=(n // tn,),
        in_specs=[
            pl.BlockSpec((rows, d), lambda j: (0, 0)),
            pl.BlockSpec((d, tn), lambda j: (0, j)),
            pl.BlockSpec((1, tn), lambda j: (0, j)),
        ],
        out_specs=pl.BlockSpec((rows, tn), lambda j: (0, j)),
        out_shape=jax.ShapeDtypeStruct((rows, n), F32),
        compiler_params=pltpu.CompilerParams(
            dimension_semantics=("arbitrary",), vmem_limit_bytes=_vmem_limit(40 << 20)),
        name="mod",
    )(cvec, w_mod, b_mod)


def _inproj_kernel(x_ref, g_ref, sh_ref, sc_ref, w_ref, o_ref, h_ref):
    @pl.when(pl.program_id(1) == 0)
    def _():
        h = _rms(x_ref[...]) * g_ref[...]
        h = h * (1.0 + sc_ref[0]) + sh_ref[0]
        h_ref[...] = h.astype(BF16)

    o_ref[...] = _dot(h_ref[...], w_ref[...])


def _inproj_call(x2, g, mod3, w_bf, row_of_tile, tm, tn, name, first_col=0):
    t, d = x2.shape
    n = w_bf.shape[1] - first_col
    col0 = first_col // tn
    assert first_col % tn == 0
    return pl.pallas_call(
        _inproj_kernel,
        grid=(t // tm, n // tn),
        in_specs=[
            pl.BlockSpec((tm, d), lambda i, j: (i, 0)),
            pl.BlockSpec((1, d), lambda i, j: (0, 0)),
            pl.BlockSpec((1, 1, d), lambda i, j: (row_of_tile(i), 0, 0)),
            pl.BlockSpec((1, 1, d), lambda i, j: (row_of_tile(i), 0, 1)),
            pl.BlockSpec((d, tn), lambda i, j: (0, col0 + j)),
        ],
        out_specs=pl.BlockSpec((tm, tn), lambda i, j: (i, j)),
        out_shape=jax.ShapeDtypeStruct((t, n), F32),
        scratch_shapes=[pltpu.VMEM((tm, d), BF16)],
        compiler_params=pltpu.CompilerParams(
            dimension_semantics=("arbitrary", "arbitrary"), vmem_limit_bytes=_vmem_limit(52 << 20)),
        name=name,
    )(x2, g, mod3, mod3, w_bf)


def _attn_kernel(q_ref, k_ref, v_ref, kc_ref, vc_ref, cos_ref, sin_ref, bias_ref, g_ref, o_ref,
                 qs_ref, ks_ref, vs_ref, kcs_ref, vcs_ref, *, block_type):
    cos = cos_ref[...]
    sin = sin_ref[...]

    def rope(x):
        return x * cos + pltpu.roll(x, HEAD_DIM // 2, 1) * sin

    qs_ref[...] = rope(q_ref[...]).astype(BF16)
    ks_ref[...] = rope(k_ref[...]).astype(BF16)
    vs_ref[...] = v_ref[...].astype(BF16)
    kcs_ref[...] = kc_ref[...].astype(BF16)
    vcs_ref[...] = vc_ref[...].astype(BF16)

    scale = HEAD_DIM ** -0.5
    qb = Q_ROWS * GRID_W
    kb = K_ROWS * GRID_W
    n_blocks = q_ref.shape[0] // qb
    max_start = q_ref.shape[0] // GRID_W - K_ROWS

    def block(j):
        q0 = pl.multiple_of(j * qb, qb)
        k_row = jnp.clip(j * Q_ROWS - NA_ROWS // 2, 0, max_start)
        k0 = pl.multiple_of(k_row * GRID_W, qb)
        q = qs_ref[pl.ds(q0, qb), :]
        btype = 0
        for jj, tt in enumerate(block_type):
            btype = jnp.where(j == jj, tt, btype)
        s_loc = _dot_nt(q, ks_ref[pl.ds(k0, kb), :]) * scale + bias_ref[0, btype]
        s_ctx = _dot_nt(q, kcs_ref[...]) * scale
        m = jnp.maximum(jnp.max(s_loc, axis=-1, keepdims=True), jnp.max(s_ctx, axis=-1, keepdims=True))
        p_loc = jnp.exp(s_loc - m)
        p_ctx = jnp.exp(s_ctx - m)
        denom = jnp.sum(p_loc, axis=-1, keepdims=True) + jnp.sum(p_ctx, axis=-1, keepdims=True)
        o = _dot(p_loc.astype(BF16), vs_ref[pl.ds(k0, kb), :]) + _dot(p_ctx.astype(BF16), vcs_ref[...])
        o = o / denom
        o_ref[pl.ds(q0, qb), :] = (_rms(o) * g_ref[...]).astype(BF16)

    def body(p, carry):
        for u in range(ATTN_BLOCKS_PER_ITER):
            block(p * ATTN_BLOCKS_PER_ITER + u)
        return carry

    lax.fori_loop(0, n_blocks // ATTN_BLOCKS_PER_ITER, body, 0)


def _attn_call(proj, proj_ctx, cos_t, sin_t, bias, g_attn, batch, seq, ctx_len):
    d_attn = N_HEADS * HEAD_DIM
    q_blk = (proj.shape[1] - 3 * d_attn) // HEAD_DIM
    n_types = bias.shape[1]
    _, block_type = _block_row_offsets(seq // GRID_W)
    qb, kb = Q_ROWS * GRID_W, K_ROWS * GRID_W
    return pl.pallas_call(
        functools.partial(_attn_kernel, block_type=block_type),
        grid=(N_HEADS, batch),
        in_specs=[
            pl.BlockSpec((seq, HEAD_DIM), lambda h, b: (b, q_blk + h)),
            pl.BlockSpec((seq, HEAD_DIM), lambda h, b: (b, q_blk + N_HEADS + h)),
            pl.BlockSpec((seq, HEAD_DIM), lambda h, b: (b, q_blk + 2 * N_HEADS + h)),
            pl.BlockSpec((ctx_len, HEAD_DIM), lambda h, b: (b, h)),
            pl.BlockSpec((ctx_len, HEAD_DIM), lambda h, b: (b, N_HEADS + h)),
            pl.BlockSpec((seq, HEAD_DIM), lambda h, b: (0, 0)),
            pl.BlockSpec((seq, HEAD_DIM), lambda h, b: (0, 0)),
            pl.BlockSpec((1, n_types, qb, kb), lambda h, b: (h, 0, 0, 0)),
            pl.BlockSpec((1, HEAD_DIM), lambda h, b: (0, h)),
        ],
        out_specs=pl.BlockSpec((seq, HEAD_DIM), lambda h, b: (b, h)),
        out_shape=jax.ShapeDtypeStruct((batch * seq, d_attn), BF16),
        scratch_shapes=[
            pltpu.VMEM((seq, HEAD_DIM), BF16),
            pltpu.VMEM((seq, HEAD_DIM), BF16),
            pltpu.VMEM((seq, HEAD_DIM), BF16),
            pltpu.VMEM((ctx_len, HEAD_DIM), BF16),
            pltpu.VMEM((ctx_len, HEAD_DIM), BF16),
        ],
        compiler_params=pltpu.CompilerParams(
            dimension_semantics=("arbitrary", "arbitrary"), vmem_limit_bytes=_vmem_limit(48 << 20)),
        name="attn",
    )(proj, proj, proj, proj_ctx, proj_ctx, cos_t, sin_t, bias, g_attn)


def _rope_tables(seq):
    t = jnp.arange(seq, dtype=I32)
    pos = jnp.stack([t // GRID_W, t % GRID_W], axis=-1).astype(F32)
    n_freq = HEAD_DIM // 4
    inv_freq = ROPE_THETA ** (-jnp.arange(n_freq, dtype=F32) / n_freq)
    ang = pos[:, :, None] * inv_freq
    cos, sin = jnp.cos(ang).reshape(seq, 2 * n_freq), jnp.sin(ang).reshape(seq, 2 * n_freq)
    cos_t = jnp.concatenate([cos, cos], axis=-1)
    sin_t = jnp.concatenate([-sin, sin], axis=-1)
    return cos_t, sin_t


def _rope_order(w_in, n_lead_cols):
    rows, cols = w_in.shape
    groups = cols // HEAD_DIM
    g0 = n_lead_cols // HEAD_DIM
    w5 = w_in.reshape(rows, groups, 2, 2, HEAD_DIM // 4)
    is_qk = (np.arange(groups) >= g0) & (np.arange(groups) < g0 + 2 * N_HEADS)
    w5 = jnp.where(is_qk[None, :, None, None, None], w5.transpose(0, 1, 3, 2, 4), w5)
    return w5.reshape(rows, cols)


def _block_row_offsets(rows):
    n_blocks = rows // Q_ROWS
    kh = min(NA_ROWS, rows)
    sel = np.zeros((n_blocks, Q_ROWS, K_ROWS), np.int32)
    for j in range(n_blocks):
        k_start = int(np.clip(j * Q_ROWS - NA_ROWS // 2, 0, rows - K_ROWS))
        for a in range(Q_ROWS):
            r = j * Q_ROWS + a
            r0 = int(np.clip(r - kh // 2, 0, rows - kh))
            for b in range(K_ROWS):
                kr = k_start + b
                sel[j, a, b] = kr - r + NA_ROWS - 1 if r0 <= kr < r0 + kh else 2 * NA_ROWS - 1
    uniq, block_type = np.unique(sel.reshape(n_blocks, -1), axis=0, return_inverse=True)
    return uniq.reshape(-1, Q_ROWS, K_ROWS), tuple(int(v) for v in np.ravel(block_type))


def _bias_table(rpb, rows):
    w = GRID_W
    qc = np.arange(w)[:, None]
    kc = np.arange(w)[None, :]
    c0 = np.clip(qc - NA_COLS // 2, 0, w - NA_COLS)
    col_ok = (kc >= c0) & (kc < c0 + NA_COLS)
    col_off = np.clip(kc - qc + NA_COLS - 1, 0, 2 * NA_COLS - 2)
    toep = jnp.where(col_ok[None, None], rpb[:, :, col_off], MASK_VALUE)
    masked = jnp.full((rpb.shape[0], 1, w, w), MASK_VALUE, F32)
    toep = jnp.concatenate([toep, masked], axis=1)
    sel, _ = _block_row_offsets(rows)
    blocks = toep[:, sel]
    blocks = jnp.transpose(blocks, (0, 1, 2, 4, 3, 5))
    return blocks.reshape(rpb.shape[0], sel.shape[0], Q_ROWS * w, K_ROWS * w)


def _conv_kernel(ub_ref, uc_ref, ux_ref, w_ref, b_ref, g_ref, o_ref):
    p = uc_ref[...] * ux_ref[...]
    seq = p.shape[0]
    row = lax.broadcasted_iota(I32, p.shape, 0)
    prev = jnp.where(row == 0, 0.0, pltpu.roll(p, 1, 0))
    nxt = jnp.where(row == seq - 1, 0.0, pltpu.roll(p, seq - 1, 0))
    w = w_ref[...]
    conv = prev * w[0:1] + p * w[1:2] + nxt * w[2:3] + b_ref[...]
    y = ub_ref[...] * conv
    g = g_ref[...]
    for c0 in range(0, y.shape[1], CONV_GROUP_DIM):
        sl = slice(c0, c0 + CONV_GROUP_DIM)
        o_ref[:, sl] = (_rms(y[:, sl]) * g[:, sl]).astype(BF16)


def _conv_call(proj, conv_w, conv_b, g_conv, batch, seq, cb=256):
    d_conv = conv_w.shape[1]
    nc = d_conv // cb
    return pl.pallas_call(
        _conv_kernel,
        grid=(batch, nc),
        in_specs=[
            pl.BlockSpec((seq, cb), lambda b, c: (b, c)),
            pl.BlockSpec((seq, cb), lambda b, c: (b, nc + c)),
            pl.BlockSpec((seq, cb), lambda b, c: (b, 2 * nc + c)),
            pl.BlockSpec((conv_w.shape[0], cb), lambda b, c: (0, c)),
            pl.BlockSpec((1, cb), lambda b, c: (0, c)),
            pl.BlockSpec((1, cb), lambda b, c: (0, c)),
        ],
        out_specs=pl.BlockSpec((seq, cb), lambda b, c: (b, c)),
        out_shape=jax.ShapeDtypeStruct((batch * seq, d_conv), BF16),
        compiler_params=pltpu.CompilerParams(
            dimension_semantics=("arbitrary", "arbitrary"), vmem_limit_bytes=_vmem_limit(48 << 20)),
        name="conv",
    )(proj, proj, proj, conv_w, conv_b, g_conv)


def _outproj_kernel(yc_ref, ya_ref, w_ref, x_ref, gpost_ref, gate_ref, gpre_ref, sh_ref, sc_ref, wr_ref,
                    x1_ref, t_ref, lg_ref):
    dc = yc_ref.shape[1]
    y = _dot(yc_ref[...], w_ref[0:dc, :]) + _dot(ya_ref[...], w_ref[dc:, :])
    x1 = x_ref[...] + gate_ref[0] * (_rms(y) * gpost_ref[...])
    x1_ref[...] = x1
    t = (_rms(x1) * gpre_ref[...]) * (1.0 + sc_ref[0]) + sh_ref[0]
    t_ref[...] = t
    ne = wr_ref.shape[0]
    wr = wr_ref[...]
    wr_hi = wr.astype(BF16)
    wr_lo = (wr - wr_hi.astype(F32)).astype(BF16)
    t_hi = t.astype(BF16)
    t_lo = (t - t_hi.astype(F32)).astype(BF16)
    both = _dot_nt(t_hi, jnp.concatenate([wr_hi, wr_lo], axis=0))
    logits = both[:, :ne] + (_dot_nt(t_lo, wr_hi) + both[:, ne:])
    lg_ref[...] = logits.T


def _outproj_call(yc, ya, w_out_bf, x2, g_post, g_pre, mod3, w_router_t, seq, tm=512):
    t, d = x2.shape
    dc = yc.shape[1]
    ne = w_router_t.shape[0]
    tiles_per_seq = seq // tm
    row = lambda i: i // tiles_per_seq
    return pl.pallas_call(
        _outproj_kernel,
        grid=(t // tm,),
        in_specs=[
            pl.BlockSpec((tm, dc), lambda i: (i, 0)),
            pl.BlockSpec((tm, ya.shape[1]), lambda i: (i, 0)),
            pl.BlockSpec(w_out_bf.shape, lambda i: (0, 0), pipeline_mode=pl.Buffered(1)),
            pl.BlockSpec((tm, d), lambda i: (i, 0)),
            pl.BlockSpec((1, d), lambda i: (0, 0)),
            pl.BlockSpec((1, 1, d), lambda i: (row(i), 0, 2)),
            pl.BlockSpec((1, d), lambda i: (0, 0)),
            pl.BlockSpec((1, 1, d), lambda i: (row(i), 0, 3)),
            pl.BlockSpec((1, 1, d), lambda i: (row(i), 0, 4)),
            pl.BlockSpec((ne, d), lambda i: (0, 0)),
        ],
        out_specs=[
            pl.BlockSpec((tm, d), lambda i: (i, 0)),
            pl.BlockSpec((tm, d), lambda i: (i, 0)),
            pl.BlockSpec((ne, tm), lambda i: (0, i)),
        ],
        out_shape=[
            jax.ShapeDtypeStruct((t, d), F32),
            jax.ShapeDtypeStruct((t, d), F32),
            jax.ShapeDtypeStruct((ne, t), F32),
        ],
        compiler_params=pltpu.CompilerParams(
            dimension_semantics=("arbitrary",), vmem_limit_bytes=_vmem_limit(57 << 20)),
        name="out_proj",
    )(yc, ya, w_out_bf, x2, g_post, mod3, g_pre, mod3, mod3, w_router_t)


def _first_argmax_mask(v, axis):
    m = jnp.max(v, axis=axis, keepdims=True)
    idx = lax.broadcasted_iota(I32, v.shape, axis)
    big = v.shape[axis]
    first = jnp.min(jnp.where(v == m, idx, big), axis=axis, keepdims=True)
    return idx == first, m


def _router_kernel(lg_ref, rb_ref, ek_ref, rk_ref, gk_ref, cnt_ref, carry_ref):
    ne, tm = lg_ref.shape
    per = ne // N_EXPERT_GROUPS

    @pl.when(pl.program_id(0) == 0)
    def _():
        carry_ref[...] = jnp.zeros_like(carry_ref)

    scores = jax.nn.sigmoid(lg_ref[...])
    sel = scores + rb_ref[...]
    s3 = sel.reshape(N_EXPERT_GROUPS, per, tm)
    hit1, m1 = _first_argmax_mask(s3, 1)
    m2 = jnp.max(jnp.where(hit1, -jnp.inf, s3), axis=1, keepdims=True)
    grp = (m1 + m2).reshape(N_EXPERT_GROUPS, tm)
    gmask = jnp.zeros(grp.shape, jnp.bool_)
    for _ in range(TOPK_GROUPS):
        hit, _ = _first_argmax_mask(jnp.where(gmask, -jnp.inf, grp), 0)
        gmask = gmask | hit
    emask = jnp.broadcast_to(gmask.reshape(N_EXPERT_GROUPS, 1, tm), s3.shape).reshape(ne, tm)
    cand = jnp.where(emask, sel, -jnp.inf)
    chosen = jnp.zeros(cand.shape, jnp.bool_)
    hits = []
    for _ in range(TOP_K):
        hit, _ = _first_argmax_mask(jnp.where(chosen, -jnp.inf, cand), 0)
        hit = hit & ~chosen
        hits.append(hit)
        chosen = chosen | hit
    wsel = jnp.where(chosen, scores, 0.0)
    gates = wsel / jnp.sum(wsel, axis=0, keepdims=True) * ROUTED_SCALE

    r_i = lax.broadcasted_iota(I32, (tm, tm), 0)
    c_i = lax.broadcasted_iota(I32, (tm, tm), 1)
    tri = jnp.where(r_i <= c_i, 1.0, 0.0).astype(BF16)
    csum = _dot(jnp.where(chosen, 1.0, 0.0).astype(BF16), tri)
    carry = carry_ref[:, 0:1]
    rank = carry + csum - 1.0
    eidx = lax.broadcasted_iota(I32, (ne, tm), 0).astype(F32)

    def pick(hit, v):
        return jnp.sum(jnp.where(hit, v, 0.0), axis=0, keepdims=True)

    ek_ref[...] = jnp.concatenate([pick(h, eidx) for h in hits], axis=0).astype(I32)
    rk_ref[...] = jnp.concatenate([pick(h, rank) for h in hits], axis=0).astype(I32)
    gk_rows = jnp.concatenate([pick(h, gates) for h in hits] + [jnp.zeros((LANES - TOP_K, tm), F32)], axis=0)
    gk_ref[...] = gk_rows.T
    new_carry = carry + csum[:, tm - 1:tm]
    carry_ref[...] = jnp.broadcast_to(new_carry, carry_ref.shape)
    cnt_ref[...] = jnp.broadcast_to(new_carry, cnt_ref.shape).astype(I32)


def _router_call(logits_t, router_bias_col, tm=512):
    ne, t = logits_t.shape
    return pl.pallas_call(
        _router_kernel,
        grid=(t // tm,),
        in_specs=[
            pl.BlockSpec((ne, tm), lambda i: (0, i)),
            pl.BlockSpec((ne, 1), lambda i: (0, 0)),
        ],
        out_specs=[
            pl.BlockSpec((TOP_K, tm), lambda i: (0, i)),
            pl.BlockSpec((TOP_K, tm), lambda i: (0, i)),
            pl.BlockSpec((tm, LANES), lambda i: (i, 0)),
            pl.BlockSpec((ne, LANES), lambda i: (0, 0)),
        ],
        out_shape=[
            jax.ShapeDtypeStruct((TOP_K, t), I32),
            jax.ShapeDtypeStruct((TOP_K, t), I32),
            jax.ShapeDtypeStruct((t, LANES), F32),
            jax.ShapeDtypeStruct((ne, LANES), I32),
        ],
        scratch_shapes=[pltpu.VMEM((ne, LANES), F32)],
        compiler_params=pltpu.CompilerParams(dimension_semantics=("arbitrary",)),
        name="router",
    )(logits_t, router_bias_col)


def _routing_tables(ek, rk, counts, n_tiles_max, tokens_per_step):
    tg = EXPERT_ROW_TILE
    k, t = ek.shape
    n_exp = counts.shape[0]
    padded = (counts + tg - 1) // tg * tg
    before = np.tril(np.ones((n_exp, n_exp), bool), -1)
    offsets = jnp.sum(jnp.where(before, padded[None, :], 0), axis=1)
    compact_offsets = jnp.sum(jnp.where(before, counts[None, :], 0), axis=1)
    ends = offsets + padded
    dest = rk
    for e in range(n_exp):
        dest = dest + jnp.where(ek == e, offsets[e], 0)
    tok = jnp.arange(t, dtype=I32)
    tok_sorted = lax.rem(jnp.sort((ek * t + tok).reshape(-1)), t)
    tok_sorted = jnp.concatenate([tok_sorted, jnp.zeros((IDX_WINDOW,), I32)])
    tile_ends = ends // tg
    n_valid = jnp.sum(padded) // tg
    tile_ids = jnp.arange(n_tiles_max, dtype=I32)
    tile_expert = jnp.sum(tile_ids[:, None] >= tile_ends[None, :], axis=1).astype(I32)
    last_expert = jnp.sum(n_valid - 1 >= tile_ends).astype(I32)
    tile_expert = jnp.where(tile_ids < n_valid, tile_expert, last_expert)
    own = tile_expert[:, None] == jnp.arange(n_exp, dtype=I32)[None, :]
    tile_start = tile_ids * tg + jnp.sum(jnp.where(own, (compact_offsets - offsets)[None, :], 0), axis=1)
    tile_start = jnp.where(tile_ids < n_valid, tile_start, 0).astype(I32)
    dest_steps = dest.reshape(k, t // tokens_per_step, tokens_per_step).transpose(1, 0, 2).reshape(-1)
    return dest_steps, tok_sorted, tile_expert, n_valid.reshape(1).astype(I32), tile_start


def _experts_kernel(te_ref, nv_ref, ts_ref, tok_hbm, t_hbm, wg_ref, wu_ref, wd_ref, y_ref, *scratch):
    idx_bufs = scratch[:2]
    x_bufs = scratch[2:2 + ROW_BUFFERS]
    wg_s, wu_s, wd_s, sem_idx, sem_row = scratch[2 + ROW_BUFFERS:]
    i = pl.program_id(0)
    nv = nv_ref[0]
    tg = y_ref.shape[0]
    ahead = ROW_LOOKAHEAD

    def tile_of(j):
        return jnp.minimum(j, nv - 1)

    def idx_copy(j, j_static):
        window = pl.multiple_of(lax.shift_left(lax.shift_right_logical(ts_ref[tile_of(j)], 10), 10), IDX_ALIGN)
        slot = j_static % 2
        return pltpu.make_async_copy(tok_hbm.at[pl.ds(window, IDX_WINDOW)], idx_bufs[slot], sem_idx.at[slot])

    def row_copy(j, j_static, r):
        base = ts_ref[tile_of(j)] & (IDX_ALIGN - 1)
        slot = j_static % ROW_BUFFERS
        return pltpu.make_async_copy(t_hbm.at[idx_bufs[j_static % 2][base + r]], x_bufs[slot].at[r],
                                     sem_row.at[slot])

    def wait_rows(slot):
        pltpu.make_async_copy(t_hbm.at[pl.ds(0, tg)], x_bufs[slot], sem_row.at[slot]).wait()

    def step(phase):
        if phase == 0:
            @pl.when(i == 0)
            def _():
                for j in range(ahead):
                    idx_copy(j, j).start()
                    idx_copy(j, j).wait()

                    def first_rows(r, c, j=j):
                        row_copy(j, j, r).start()
                        return c

                    lax.fori_loop(0, tg, first_rows, 0)
                idx_copy(ahead, ahead).start()

        idx_copy(i + ahead, phase + ahead).wait()
        idx_copy(i + ahead + 1, phase + ahead + 1).start()

        @pl.when((i == 0) | (te_ref[i] != te_ref[jnp.maximum(i - 1, 0)]))
        def _():
            wg_s[...] = wg_ref[0].astype(BF16)
            wu_s[...] = wu_ref[0].astype(BF16)
            wd_s[...] = wd_ref[0].astype(BF16)

        wait_rows(phase)
        x = x_bufs[phase][...].astype(BF16)
        hid = _silu(_dot(x, wg_s[...])) * _dot(x, wu_s[...])
        y_ref[...] = _dot(hid.astype(BF16), wd_s[...])
        for r in range(tg):
            row_copy(i + ahead, phase + ahead, r).start()

        @pl.when(i == nv - 1)
        def _():
            for other in range(1, ROW_BUFFERS):
                wait_rows((phase + other) % ROW_BUFFERS)
            idx_copy(i + ahead + 1, phase + ahead + 1).wait()

    for phase in range(ROW_BUFFERS):
        @pl.when((i < nv) & ((i & (ROW_BUFFERS - 1)) == phase))
        def _(phase=phase):
            step(phase)

    @pl.when(i >= nv)
    def _():
        y_ref[...] = jnp.zeros_like(y_ref)


def _experts_call(tile_expert, n_valid, tile_start, tok_sorted, t_f32, w_gate, w_up, w_down):
    _, d = t_f32.shape
    ne, _, df = w_gate.shape
    tg = EXPERT_ROW_TILE
    n_tiles = tile_expert.shape[0]
    grid_spec = pltpu.PrefetchScalarGridSpec(
        num_scalar_prefetch=3,
        grid=(n_tiles,),
        in_specs=[
            pl.BlockSpec(memory_space=pl.ANY),
            pl.BlockSpec(memory_space=pl.ANY),
            pl.BlockSpec((1, d, df), lambda i, te, nv, ts: (te[i], 0, 0)),
            pl.BlockSpec((1, d, df), lambda i, te, nv, ts: (te[i], 0, 0)),
            pl.BlockSpec((1, df, d), lambda i, te, nv, ts: (te[i], 0, 0)),
        ],
        out_specs=pl.BlockSpec((tg, d), lambda i, te, nv, ts: (i, 0)),
        scratch_shapes=(
            [pltpu.SMEM((IDX_WINDOW,), I32)] * 2
            + [pltpu.VMEM((tg, d), F32)] * ROW_BUFFERS
            + [pltpu.VMEM((d, df), BF16), pltpu.VMEM((d, df), BF16), pltpu.VMEM((df, d), BF16),
               pltpu.SemaphoreType.DMA((2,)), pltpu.SemaphoreType.DMA((ROW_BUFFERS,))]),
    )
    return pl.pallas_call(
        _experts_kernel,
        grid_spec=grid_spec,
        out_shape=jax.ShapeDtypeStruct((n_tiles * tg, d), F32),
        compiler_params=pltpu.CompilerParams(
            dimension_semantics=("arbitrary",), vmem_limit_bytes=_vmem_limit(57 << 20)),
        name="experts",
    )(tile_expert, n_valid, tile_start, tok_sorted, t_f32, w_gate, w_up, w_down)


def _combine_kernel(dest_hbm, y_hbm, gk_ref, t_ref, wsg_ref, wsu_ref, wsd_ref, x1_ref, gpost_ref, gate_ref, o_ref,
                    idx_smem, rows_ref, sem_idx, sem_row, *, tc):
    i = pl.program_id(0)
    n_idx = TOP_K * tc
    idx_copy = pltpu.make_async_copy(dest_hbm.at[pl.ds(pl.multiple_of(i * n_idx, n_idx), n_idx)], idx_smem, sem_idx)
    idx_copy.start()
    idx_copy.wait()

    def issue(tb, carry):
        for u in range(ISSUE_UNROLL):
            t = tb * ISSUE_UNROLL + u
            for k in range(TOP_K):
                pltpu.make_async_copy(y_hbm.at[idx_smem[k * tc + t]], rows_ref.at[k, t], sem_row).start()
        return carry

    lax.fori_loop(0, tc // ISSUE_UNROLL, issue, 0)

    t = t_ref[...].astype(BF16)
    hid = _silu(_dot(t, wsg_ref[...])) * _dot(t, wsu_ref[...])
    f = _dot(hid.astype(BF16), wsd_ref[...])

    for k in range(TOP_K):
        pltpu.make_async_copy(y_hbm.at[pl.ds(0, tc)], rows_ref.at[k], sem_row).wait()
    g = gk_ref[...]
    for k in range(TOP_K):
        f = f + g[:, k:k + 1] * rows_ref[k]
    o_ref[...] = x1_ref[...] + gate_ref[0] * (_rms(f) * gpost_ref[...])


def _combine_call(dest_steps, y, gk, t_f32, wsg_bf, wsu_bf, wsd_bf, x1, g_post, mod3, seq):
    t, d = x1.shape
    tc = COMBINE_TOKENS
    tiles_per_seq = seq // tc
    return pl.pallas_call(
        functools.partial(_combine_kernel, tc=tc),
        grid=(t // tc,),
        in_specs=[
            pl.BlockSpec(memory_space=pl.ANY),
            pl.BlockSpec(memory_space=pl.ANY),
            pl.BlockSpec((tc, LANES), lambda i: (i, 0)),
            pl.BlockSpec((tc, d), lambda i: (i, 0)),
            pl.BlockSpec(wsg_bf.shape, lambda i: (0, 0)),
            pl.BlockSpec(wsu_bf.shape, lambda i: (0, 0)),
            pl.BlockSpec(wsd_bf.shape, lambda i: (0, 0)),
            pl.BlockSpec((tc, d), lambda i: (i, 0)),
            pl.BlockSpec((1, d), lambda i: (0, 0)),
            pl.BlockSpec((1, 1, d), lambda i: (i // tiles_per_seq, 0, 5)),
        ],
        out_specs=pl.BlockSpec((tc, d), lambda i: (i, 0)),
        out_shape=jax.ShapeDtypeStruct((t, d), F32),
        scratch_shapes=[
            pltpu.SMEM((TOP_K * tc,), I32),
            pltpu.VMEM((TOP_K, tc, d), F32),
            pltpu.SemaphoreType.DMA(()),
            pltpu.SemaphoreType.DMA(()),
        ],
        compiler_params=pltpu.CompilerParams(
            dimension_semantics=("arbitrary",), vmem_limit_bytes=_vmem_limit(48 << 20)),
        name="combine",
    )(dest_steps, y, gk, t_f32, wsg_bf, wsu_bf, wsd_bf, x1, g_post, mod3)


def kernel(x, c, ctx, c_ctx, w_mod, b_mod, g_pre_mix, g_post_mix, g_pre_ffn, g_post_ffn, w_in, conv_w, conv_b,
           rpb, g_conv_out, g_attn_out, w_out, w_router, router_bias, w_exp_gate, w_exp_up, w_exp_down,
           w_sh_gate, w_sh_up, w_sh_down):
    batch, seq, d = x.shape
    ctx_len = ctx.shape[1]
    n_tok = batch * seq
    assert w_mod.shape[0] == 1, "single-layer kernel"
    assert seq % (Q_ROWS * GRID_W) == 0 and seq // GRID_W >= K_ROWS
    assert w_router.shape[2] == N_EXPERTS and seq % COMBINE_TOKENS == 0
    assert EXPERT_ROW_TILE <= IDX_ALIGN and n_tok * TOP_K % EXPERT_ROW_TILE == 0
    d_attn = N_HEADS * HEAD_DIM
    kv_offset = w_in.shape[2] - 2 * d_attn
    row2 = lambda a: a.reshape(1, -1)

    x2 = x.reshape(n_tok, d)
    ctx2 = ctx.reshape(batch * ctx_len, d)

    n_rows = -(-(batch + 1) // 8) * 8
    cvec = jnp.concatenate([c, c_ctx[None], jnp.zeros((n_rows - batch - 1, d), F32)], axis=0)
    mod = _mod_call(cvec, w_mod[0], row2(b_mod[0]))
    mod3 = mod.reshape(n_rows, 1, 6 * d)

    w_in_bf = _rope_order(w_in[0].astype(BF16), w_in.shape[2] - 3 * d_attn)
    tm_in = 1024
    tiles_per_seq = seq // tm_in
    proj = _inproj_call(x2, row2(g_pre_mix[0]), mod3, w_in_bf, lambda i: i // tiles_per_seq,
                        tm_in, 1024, "in_proj")
    proj_ctx = _inproj_call(ctx2, row2(g_pre_mix[0]), mod3, w_in_bf, lambda i: batch,
                            min(tm_in, ctx2.shape[0]), 1024, "in_proj_ctx", first_col=kv_offset)

    cos_t, sin_t = _rope_tables(seq)
    bias = _bias_table(rpb[0], seq // GRID_W)
    ya = _attn_call(proj, proj_ctx, cos_t, sin_t, bias, row2(g_attn_out[0]), batch, seq, ctx_len)
    yc = _conv_call(proj, conv_w[0], row2(conv_b[0]), row2(g_conv_out[0]), batch, seq)

    x1, t_f32, logits_t = _outproj_call(yc, ya, w_out[0].astype(BF16), x2, row2(g_post_mix[0]),
                                        row2(g_pre_ffn[0]), mod3, w_router[0].T, seq)
    ek, rk, gk, cnt = _router_call(logits_t, router_bias[0].reshape(-1, 1))

    n_tiles_max = n_tok * TOP_K // EXPERT_ROW_TILE + N_EXPERTS
    dest_c, tok_sorted, tile_expert, n_valid, tile_start = _routing_tables(
        ek, rk, cnt[:, 0], n_tiles_max, COMBINE_TOKENS)
    y = _experts_call(tile_expert, n_valid, tile_start, tok_sorted, t_f32,
                      w_exp_gate[0], w_exp_up[0], w_exp_down[0])
    out = _combine_call(dest_c, y, gk, t_f32, w_sh_gate[0].astype(BF16), w_sh_up[0].astype(BF16),
                        w_sh_down[0].astype(BF16), x1, row2(g_post_ffn[0]), mod3, seq)
    return out.reshape(batch, seq, d)
```

```python
import functools

import numpy as np
import jax
import jax.numpy as jnp
from jax import lax
from jax.experimental import pallas as pl
from jax.experimental.pallas import tpu as pltpu

F32 = jnp.float32
BF16 = jnp.bfloat16
I32 = jnp.int32

GRID_W = 64
N_HEADS = 8
HEAD_DIM = 128
NA_ROWS = 8
NA_COLS = 16
ROPE_THETA = 10000.0
CONV_GROUP_DIM = 128
N_EXPERTS = 64
N_EXPERT_GROUPS = 8
TOPK_GROUPS = 4
TOP_K = 8
ROUTED_SCALE = 2.5
EPS = 1e-6
MASK_VALUE = -1e30

Q_ROWS = 4
K_ROWS = 12
ATTN_BLOCKS_PER_ITER = 2

V7X_VMEM_BYTES = 64 * 1024 * 1024
LANES = 128

EXPERT_ROW_TILE = 512
ROW_BUFFERS = 2
ROW_LOOKAHEAD = ROW_BUFFERS - 1
COMBINE_TOKENS = 128
IDX_ALIGN = 1024
IDX_WINDOW = 2 * IDX_ALIGN


def _vmem_limit(nbytes):
    return int(min(nbytes, V7X_VMEM_BYTES - 6 * 1024 * 1024))


def _rms(x):
    return x * lax.rsqrt(jnp.mean(x * x, axis=-1, keepdims=True) + EPS)


def _silu(x):
    return x * jax.nn.sigmoid(x)


def _dot(a, b):
    return jnp.dot(a, b, preferred_element_type=F32)


def _dot_nt(a, b, precision=None):
    return lax.dot_general(a, b, (((1,), (1,)), ((), ())), preferred_element_type=F32, precision=precision)


def _mod_kernel(c_ref, w_ref, b_ref, o_ref):
    a = _silu(c_ref[...]).astype(BF16)
    o_ref[...] = _dot(a, w_ref[...].astype(BF16)) + b_ref[...]


def _mod_call(cvec, w_mod, b_mod, tn=1024):
    rows, d = cvec.shape
    n = w_mod.shape[1]
    return pl.pallas_call(
        _mod_kernel,
        grid=(n // tn,),
        in_specs=[
            pl.BlockSpec((rows, d), lambda j: (0, 0)),
            pl.BlockSpec((d, tn), lambda j: (0, j)),
            pl.BlockSpec((1, tn), lambda j: (0, j)),
        ],
        out_specs=pl.BlockSpec((rows, tn), lambda j: (0, j)),
        out_shape=jax.ShapeDtypeStruct((rows, n), F32),
        compiler_params=pltpu.CompilerParams(
            dimension_semantics=("arbitrary",), vmem_limit_bytes=_vmem_limit(40 << 20)),
        name="mod",
    )(cvec, w_mod, b_mod)


def _inproj_kernel(x_ref, g_ref, sh_ref, sc_ref, w_ref, o_ref, h_ref):
    @pl.when(pl.program_id(1) == 0)
    def _():
        h = _rms(x_ref[...]) * g_ref[...]
        h = h * (1.0 + sc_ref[0]) + sh_ref[0]
        h_ref[...] = h.astype(BF16)

    o_ref[...] = _dot(h_ref[...], w_ref[...])


def _inproj_call(x2, g, mod3, w_bf, row_of_tile, tm, tn, name, first_col=0):
    t, d = x2.shape
    n = w_bf.shape[1] - first_col
    col0 = first_col // tn
    assert first_col % tn == 0
    return pl.pallas_call(
        _inproj_kernel,
        grid=(t // tm, n // tn),
        in_specs=[
            pl.BlockSpec((tm, d), lambda i, j: (i, 0)),
            pl.BlockSpec((1, d), lambda i, j: (0, 0)),
            pl.BlockSpec((1, 1, d), lambda i, j: (row_of_tile(i), 0, 0)),
            pl.BlockSpec((1, 1, d), lambda i, j: (row_of_tile(i), 0, 1)),
            pl.BlockSpec((d, tn), lambda i, j: (0, col0 + j)),
        ],
        out_specs=pl.BlockSpec((tm, tn), lambda i, j: (i, j)),
        out_shape=jax.ShapeDtypeStruct((t, n), F32),
        scratch_shapes=[pltpu.VMEM((tm, d), BF16)],
        compiler_params=pltpu.CompilerParams(
            dimension_semantics=("arbitrary", "arbitrary"), vmem_limit_bytes=_vmem_limit(52 << 20)),
        name=name,
    )(x2, g, mod3, mod3, w_bf)


def _attn_kernel(q_ref, k_ref, v_ref, kc_ref, vc_ref, cos_ref, sin_ref, bias_ref, g_ref, o_ref,
                 qs_ref, ks_ref, vs_ref, kcs_ref, vcs_ref, *, block_type):
    cos = cos_ref[...]
    sin = sin_ref[...]

    def rope(x):
        return x * cos + pltpu.roll(x, HEAD_DIM // 2, 1) * sin

    qs_ref[...] = rope(q_ref[...]).astype(BF16)
    ks_ref[...] = rope(k_ref[...]).astype(BF16)
    vs_ref[...] = v_ref[...].astype(BF16)
    kcs_ref[...] = kc_ref[...].astype(BF16)
    vcs_ref[...] = vc_ref[...].astype(BF16)

    scale = HEAD_DIM ** -0.5
    qb = Q_ROWS * GRID_W
    kb = K_ROWS * GRID_W
    n_blocks = q_ref.shape[0] // qb
    max_start = q_ref.shape[0] // GRID_W - K_ROWS

    def block(j):
        q0 = pl.multiple_of(j * qb, qb)
        k_row = jnp.clip(j * Q_ROWS - NA_ROWS // 2, 0, max_start)
        k0 = pl.multiple_of(k_row * GRID_W, qb)
        q = qs_ref[pl.ds(q0, qb), :]
        btype = 0
        for jj, tt in enumerate(block_type):
            btype = jnp.where(j == jj, tt, btype)
        s_loc = _dot_nt(q, ks_ref[pl.ds(k0, kb), :]) * scale + bias_ref[0, btype]
        s_ctx = _dot_nt(q, kcs_ref[...]) * scale
        m = jnp.maximum(jnp.max(s_loc, axis=-1, keepdims=True), jnp.max(s_ctx, axis=-1, keepdims=True))
        p_loc = jnp.exp(s_loc - m)
        p_ctx = jnp.exp(s_ctx - m)
        denom = jnp.sum(p_loc, axis=-1, keepdims=True) + jnp.sum(p_ctx, axis=-1, keepdims=True)
        o = _dot(p_loc.astype(BF16), vs_ref[pl.ds(k0, kb), :]) + _dot(p_ctx.astype(BF16), vcs_ref[...])
        o = o / denom
        o_ref[pl.ds(q0, qb), :] = (_rms(o) * g_ref[...]).astype(BF16)

    def body(p, carry):
        for u in range(ATTN_BLOCKS_PER_ITER):
            block(p * ATTN_BLOCKS_PER_ITER + u)
        return carry

    lax.fori_loop(0, n_blocks // ATTN_BLOCKS_PER_ITER, body, 0)


def _attn_call(proj, proj_ctx, cos_t, sin_t, bias, g_attn, batch, seq, ctx_len):
    d_attn = N_HEADS * HEAD_DIM
    q_blk = (proj.shape[1] - 3 * d_attn) // HEAD_DIM
    n_types = bias.shape[1]
    _, block_type = _block_row_offsets(seq // GRID_W)
    qb, kb = Q_ROWS * GRID_W, K_ROWS * GRID_W
    return pl.pallas_call(
        functools.partial(_attn_kernel, block_type=block_type),
        grid=(N_HEADS, batch),
        in_specs=[
            pl.BlockSpec((seq, HEAD_DIM), lambda h, b: (b, q_blk + h)),
            pl.BlockSpec((seq, HEAD_DIM), lambda h, b: (b, q_blk + N_HEADS + h)),
            pl.BlockSpec((seq, HEAD_DIM), lambda h, b: (b, q_blk + 2 * N_HEADS + h)),
            pl.BlockSpec((ctx_len, HEAD_DIM), lambda h, b: (b, h)),
            pl.BlockSpec((ctx_len, HEAD_DIM), lambda h, b: (b, N_HEADS + h)),
            pl.BlockSpec((seq, HEAD_DIM), lambda h, b: (0, 0)),
            pl.BlockSpec((seq, HEAD_DIM), lambda h, b: (0, 0)),
            pl.BlockSpec((1, n_types, qb, kb), lambda h, b: (h, 0, 0, 0)),
            pl.BlockSpec((1, HEAD_DIM), lambda h, b: (0, h)),
        ],
        out_specs=pl.BlockSpec((seq, HEAD_DIM), lambda h, b: (b, h)),
        out_shape=jax.ShapeDtypeStruct((batch * seq, d_attn), BF16),
        scratch_shapes=[
            pltpu.VMEM((seq, HEAD_DIM), BF16),
            pltpu.VMEM((seq, HEAD_DIM), BF16),
            pltpu.VMEM((seq, HEAD_DIM), BF16),
            pltpu.VMEM((ctx_len, HEAD_DIM), BF16),
            pltpu.VMEM((ctx_len, HEAD_DIM), BF16),
        ],
        compiler_params=pltpu.CompilerParams(
            dimension_semantics=("arbitrary", "arbitrary"), vmem_limit_bytes=_vmem_limit(48 << 20)),
        name="attn",
    )(proj, proj, proj, proj_ctx, proj_ctx, cos_t, sin_t, bias, g_attn)


def _rope_tables(seq):
    t = jnp.arange(seq, dtype=I32)
    pos = jnp.stack([t // GRID_W, t % GRID_W], axis=-1).astype(F32)
    n_freq = HEAD_DIM // 4
    inv_freq = ROPE_THETA ** (-jnp.arange(n_freq, dtype=F32) / n_freq)
    ang = pos[:, :, None] * inv_freq
    cos, sin = jnp.cos(ang).reshape(seq, 2 * n_freq), jnp.sin(ang).reshape(seq, 2 * n_freq)
    cos_t = jnp.concatenate([cos, cos], axis=-1)
    sin_t = jnp.concatenate([-sin, sin], axis=-1)
    return cos_t, sin_t


def _rope_order(w_in, n_lead_cols):
    rows, cols = w_in.shape
    groups = cols // HEAD_DIM
    g0 = n_lead_cols // HEAD_DIM
    w5 = w_in.reshape(rows, groups, 2, 2, HEAD_DIM // 4)
    is_qk = (np.arange(groups) >= g0) & (np.arange(groups) < g0 + 2 * N_HEADS)
    w5 = jnp.where(is_qk[None, :, None, None, None], w5.transpose(0, 1, 3, 2, 4), w5)
    return w5.reshape(rows, cols)


def _block_row_offsets(rows):
    n_blocks = rows // Q_ROWS
    kh = min(NA_ROWS, rows)
    sel = np.zeros((n_blocks, Q_ROWS, K_ROWS), np.int32)
    for j in range(n_blocks):
        k_start = int(np.clip(j * Q_ROWS - NA_ROWS // 2, 0, rows - K_ROWS))
        for a in range(Q_ROWS):
            r = j * Q_ROWS + a
            r0 = int(np.clip(r - kh // 2, 0, rows - kh))
            for b in range(K_ROWS):
                kr = k_start + b
                sel[j, a, b] = kr - r + NA_ROWS - 1 if r0 <= kr < r0 + kh else 2 * NA_ROWS - 1
    uniq, block_type = np.unique(sel.reshape(n_blocks, -1), axis=0, return_inverse=True)
    return uniq.reshape(-1, Q_ROWS, K_ROWS), tuple(int(v) for v in np.ravel(block_type))


def _bias_table(rpb, rows):
    w = GRID_W
    qc = np.arange(w)[:, None]
    kc = np.arange(w)[None, :]
    c0 = np.clip(qc - NA_COLS // 2, 0, w - NA_COLS)
    col_ok = (kc >= c0) & (kc < c0 + NA_COLS)
    col_off = np.clip(kc - qc + NA_COLS - 1, 0, 2 * NA_COLS - 2)
    toep = jnp.where(col_ok[None, None], rpb[:, :, col_off], MASK_VALUE)
    masked = jnp.full((rpb.shape[0], 1, w, w), MASK_VALUE, F32)
    toep = jnp.concatenate([toep, masked], axis=1)
    sel, _ = _block_row_offsets(rows)
    blocks = toep[:, sel]
    blocks = jnp.transpose(blocks, (0, 1, 2, 4, 3, 5))
    return blocks.reshape(rpb.shape[0], sel.shape[0], Q_ROWS * w, K_ROWS * w)


def _conv_kernel(ub_ref, uc_ref, ux_ref, w_ref, b_ref, g_ref, o_ref):
    p = uc_ref[...] * ux_ref[...]
    seq = p.shape[0]
    row = lax.broadcasted_iota(I32, p.shape, 0)
    prev = jnp.where(row == 0, 0.0, pltpu.roll(p, 1, 0))
    nxt = jnp.where(row == seq - 1, 0.0, pltpu.roll(p, seq - 1, 0))
    w = w_ref[...]
    conv = prev * w[0:1] + p * w[1:2] + nxt * w[2:3] + b_ref[...]
    y = ub_ref[...] * conv
    g = g_ref[...]
    for c0 in range(0, y.shape[1], CONV_GROUP_DIM):
        sl = slice(c0, c0 + CONV_GROUP_DIM)
        o_ref[:, sl] = (_rms(y[:, sl]) * g[:, sl]).astype(BF16)


def _conv_call(proj, conv_w, conv_b, g_conv, batch, seq, cb=256):
    d_conv = conv_w.shape[1]
    nc = d_conv // cb
    return pl.pallas_call(
        _conv_kernel,
        grid=(batch, nc),
        in_specs=[
            pl.BlockSpec((seq, cb), lambda b, c: (b, c)),
            pl.BlockSpec((seq, cb), lambda b, c: (b, nc + c)),
            pl.BlockSpec((seq, cb), lambda b, c: (b, 2 * nc + c)),
            pl.BlockSpec((conv_w.shape[0], cb), lambda b, c: (0, c)),
            pl.BlockSpec((1, cb), lambda b, c: (0, c)),
            pl.BlockSpec((1, cb), lambda b, c: (0, c)),
        ],
        out_specs=pl.BlockSpec((seq, cb), lambda b, c: (b, c)),
        out_shape=jax.ShapeDtypeStruct((batch * seq, d_conv), BF16),
        compiler_params=pltpu.CompilerParams(
            dimension_semantics=("arbitrary", "arbitrary"), vmem_limit_bytes=_vmem_limit(48 << 20)),
        name="conv",
    )(proj, proj, proj, conv_w, conv_b, g_conv)


def _outproj_kernel(yc_ref, ya_ref, w_ref, x_ref, gpost_ref, gate_ref, gpre_ref, sh_ref, sc_ref, wr_ref,
                    x1_ref, t_ref, lg_ref):
    dc = yc_ref.shape[1]
    y = _dot(yc_ref[...], w_ref[0:dc, :]) + _dot(ya_ref[...], w_ref[dc:, :])
    x1 = x_ref[...] + gate_ref[0] * (_rms(y) * gpost_ref[...])
    x1_ref[...] = x1
    t = (_rms(x1) * gpre_ref[...]) * (1.0 + sc_ref[0]) + sh_ref[0]
    t_ref[...] = t
    ne = wr_ref.shape[0]
    wr = wr_ref[...]
    wr_hi = wr.astype(BF16)
    wr_lo = (wr - wr_hi.astype(F32)).astype(BF16)
    t_hi = t.astype(BF16)
    t_lo = (t - t_hi.astype(F32)).astype(BF16)
    both = _dot_nt(t_hi, jnp.concatenate([wr_hi, wr_lo], axis=0))
    logits = both[:, :ne] + (_dot_nt(t_lo, wr_hi) + both[:, ne:])
    lg_ref[...] = logits.T


def _outproj_call(yc, ya, w_out_bf, x2, g_post, g_pre, mod3, w_router_t, seq, tm=512):
    t, d = x2.shape
    dc = yc.shape[1]
    ne = w_router_t.shape[0]
    tiles_per_seq = seq // tm
    row = lambda i: i // tiles_per_seq
    return pl.pallas_call(
        _outproj_kernel,
        grid=(t // tm,),
        in_specs=[
            pl.BlockSpec((tm, dc), lambda i: (i, 0)),
            pl.BlockSpec((tm, ya.shape[1]), lambda i: (i, 0)),
            pl.BlockSpec(w_out_bf.shape, lambda i: (0, 0), pipeline_mode=pl.Buffered(1)),
            pl.BlockSpec((tm, d), lambda i: (i, 0)),
            pl.BlockSpec((1, d), lambda i: (0, 0)),
            pl.BlockSpec((1, 1, d), lambda i: (row(i), 0, 2)),
            pl.BlockSpec((1, d), lambda i: (0, 0)),
            pl.BlockSpec((1, 1, d), lambda i: (row(i), 0, 3)),
            pl.BlockSpec((1, 1, d), lambda i: (row(i), 0, 4)),
            pl.BlockSpec((ne, d), lambda i: (0, 0)),
        ],
        out_specs=[
            pl.BlockSpec((tm, d), lambda i: (i, 0)),
            pl.BlockSpec((tm, d), lambda i: (i, 0)),
            pl.BlockSpec((ne, tm), lambda i: (0, i)),
        ],
        out_shape=[
            jax.ShapeDtypeStruct((t, d), F32),
            jax.ShapeDtypeStruct((t, d), F32),
            jax.ShapeDtypeStruct((ne, t), F32),
        ],
        compiler_params=pltpu.CompilerParams(
            dimension_semantics=("arbitrary",), vmem_limit_bytes=_vmem_limit(57 << 20)),
        name="out_proj",
    )(yc, ya, w_out_bf, x2, g_post, mod3, g_pre, mod3, mod3, w_router_t)


def _first_argmax_mask(v, axis):
    m = jnp.max(v, axis=axis, keepdims=True)
    idx = lax.broadcasted_iota(I32, v.shape, axis)
    big = v.shape[axis]
    first = jnp.min(jnp.where(v == m, idx, big), axis=axis, keepdims=True)
    return idx == first, m


def _router_kernel(lg_ref, rb_ref, ek_ref, rk_ref, gk_ref, cnt_ref, carry_ref):
    ne, tm = lg_ref.shape
    per = ne // N_EXPERT_GROUPS

    @pl.when(pl.program_id(0) == 0)
    def _():
        carry_ref[...] = jnp.zeros_like(carry_ref)

    scores = jax.nn.sigmoid(lg_ref[...])
    sel = scores + rb_ref[...]
    s3 = sel.reshape(N_EXPERT_GROUPS, per, tm)
    hit1, m1 = _first_argmax_mask(s3, 1)
    m2 = jnp.max(jnp.where(hit1, -jnp.inf, s3), axis=1, keepdims=True)
    grp = (m1 + m2).reshape(N_EXPERT_GROUPS, tm)
    gmask = jnp.zeros(grp.shape, jnp.bool_)
    for _ in range(TOPK_GROUPS):
        hit, _ = _first_argmax_mask(jnp.where(gmask, -jnp.inf, grp), 0)
        gmask = gmask | hit
    emask = jnp.broadcast_to(gmask.reshape(N_EXPERT_GROUPS, 1, tm), s3.shape).reshape(ne, tm)
    cand = jnp.where(emask, sel, -jnp.inf)
    chosen = jnp.zeros(cand.shape, jnp.bool_)
    hits = []
    for _ in range(TOP_K):
        hit, _ = _first_argmax_mask(jnp.where(chosen, -jnp.inf, cand), 0)
        hit = hit & ~chosen
        hits.append(hit)
        chosen = chosen | hit
    wsel = jnp.where(chosen, scores, 0.0)
    gates = wsel / jnp.sum(wsel, axis=0, keepdims=True) * ROUTED_SCALE

    r_i = lax.broadcasted_iota(I32, (tm, tm), 0)
    c_i = lax.broadcasted_iota(I32, (tm, tm), 1)
    tri = jnp.where(r_i <= c_i, 1.0, 0.0).astype(BF16)
    csum = _dot(jnp.where(chosen, 1.0, 0.0).astype(BF16), tri)
    carry = carry_ref[:, 0:1]
    rank = carry + csum - 1.0
    eidx = lax.broadcasted_iota(I32, (ne, tm), 0).astype(F32)

    def pick(hit, v):
        return jnp.sum(jnp.where(hit, v, 0.0), axis=0, keepdims=True)

    ek_ref[...] = jnp.concatenate([pick(h, eidx) for h in hits], axis=0).astype(I32)
    rk_ref[...] = jnp.concatenate([pick(h, rank) for h in hits], axis=0).astype(I32)
    gk_rows = jnp.concatenate([pick(h, gates) for h in hits] + [jnp.zeros((LANES - TOP_K, tm), F32)], axis=0)
    gk_ref[...] = gk_rows.T
    new_carry = carry + csum[:, tm - 1:tm]
    carry_ref[...] = jnp.broadcast_to(new_carry, carry_ref.shape)
    cnt_ref[...] = jnp.broadcast_to(new_carry, cnt_ref.shape).astype(I32)


def _router_call(logits_t, router_bias_col, tm=512):
    ne, t = logits_t.shape
    return pl.pallas_call(
        _router_kernel,
        grid=(t // tm,),
        in_specs=[
            pl.BlockSpec((ne, tm), lambda i: (0, i)),
            pl.BlockSpec((ne, 1), lambda i: (0, 0)),
        ],
        out_specs=[
            pl.BlockSpec((TOP_K, tm), lambda i: (0, i)),
            pl.BlockSpec((TOP_K, tm), lambda i: (0, i)),
            pl.BlockSpec((tm, LANES), lambda i: (i, 0)),
            pl.BlockSpec((ne, LANES), lambda i: (0, 0)),
        ],
        out_shape=[
            jax.ShapeDtypeStruct((TOP_K, t), I32),
            jax.ShapeDtypeStruct((TOP_K, t), I32),
            jax.ShapeDtypeStruct((t, LANES), F32),
            jax.ShapeDtypeStruct((ne, LANES), I32),
        ],
        scratch_shapes=[pltpu.VMEM((ne, LANES), F32)],
        compiler_params=pltpu.CompilerParams(dimension_semantics=("arbitrary",)),
        name="router",
    )(logits_t, router_bias_col)


def _routing_tables(ek, rk, counts, n_tiles_max, tokens_per_step):
    tg = EXPERT_ROW_TILE
    k, t = ek.shape
    n_exp = counts.shape[0]
    padded = (counts + tg - 1) // tg * tg
    before = np.tril(np.ones((n_exp, n_exp), bool), -1)
    offsets = jnp.sum(jnp.where(before, padded[None, :], 0), axis=1)
    compact_offsets = jnp.sum(jnp.where(before, counts[None, :], 0), axis=1)
    ends = offsets + padded
    dest = rk
    for e in range(n_exp):
        dest = dest + jnp.where(ek == e, offsets[e], 0)
    tok = jnp.arange(t, dtype=I32)
    tok_sorted = lax.rem(jnp.sort((ek * t + tok).reshape(-1)), t)
    tok_sorted = jnp.concatenate([tok_sorted, jnp.zeros((IDX_WINDOW,), I32)])
    tile_ends = ends // tg
    n_valid = jnp.sum(padded) // tg
    tile_ids = jnp.arange(n_tiles_max, dtype=I32)
    tile_expert = jnp.sum(tile_ids[:, None] >= tile_ends[None, :], axis=1).astype(I32)
    last_expert = jnp.sum(n_valid - 1 >= tile_ends).astype(I32)
    tile_expert = jnp.where(tile_ids < n_valid, tile_expert, last_expert)
    own = tile_expert[:, None] == jnp.arange(n_exp, dtype=I32)[None, :]
    tile_start = tile_ids * tg + jnp.sum(jnp.where(own, (compact_offsets - offsets)[None, :], 0), axis=1)
    tile_start = jnp.where(tile_ids < n_valid, tile_start, 0).astype(I32)
    dest_steps = dest.reshape(k, t // tokens_per_step, tokens_per_step).transpose(1, 0, 2).reshape(-1)
    return dest_steps, tok_sorted, tile_expert, n_valid.reshape(1).astype(I32), tile_start


def _experts_kernel(te_ref, nv_ref, ts_ref, tok_hbm, t_hbm, wg_ref, wu_ref, wd_ref, y_ref, *scratch):
    idx_bufs = scratch[:2]
    x_bufs = scratch[2:2 + ROW_BUFFERS]
    wg_s, wu_s, wd_s, sem_idx, sem_row = scratch[2 + ROW_BUFFERS:]
    i = pl.program_id(0)
    nv = nv_ref[0]
    tg = y_ref.shape[0]
    ahead = ROW_LOOKAHEAD

    def tile_of(j):
        return jnp.minimum(j, nv - 1)

    def idx_copy(j, j_static):
        window = pl.multiple_of(lax.shift_left(lax.shift_right_logical(ts_ref[tile_of(j)], 10), 10), IDX_ALIGN)
        slot = j_static % 2
        return pltpu.make_async_copy(tok_hbm.at[pl.ds(window, IDX_WINDOW)], idx_bufs[slot], sem_idx.at[slot])

    def row_copy(j, j_static, r):
        base = ts_ref[tile_of(j)] & (IDX_ALIGN - 1)
        slot = j_static % ROW_BUFFERS
        return pltpu.make_async_copy(t_hbm.at[idx_bufs[j_static % 2][base + r]], x_bufs[slot].at[r],
                                     sem_row.at[slot])

    def wait_rows(slot):
        pltpu.make_async_copy(t_hbm.at[pl.ds(0, tg)], x_bufs[slot], sem_row.at[slot]).wait()

    def step(phase):
        if phase == 0:
            @pl.when(i == 0)
            def _():
                for j in range(ahead):
                    idx_copy(j, j).start()
                    idx_copy(j, j).wait()

                    def first_rows(r, c, j=j):
                        row_copy(j, j, r).start()
                        return c

                    lax.fori_loop(0, tg, first_rows, 0)
                idx_copy(ahead, ahead).start()

        idx_copy(i + ahead, phase + ahead).wait()
        idx_copy(i + ahead + 1, phase + ahead + 1).start()

        @pl.when((i == 0) | (te_ref[i] != te_ref[jnp.maximum(i - 1, 0)]))
        def _():
            wg_s[...] = wg_ref[0].astype(BF16)
            wu_s[...] = wu_ref[0].astype(BF16)
            wd_s[...] = wd_ref[0].astype(BF16)

        wait_rows(phase)
        x = x_bufs[phase][...].astype(BF16)
        hid = _silu(_dot(x, wg_s[...])) * _dot(x, wu_s[...])
        y_ref[...] = _dot(hid.astype(BF16), wd_s[...])
        for r in range(tg):
            row_copy(i + ahead, phase + ahead, r).start()

        @pl.when(i == nv - 1)
        def _():
            for other in range(1, ROW_BUFFERS):
                wait_rows((phase + other) % ROW_BUFFERS)
            idx_copy(i + ahead + 1, phase + ahead + 1).wait()

    for phase in range(ROW_BUFFERS):
        @pl.when((i < nv) & ((i & (ROW_BUFFERS - 1)) == phase))
        def _(phase=phase):
            step(phase)

    @pl.when(i >= nv)
    def _():
        y_ref[...] = jnp.zeros_like(y_ref)


def _experts_call(tile_expert, n_valid, tile_start, tok_sorted, t_f32, w_gate, w_up, w_down):
    _, d = t_f32.shape
    ne, _, df = w_gate.shape
    tg = EXPERT_ROW_TILE
    n_tiles = tile_expert.shape[0]
    grid_spec = pltpu.PrefetchScalarGridSpec(
        num_scalar_prefetch=3,
        grid=(n_tiles,),
        in_specs=[
            pl.BlockSpec(memory_space=pl.ANY),
            pl.BlockSpec(memory_space=pl.ANY),
            pl.BlockSpec((1, d, df), lambda i, te, nv, ts: (te[i], 0, 0)),
            pl.BlockSpec((1, d, df), lambda i, te, nv, ts: (te[i], 0, 0)),
            pl.BlockSpec((1, df, d), lambda i, te, nv, ts: (te[i], 0, 0)),
        ],
        out_specs=pl.BlockSpec((tg, d), lambda i, te, nv, ts: (i, 0)),
        scratch_shapes=(
            [pltpu.SMEM((IDX_WINDOW,), I32)] * 2
            + [pltpu.VMEM((tg, d), F32)] * ROW_BUFFERS
            + [pltpu.VMEM((d, df), BF16), pltpu.VMEM((d, df), BF16), pltpu.VMEM((df, d), BF16),
               pltpu.SemaphoreType.DMA((2,)), pltpu.SemaphoreType.DMA((ROW_BUFFERS,))]),
    )
    return pl.pallas_call(
        _experts_kernel,
        grid_spec=grid_spec,
        out_shape=jax.ShapeDtypeStruct((n_tiles * tg, d), F32),
        compiler_params=pltpu.CompilerParams(
            dimension_semantics=("arbitrary",), vmem_limit_bytes=_vmem_limit(57 << 20)),
        name="experts",
    )(tile_expert, n_valid, tile_start, tok_sorted, t_f32, w_gate, w_up, w_down)


def _combine_kernel(dest_hbm, y_hbm, gk_ref, t_ref, wsg_ref, wsu_ref, wsd_ref, x1_ref, gpost_ref, gate_ref, o_ref,
                    idx0, idx1, rows0, rows1, sem_idx, sem_row, *, tc, n_steps):
    i = pl.program_id(0)
    n_idx = TOP_K * tc
    idx_bufs, row_bufs = (idx0, idx1), (rows0, rows1)

    def idx_copy(j, j_static):
        slot = j_static % 2
        start = pl.multiple_of(jnp.minimum(j, n_steps - 1) * n_idx, n_idx)
        return pltpu.make_async_copy(dest_hbm.at[pl.ds(start, n_idx)], idx_bufs[slot], sem_idx.at[slot])

    def row_copy(j_static, k, t):
        slot = j_static % 2
        return pltpu.make_async_copy(y_hbm.at[idx_bufs[slot][k * tc + t]], row_bufs[slot].at[k, t], sem_row.at[slot])

    def wait_rows(slot):
        for k in range(TOP_K):
            pltpu.make_async_copy(y_hbm.at[pl.ds(0, tc)], row_bufs[slot].at[k], sem_row.at[slot]).wait()

    def step(phase):
        if phase == 0:
            @pl.when(i == 0)
            def _():
                idx_copy(0, 0).start()
                idx_copy(0, 0).wait()

                def first_rows(t, c):
                    for k in range(TOP_K):
                        row_copy(0, k, t).start()
                    return c

                lax.fori_loop(0, tc, first_rows, 0)
                idx_copy(1, 1).start()

        idx_copy(i + 1, phase + 1).wait()
        idx_copy(i + 2, phase + 2).start()
        wait_rows(phase)
        for t in range(tc):
            for k in range(TOP_K):
                row_copy(phase + 1, k, t).start()

        rows_ref = row_bufs[phase]
        t = t_ref[...].astype(BF16)
        hid = _silu(_dot(t, wsg_ref[...])) * _dot(t, wsu_ref[...])
        f = _dot(hid.astype(BF16), wsd_ref[...])
        g = gk_ref[...]
        for k in range(TOP_K):
            f = f + g[:, k:k + 1] * rows_ref[k]
        o_ref[...] = x1_ref[...] + gate_ref[0] * (_rms(f) * gpost_ref[...])

        @pl.when(i == n_steps - 1)
        def _():
            wait_rows(1 - phase)
            idx_copy(i + 2, phase + 2).wait()

    for phase in range(2):
        @pl.when((i & 1) == phase)
        def _(phase=phase):
            step(phase)


def _combine_call(dest_steps, y, gk, t_f32, wsg_bf, wsu_bf, wsd_bf, x1, g_post, mod3, seq):
    t, d = x1.shape
    tc = COMBINE_TOKENS
    tiles_per_seq = seq // tc
    return pl.pallas_call(
        functools.partial(_combine_kernel, tc=tc, n_steps=t // tc),
        grid=(t // tc,),
        in_specs=[
            pl.BlockSpec(memory_space=pl.ANY),
            pl.BlockSpec(memory_space=pl.ANY),
            pl.BlockSpec((tc, LANES), lambda i: (i, 0)),
            pl.BlockSpec((tc, d), lambda i: (i, 0)),
            pl.BlockSpec(wsg_bf.shape, lambda i: (0, 0)),
            pl.BlockSpec(wsu_bf.shape, lambda i: (0, 0)),
            pl.BlockSpec(wsd_bf.shape, lambda i: (0, 0)),
            pl.BlockSpec((tc, d), lambda i: (i, 0)),
            pl.BlockSpec((1, d), lambda i: (0, 0)),
            pl.BlockSpec((1, 1, d), lambda i: (i // tiles_per_seq, 0, 5)),
        ],
        out_specs=pl.BlockSpec((tc, d), lambda i: (i, 0)),
        out_shape=jax.ShapeDtypeStruct((t, d), F32),
        scratch_shapes=[
            pltpu.SMEM((TOP_K * tc,), I32),
            pltpu.SMEM((TOP_K * tc,), I32),
            pltpu.VMEM((TOP_K, tc, d), F32),
            pltpu.VMEM((TOP_K, tc, d), F32),
            pltpu.SemaphoreType.DMA((2,)),
            pltpu.SemaphoreType.DMA((2,)),
        ],
        compiler_params=pltpu.CompilerParams(
            dimension_semantics=("arbitrary",), vmem_limit_bytes=_vmem_limit(48 << 20)),
        name="combine",
    )(dest_steps, y, gk, t_f32, wsg_bf, wsu_bf, wsd_bf, x1, g_post, mod3)


def kernel(x, c, ctx, c_ctx, w_mod, b_mod, g_pre_mix, g_post_mix, g_pre_ffn, g_post_ffn, w_in, conv_w, conv_b,
           rpb, g_conv_out, g_attn_out, w_out, w_router, router_bias, w_exp_gate, w_exp_up, w_exp_down,
           w_sh_gate, w_sh_up, w_sh_down):
    batch, seq, d = x.shape
    ctx_len = ctx.shape[1]
    n_tok = batch * seq
    assert w_mod.shape[0] == 1, "single-layer kernel"
    assert seq % (Q_ROWS * GRID_W) == 0 and seq // GRID_W >= K_ROWS
    assert w_router.shape[2] == N_EXPERTS and seq % COMBINE_TOKENS == 0
    assert EXPERT_ROW_TILE <= IDX_ALIGN and n_tok * TOP_K % EXPERT_ROW_TILE == 0
    d_attn = N_HEADS * HEAD_DIM
    kv_offset = w_in.shape[2] - 2 * d_attn
    row2 = lambda a: a.reshape(1, -1)

    x2 = x.reshape(n_tok, d)
    ctx2 = ctx.reshape(batch * ctx_len, d)

    n_rows = -(-(batch + 1) // 8) * 8
    cvec = jnp.concatenate([c, c_ctx[None], jnp.zeros((n_rows - batch - 1, d), F32)], axis=0)
    mod = _mod_call(cvec, w_mod[0], row2(b_mod[0]))
    mod3 = mod.reshape(n_rows, 1, 6 * d)

    w_in_bf = _rope_order(w_in[0].astype(BF16), w_in.shape[2] - 3 * d_attn)
    tm_in = 1024
    tiles_per_seq = seq // tm_in
    proj = _inproj_call(x2, row2(g_pre_mix[0]), mod3, w_in_bf, lambda i: i // tiles_per_seq,
                        tm_in, 1024, "in_proj")
    proj_ctx = _inproj_call(ctx2, row2(g_pre_mix[0]), mod3, w_in_bf, lambda i: batch,
                            min(tm_in, ctx2.shape[0]), 1024, "in_proj_ctx", first_col=kv_offset)

    cos_t, sin_t = _rope_tables(seq)
    bias = _bias_table(rpb[0], seq // GRID_W)
    ya = _attn_call(proj, proj_ctx, cos_t, sin_t, bias, row2(g_attn_out[0]), batch, seq, ctx_len)
    yc = _conv_call(proj, conv_w[0], row2(conv_b[0]), row2(g_conv_out[0]), batch, seq)

    x1, t_f32, logits_t = _outproj_call(yc, ya, w_out[0].astype(BF16), x2, row2(g_post_mix[0]),
                                        row2(g_pre_ffn[0]), mod3, w_router[0].T, seq)
    ek, rk, gk, cnt = _router_call(logits_t, router_bias[0].reshape(-1, 1))

    n_tiles_max = n_tok * TOP_K // EXPERT_ROW_TILE + N_EXPERTS
    dest_c, tok_sorted, tile_expert, n_valid, tile_start = _routing_tables(
        ek, rk, cnt[:, 0], n_tiles_max, COMBINE_TOKENS)
    y = _experts_call(tile_expert, n_valid, tile_start, tok_sorted, t_f32,
                      w_exp_gate[0], w_exp_up[0], w_exp_down[0])
    out = _combine_call(dest_c, y, gk, t_f32, w_sh_gate[0].astype(BF16), w_sh_up[0].astype(BF16),
                        w_sh_down[0].astype(BF16), x1, row2(g_post_ffn[0]), mod3, seq)
    return out.reshape(batch, seq, d)
```

```python
import functools

import numpy as np
import jax
import jax.numpy as jnp
from jax import lax
from jax.experimental import pallas as pl
from jax.experimental.pallas import tpu as pltpu

F32 = jnp.float32
BF16 = jnp.bfloat16
I32 = jnp.int32

GRID_W = 64
N_HEADS = 8
HEAD_DIM = 128
NA_ROWS = 8
NA_COLS = 16
ROPE_THETA = 10000.0
CONV_GROUP_DIM = 128
N_EXPERTS = 64
N_EXPERT_GROUPS = 8
TOPK_GROUPS = 4
TOP_K = 8
ROUTED_SCALE = 2.5
EPS = 1e-6
MASK_VALUE = -1e30

Q_ROWS = 4
K_ROWS = 12
ATTN_BLOCKS_PER_ITER = 4

V7X_VMEM_BYTES = 64 * 1024 * 1024
LANES = 128

EXPERT_ROW_TILE = 512
ROW_BUFFERS = 2
ROW_LOOKAHEAD = ROW_BUFFERS - 1
COMBINE_TOKENS = 128
IDX_ALIGN = 1024
IDX_WINDOW = 2 * IDX_ALIGN


def _vmem_limit(nbytes):
    return int(min(nbytes, V7X_VMEM_BYTES - 6 * 1024 * 1024))


def _rms(x):
    return x * lax.rsqrt(jnp.mean(x * x, axis=-1, keepdims=True) + EPS)


def _silu(x):
    return x * jax.nn.sigmoid(x)


def _dot(a, b):
    return jnp.dot(a, b, preferred_element_type=F32)


def _dot_nt(a, b, precision=None):
    return lax.dot_general(a, b, (((1,), (1,)), ((), ())), preferred_element_type=F32, precision=precision)


def _mod_kernel(c_ref, w_ref, b_ref, o_ref):
    a = _silu(c_ref[...]).astype(BF16)
    o_ref[...] = _dot(a, w_ref[...].astype(BF16)) + b_ref[...]


def _mod_call(cvec, w_mod, b_mod, tn=1024):
    rows, d = cvec.shape
    n = w_mod.shape[1]
    return pl.pallas_call(
        _mod_kernel,
        grid=(n // tn,),
        in_specs=[
            pl.BlockSpec((rows, d), lambda j: (0, 0)),
            pl.BlockSpec((d, tn), lambda j: (0, j)),
            pl.BlockSpec((1, tn), lambda j: (0, j)),
        ],
        out_specs=pl.BlockSpec((rows, tn), lambda j: (0, j)),
        out_shape=jax.ShapeDtypeStruct((rows, n), F32),
        compiler_params=pltpu.CompilerParams(
            dimension_semantics=("arbitrary",), vmem_limit_bytes=_vmem_limit(40 << 20)),
        name="mod",
    )(cvec, w_mod, b_mod)


def _inproj_kernel(x_ref, g_ref, sh_ref, sc_ref, w_ref, o_ref, h_ref):
    @pl.when(pl.program_id(1) == 0)
    def _():
        h = _rms(x_ref[...]) * g_ref[...]
        h = h * (1.0 + sc_ref[0]) + sh_ref[0]
        h_ref[...] = h.astype(BF16)

    o_ref[...] = _dot(h_ref[...], w_ref[...])


def _inproj_call(x2, g, mod3, w_bf, row_of_tile, tm, tn, name, first_col=0):
    t, d = x2.shape
    n = w_bf.shape[1] - first_col
    col0 = first_col // tn
    assert first_col % tn == 0
    return pl.pallas_call(
        _inproj_kernel,
        grid=(t // tm, n // tn),
        in_specs=[
            pl.BlockSpec((tm, d), lambda i, j: (i, 0)),
            pl.BlockSpec((1, d), lambda i, j: (0, 0)),
            pl.BlockSpec((1, 1, d), lambda i, j: (row_of_tile(i), 0, 0)),
            pl.BlockSpec((1, 1, d), lambda i, j: (row_of_tile(i), 0, 1)),
            pl.BlockSpec((d, tn), lambda i, j: (0, col0 + j)),
        ],
        out_specs=pl.BlockSpec((tm, tn), lambda i, j: (i, j)),
        out_shape=jax.ShapeDtypeStruct((t, n), F32),
        scratch_shapes=[pltpu.VMEM((tm, d), BF16)],
        compiler_params=pltpu.CompilerParams(
            dimension_semantics=("arbitrary", "arbitrary"), vmem_limit_bytes=_vmem_limit(52 << 20)),
        name=name,
    )(x2, g, mod3, mod3, w_bf)


def _attn_kernel(q_ref, k_ref, v_ref, kc_ref, vc_ref, cos_ref, sin_ref, bias_ref, g_ref, o_ref,
                 qs_ref, ks_ref, vs_ref, kcs_ref, vcs_ref, *, block_type):
    cos = cos_ref[...]
    sin = sin_ref[...]

    def rope(x):
        return x * cos + pltpu.roll(x, HEAD_DIM // 2, 1) * sin

    qs_ref[...] = rope(q_ref[...]).astype(BF16)
    ks_ref[...] = rope(k_ref[...]).astype(BF16)
    vs_ref[...] = v_ref[...].astype(BF16)
    kcs_ref[...] = kc_ref[...].astype(BF16)
    vcs_ref[...] = vc_ref[...].astype(BF16)

    scale = HEAD_DIM ** -0.5
    qb = Q_ROWS * GRID_W
    kb = K_ROWS * GRID_W
    n_blocks = q_ref.shape[0] // qb
    max_start = q_ref.shape[0] // GRID_W - K_ROWS

    def block(j):
        q0 = pl.multiple_of(j * qb, qb)
        k_row = jnp.clip(j * Q_ROWS - NA_ROWS // 2, 0, max_start)
        k0 = pl.multiple_of(k_row * GRID_W, qb)
        q = qs_ref[pl.ds(q0, qb), :]
        btype = 0
        for jj, tt in enumerate(block_type):
            btype = jnp.where(j == jj, tt, btype)
        s_loc = _dot_nt(q, ks_ref[pl.ds(k0, kb), :]) * scale + bias_ref[0, btype]
        s_ctx = _dot_nt(q, kcs_ref[...]) * scale
        m = jnp.maximum(jnp.max(s_loc, axis=-1, keepdims=True), jnp.max(s_ctx, axis=-1, keepdims=True))
        p_loc = jnp.exp(s_loc - m)
        p_ctx = jnp.exp(s_ctx - m)
        denom = jnp.sum(p_loc, axis=-1, keepdims=True) + jnp.sum(p_ctx, axis=-1, keepdims=True)
        o = _dot(p_loc.astype(BF16), vs_ref[pl.ds(k0, kb), :]) + _dot(p_ctx.astype(BF16), vcs_ref[...])
        o = o / denom
        o_ref[pl.ds(q0, qb), :] = (_rms(o) * g_ref[...]).astype(BF16)

    def body(p, carry):
        for u in range(ATTN_BLOCKS_PER_ITER):
            block(p * ATTN_BLOCKS_PER_ITER + u)
        return carry

    lax.fori_loop(0, n_blocks // ATTN_BLOCKS_PER_ITER, body, 0)


def _attn_call(proj, proj_ctx, cos_t, sin_t, bias, g_attn, batch, seq, ctx_len):
    d_attn = N_HEADS * HEAD_DIM
    q_blk = (proj.shape[1] - 3 * d_attn) // HEAD_DIM
    n_types = bias.shape[1]
    _, block_type = _block_row_offsets(seq // GRID_W)
    qb, kb = Q_ROWS * GRID_W, K_ROWS * GRID_W
    return pl.pallas_call(
        functools.partial(_attn_kernel, block_type=block_type),
        grid=(N_HEADS, batch),
        in_specs=[
            pl.BlockSpec((seq, HEAD_DIM), lambda h, b: (b, q_blk + h)),
            pl.BlockSpec((seq, HEAD_DIM), lambda h, b: (b, q_blk + N_HEADS + h)),
            pl.BlockSpec((seq, HEAD_DIM), lambda h, b: (b, q_blk + 2 * N_HEADS + h)),
            pl.BlockSpec((ctx_len, HEAD_DIM), lambda h, b: (b, h)),
            pl.BlockSpec((ctx_len, HEAD_DIM), lambda h, b: (b, N_HEADS + h)),
            pl.BlockSpec((seq, HEAD_DIM), lambda h, b: (0, 0)),
            pl.BlockSpec((seq, HEAD_DIM), lambda h, b: (0, 0)),
            pl.BlockSpec((1, n_types, qb, kb), lambda h, b: (h, 0, 0, 0)),
            pl.BlockSpec((1, HEAD_DIM), lambda h, b: (0, h)),
        ],
        out_specs=pl.BlockSpec((seq, HEAD_DIM), lambda h, b: (b, h)),
        out_shape=jax.ShapeDtypeStruct((batch * seq, d_attn), BF16),
        scratch_shapes=[
            pltpu.VMEM((seq, HEAD_DIM), BF16),
            pltpu.VMEM((seq, HEAD_DIM), BF16),
            pltpu.VMEM((seq, HEAD_DIM), BF16),
            pltpu.VMEM((ctx_len, HEAD_DIM), BF16),
            pltpu.VMEM((ctx_len, HEAD_DIM), BF16),
        ],
        compiler_params=pltpu.CompilerParams(
            dimension_semantics=("arbitrary", "arbitrary"), vmem_limit_bytes=_vmem_limit(48 << 20)),
        name="attn",
    )(proj, proj, proj, proj_ctx, proj_ctx, cos_t, sin_t, bias, g_attn)


def _rope_tables(seq):
    t = jnp.arange(seq, dtype=I32)
    pos = jnp.stack([t // GRID_W, t % GRID_W], axis=-1).astype(F32)
    n_freq = HEAD_DIM // 4
    inv_freq = ROPE_THETA ** (-jnp.arange(n_freq, dtype=F32) / n_freq)
    ang = pos[:, :, None] * inv_freq
    cos, sin = jnp.cos(ang).reshape(seq, 2 * n_freq), jnp.sin(ang).reshape(seq, 2 * n_freq)
    cos_t = jnp.concatenate([cos, cos], axis=-1)
    sin_t = jnp.concatenate([-sin, sin], axis=-1)
    return cos_t, sin_t


def _win_prep_kernel(w_ref, o_ref, *, first_qk_block, n_qk_blocks):
    j = pl.program_id(0)
    is_qk = (j >= first_qk_block) & (j < first_qk_block + n_qk_blocks)

    @pl.when(is_qk)
    def _():
        q = HEAD_DIM // 4
        for c0 in range(0, w_ref.shape[1], HEAD_DIM):
            x = w_ref[:, c0:c0 + HEAD_DIM]
            x = jnp.concatenate([x[:, 0:q], x[:, 2 * q:3 * q], x[:, q:2 * q], x[:, 3 * q:]], axis=1)
            o_ref[:, c0:c0 + HEAD_DIM] = x.astype(BF16)

    @pl.when(jnp.logical_not(is_qk))
    def _():
        o_ref[...] = w_ref[...].astype(BF16)


def _win_prep_call(w_in, n_lead_cols, tn=1024):
    rows, cols = w_in.shape
    qk_cols = 2 * N_HEADS * HEAD_DIM
    assert n_lead_cols % tn == 0 and qk_cols % tn == 0
    return pl.pallas_call(
        functools.partial(_win_prep_kernel, first_qk_block=n_lead_cols // tn, n_qk_blocks=qk_cols // tn),
        grid=(cols // tn,),
        in_specs=[pl.BlockSpec((rows, tn), lambda j: (0, j))],
        out_specs=pl.BlockSpec((rows, tn), lambda j: (0, j)),
        out_shape=jax.ShapeDtypeStruct((rows, cols), BF16),
        compiler_params=pltpu.CompilerParams(
            dimension_semantics=("arbitrary",), vmem_limit_bytes=_vmem_limit(40 << 20)),
        name="w_in_prep",
    )(w_in)


def _block_row_offsets(rows):
    n_blocks = rows // Q_ROWS
    kh = min(NA_ROWS, rows)
    sel = np.zeros((n_blocks, Q_ROWS, K_ROWS), np.int32)
    for j in range(n_blocks):
        k_start = int(np.clip(j * Q_ROWS - NA_ROWS // 2, 0, rows - K_ROWS))
        for a in range(Q_ROWS):
            r = j * Q_ROWS + a
            r0 = int(np.clip(r - kh // 2, 0, rows - kh))
            for b in range(K_ROWS):
                kr = k_start + b
                sel[j, a, b] = kr - r + NA_ROWS - 1 if r0 <= kr < r0 + kh else 2 * NA_ROWS - 1
    uniq, block_type = np.unique(sel.reshape(n_blocks, -1), axis=0, return_inverse=True)
    return uniq.reshape(-1, Q_ROWS, K_ROWS), tuple(int(v) for v in np.ravel(block_type))


def _bias_table(rpb, rows):
    w = GRID_W
    qc = np.arange(w)[:, None]
    kc = np.arange(w)[None, :]
    c0 = np.clip(qc - NA_COLS // 2, 0, w - NA_COLS)
    col_ok = (kc >= c0) & (kc < c0 + NA_COLS)
    col_off = np.clip(kc - qc + NA_COLS - 1, 0, 2 * NA_COLS - 2)
    toep = jnp.where(col_ok[None, None], rpb[:, :, col_off], MASK_VALUE)
    masked = jnp.full((rpb.shape[0], 1, w, w), MASK_VALUE, F32)
    toep = jnp.concatenate([toep, masked], axis=1)
    sel, _ = _block_row_offsets(rows)
    blocks = toep[:, sel]
    blocks = jnp.transpose(blocks, (0, 1, 2, 4, 3, 5))
    return blocks.reshape(rpb.shape[0], sel.shape[0], Q_ROWS * w, K_ROWS * w)


def _conv_kernel(ub_ref, uc_ref, ux_ref, w_ref, b_ref, g_ref, o_ref):
    p = uc_ref[...] * ux_ref[...]
    seq = p.shape[0]
    row = lax.broadcasted_iota(I32, p.shape, 0)
    prev = jnp.where(row == 0, 0.0, pltpu.roll(p, 1, 0))
    nxt = jnp.where(row == seq - 1, 0.0, pltpu.roll(p, seq - 1, 0))
    w = w_ref[...]
    conv = prev * w[0:1] + p * w[1:2] + nxt * w[2:3] + b_ref[...]
    y = ub_ref[...] * conv
    g = g_ref[...]
    for c0 in range(0, y.shape[1], CONV_GROUP_DIM):
        sl = slice(c0, c0 + CONV_GROUP_DIM)
        o_ref[:, sl] = (_rms(y[:, sl]) * g[:, sl]).astype(BF16)


def _conv_call(proj, conv_w, conv_b, g_conv, batch, seq, cb=256):
    d_conv = conv_w.shape[1]
    nc = d_conv // cb
    return pl.pallas_call(
        _conv_kernel,
        grid=(batch, nc),
        in_specs=[
            pl.BlockSpec((seq, cb), lambda b, c: (b, c)),
            pl.BlockSpec((seq, cb), lambda b, c: (b, nc + c)),
            pl.BlockSpec((seq, cb), lambda b, c: (b, 2 * nc + c)),
            pl.BlockSpec((conv_w.shape[0], cb), lambda b, c: (0, c)),
            pl.BlockSpec((1, cb), lambda b, c: (0, c)),
            pl.BlockSpec((1, cb), lambda b, c: (0, c)),
        ],
        out_specs=pl.BlockSpec((seq, cb), lambda b, c: (b, c)),
        out_shape=jax.ShapeDtypeStruct((batch * seq, d_conv), BF16),
        compiler_params=pltpu.CompilerParams(
            dimension_semantics=("arbitrary", "arbitrary"), vmem_limit_bytes=_vmem_limit(48 << 20)),
        name="conv",
    )(proj, proj, proj, conv_w, conv_b, g_conv)


def _outproj_kernel(yc_ref, ya_ref, w_ref, x_ref, gpost_ref, gate_ref, gpre_ref, sh_ref, sc_ref, wr_ref,
                    x1_ref, t_ref, lg_ref):
    dc = yc_ref.shape[1]
    y = _dot(yc_ref[...], w_ref[0:dc, :]) + _dot(ya_ref[...], w_ref[dc:, :])
    x1 = x_ref[...] + gate_ref[0] * (_rms(y) * gpost_ref[...])
    x1_ref[...] = x1
    t = (_rms(x1) * gpre_ref[...]) * (1.0 + sc_ref[0]) + sh_ref[0]
    t_ref[...] = t
    ne = wr_ref.shape[0]
    wr = wr_ref[...]
    wr_hi = wr.astype(BF16)
    wr_lo = (wr - wr_hi.astype(F32)).astype(BF16)
    t_hi = t.astype(BF16)
    t_lo = (t - t_hi.astype(F32)).astype(BF16)
    both = _dot_nt(t_hi, jnp.concatenate([wr_hi, wr_lo], axis=0))
    logits = both[:, :ne] + (_dot_nt(t_lo, wr_hi) + both[:, ne:])
    lg_ref[...] = logits.T


def _outproj_call(yc, ya, w_out_bf, x2, g_post, g_pre, mod3, w_router_t, seq, tm=512):
    t, d = x2.shape
    dc = yc.shape[1]
    ne = w_router_t.shape[0]
    tiles_per_seq = seq // tm
    row = lambda i: i // tiles_per_seq
    return pl.pallas_call(
        _outproj_kernel,
        grid=(t // tm,),
        in_specs=[
            pl.BlockSpec((tm, dc), lambda i: (i, 0)),
            pl.BlockSpec((tm, ya.shape[1]), lambda i: (i, 0)),
            pl.BlockSpec(w_out_bf.shape, lambda i: (0, 0), pipeline_mode=pl.Buffered(1)),
            pl.BlockSpec((tm, d), lambda i: (i, 0)),
            pl.BlockSpec((1, d), lambda i: (0, 0)),
            pl.BlockSpec((1, 1, d), lambda i: (row(i), 0, 2)),
            pl.BlockSpec((1, d), lambda i: (0, 0)),
            pl.BlockSpec((1, 1, d), lambda i: (row(i), 0, 3)),
            pl.BlockSpec((1, 1, d), lambda i: (row(i), 0, 4)),
            pl.BlockSpec((ne, d), lambda i: (0, 0)),
        ],
        out_specs=[
            pl.BlockSpec((tm, d), lambda i: (i, 0)),
            pl.BlockSpec((tm, d), lambda i: (i, 0)),
            pl.BlockSpec((ne, tm), lambda i: (0, i)),
        ],
        out_shape=[
            jax.ShapeDtypeStruct((t, d), F32),
            jax.ShapeDtypeStruct((t, d), F32),
            jax.ShapeDtypeStruct((ne, t), F32),
        ],
        compiler_params=pltpu.CompilerParams(
            dimension_semantics=("arbitrary",), vmem_limit_bytes=_vmem_limit(57 << 20)),
        name="out_proj",
    )(yc, ya, w_out_bf, x2, g_post, mod3, g_pre, mod3, mod3, w_router_t)


def _first_argmax_mask(v, axis):
    m = jnp.max(v, axis=axis, keepdims=True)
    idx = lax.broadcasted_iota(I32, v.shape, axis)
    big = v.shape[axis]
    first = jnp.min(jnp.where(v == m, idx, big), axis=axis, keepdims=True)
    return idx == first, m


def _router_kernel(lg_ref, rb_ref, ek_ref, rk_ref, gk_ref, cnt_ref, carry_ref):
    ne, tm = lg_ref.shape
    per = ne // N_EXPERT_GROUPS

    @pl.when(pl.program_id(0) == 0)
    def _():
        carry_ref[...] = jnp.zeros_like(carry_ref)

    scores = jax.nn.sigmoid(lg_ref[...])
    sel = scores + rb_ref[...]
    s3 = sel.reshape(N_EXPERT_GROUPS, per, tm)
    hit1, m1 = _first_argmax_mask(s3, 1)
    m2 = jnp.max(jnp.where(hit1, -jnp.inf, s3), axis=1, keepdims=True)
    grp = (m1 + m2).reshape(N_EXPERT_GROUPS, tm)
    gmask = jnp.zeros(grp.shape, jnp.bool_)
    for _ in range(TOPK_GROUPS):
        hit, _ = _first_argmax_mask(jnp.where(gmask, -jnp.inf, grp), 0)
        gmask = gmask | hit
    emask = jnp.broadcast_to(gmask.reshape(N_EXPERT_GROUPS, 1, tm), s3.shape).reshape(ne, tm)
    cand = jnp.where(emask, sel, -jnp.inf)
    chosen = jnp.zeros(cand.shape, jnp.bool_)
    hits = []
    for _ in range(TOP_K):
        hit, _ = _first_argmax_mask(jnp.where(chosen, -jnp.inf, cand), 0)
        hit = hit & ~chosen
        hits.append(hit)
        chosen = chosen | hit
    wsel = jnp.where(chosen, scores, 0.0)
    gates = wsel / jnp.sum(wsel, axis=0, keepdims=True) * ROUTED_SCALE

    r_i = lax.broadcasted_iota(I32, (tm, tm), 0)
    c_i = lax.broadcasted_iota(I32, (tm, tm), 1)
    tri = jnp.where(r_i <= c_i, 1.0, 0.0).astype(BF16)
    csum = _dot(jnp.where(chosen, 1.0, 0.0).astype(BF16), tri)
    carry = carry_ref[:, 0:1]
    rank = carry + csum - 1.0
    eidx = lax.broadcasted_iota(I32, (ne, tm), 0).astype(F32)

    def pick(hit, v):
        return jnp.sum(jnp.where(hit, v, 0.0), axis=0, keepdims=True)

    ek_ref[...] = jnp.concatenate([pick(h, eidx) for h in hits], axis=0).astype(I32)
    rk_ref[...] = jnp.concatenate([pick(h, rank) for h in hits], axis=0).astype(I32)
    gk_rows = jnp.concatenate([pick(h, gates) for h in hits] + [jnp.zeros((LANES - TOP_K, tm), F32)], axis=0)
    gk_ref[...] = gk_rows.T
    new_carry = carry + csum[:, tm - 1:tm]
    carry_ref[...] = jnp.broadcast_to(new_carry, carry_ref.shape)
    cnt_ref[...] = jnp.broadcast_to(new_carry, cnt_ref.shape).astype(I32)


def _router_call(logits_t, router_bias_col, tm=512):
    ne, t = logits_t.shape
    return pl.pallas_call(
        _router_kernel,
        grid=(t // tm,),
        in_specs=[
            pl.BlockSpec((ne, tm), lambda i: (0, i)),
            pl.BlockSpec((ne, 1), lambda i: (0, 0)),
        ],
        out_specs=[
            pl.BlockSpec((TOP_K, tm), lambda i: (0, i)),
            pl.BlockSpec((TOP_K, tm), lambda i: (0, i)),
            pl.BlockSpec((tm, LANES), lambda i: (i, 0)),
            pl.BlockSpec((ne, LANES), lambda i: (0, 0)),
        ],
        out_shape=[
            jax.ShapeDtypeStruct((TOP_K, t), I32),
            jax.ShapeDtypeStruct((TOP_K, t), I32),
            jax.ShapeDtypeStruct((t, LANES), F32),
            jax.ShapeDtypeStruct((ne, LANES), I32),
        ],
        scratch_shapes=[pltpu.VMEM((ne, LANES), F32)],
        compiler_params=pltpu.CompilerParams(dimension_semantics=("arbitrary",)),
        name="router",
    )(logits_t, router_bias_col)


def _routing_tables(ek, rk, counts, n_tiles_max, tokens_per_step):
    tg = EXPERT_ROW_TILE
    k, t = ek.shape
    n_exp = counts.shape[0]
    padded = (counts + tg - 1) // tg * tg
    before = np.tril(np.ones((n_exp, n_exp), bool), -1)
    offsets = jnp.sum(jnp.where(before, padded[None, :], 0), axis=1)
    compact_offsets = jnp.sum(jnp.where(before, counts[None, :], 0), axis=1)
    ends = offsets + padded
    dest = rk
    for e in range(n_exp):
        dest = dest + jnp.where(ek == e, offsets[e], 0)
    tok = jnp.arange(t, dtype=I32)
    tok_sorted = lax.rem(jnp.sort((ek * t + tok).reshape(-1)), t)
    tok_sorted = jnp.concatenate([tok_sorted, jnp.zeros((IDX_WINDOW,), I32)])
    tile_ends = ends // tg
    n_valid = jnp.sum(padded) // tg
    tile_ids = jnp.arange(n_tiles_max, dtype=I32)
    tile_expert = jnp.sum(tile_ids[:, None] >= tile_ends[None, :], axis=1).astype(I32)
    last_expert = jnp.sum(n_valid - 1 >= tile_ends).astype(I32)
    tile_expert = jnp.where(tile_ids < n_valid, tile_expert, last_expert)
    own = tile_expert[:, None] == jnp.arange(n_exp, dtype=I32)[None, :]
    tile_start = tile_ids * tg + jnp.sum(jnp.where(own, (compact_offsets - offsets)[None, :], 0), axis=1)
    tile_start = jnp.where(tile_ids < n_valid, tile_start, 0).astype(I32)
    dest_steps = dest.reshape(k, t // tokens_per_step, tokens_per_step).transpose(1, 0, 2).reshape(-1)
    return dest_steps, tok_sorted, tile_expert, n_valid.reshape(1).astype(I32), tile_start


def _experts_kernel(te_ref, nv_ref, ts_ref, tok_hbm, t_hbm, wg_ref, wu_ref, wd_ref, y_ref, *scratch):
    idx_bufs = scratch[:2]
    x_bufs = scratch[2:2 + ROW_BUFFERS]
    wg_s, wu_s, wd_s, sem_idx, sem_row = scratch[2 + ROW_BUFFERS:]
    i = pl.program_id(0)
    nv = nv_ref[0]
    tg = y_ref.shape[0]
    ahead = ROW_LOOKAHEAD

    def tile_of(j):
        return jnp.minimum(j, nv - 1)

    def idx_copy(j, j_static):
        window = pl.multiple_of(lax.shift_left(lax.shift_right_logical(ts_ref[tile_of(j)], 10), 10), IDX_ALIGN)
        slot = j_static % 2
        return pltpu.make_async_copy(tok_hbm.at[pl.ds(window, IDX_WINDOW)], idx_bufs[slot], sem_idx.at[slot])

    def row_copy(j, j_static, r):
        base = ts_ref[tile_of(j)] & (IDX_ALIGN - 1)
        slot = j_static % ROW_BUFFERS
        return pltpu.make_async_copy(t_hbm.at[idx_bufs[j_static % 2][base + r]], x_bufs[slot].at[r],
                                     sem_row.at[slot])

    def wait_rows(slot):
        pltpu.make_async_copy(t_hbm.at[pl.ds(0, tg)], x_bufs[slot], sem_row.at[slot]).wait()

    def step(phase):
        if phase == 0:
            @pl.when(i == 0)
            def _():
                for j in range(ahead):
                    idx_copy(j, j).start()
                    idx_copy(j, j).wait()

                    def first_rows(r, c, j=j):
                        row_copy(j, j, r).start()
                        return c

                    lax.fori_loop(0, tg, first_rows, 0)
                idx_copy(ahead, ahead).start()

        idx_copy(i + ahead, phase + ahead).wait()
        idx_copy(i + ahead + 1, phase + ahead + 1).start()

        @pl.when((i == 0) | (te_ref[i] != te_ref[jnp.maximum(i - 1, 0)]))
        def _():
            wg_s[...] = wg_ref[0].astype(BF16)
            wu_s[...] = wu_ref[0].astype(BF16)
            wd_s[...] = wd_ref[0].astype(BF16)

        wait_rows(phase)
        x = x_bufs[phase][...].astype(BF16)
        hid = _silu(_dot(x, wg_s[...])) * _dot(x, wu_s[...])
        y_ref[...] = _dot(hid.astype(BF16), wd_s[...])
        for r in range(tg):
            row_copy(i + ahead, phase + ahead, r).start()

        @pl.when(i == nv - 1)
        def _():
            for other in range(1, ROW_BUFFERS):
                wait_rows((phase + other) % ROW_BUFFERS)
            idx_copy(i + ahead + 1, phase + ahead + 1).wait()

    for phase in range(ROW_BUFFERS):
        @pl.when((i < nv) & ((i & (ROW_BUFFERS - 1)) == phase))
        def _(phase=phase):
            step(phase)

    @pl.when(i >= nv)
    def _():
        y_ref[...] = jnp.zeros_like(y_ref)


def _experts_call(tile_expert, n_valid, tile_start, tok_sorted, t_f32, w_gate, w_up, w_down):
    _, d = t_f32.shape
    ne, _, df = w_gate.shape
    tg = EXPERT_ROW_TILE
    n_tiles = tile_expert.shape[0]
    grid_spec = pltpu.PrefetchScalarGridSpec(
        num_scalar_prefetch=3,
        grid=(n_tiles,),
        in_specs=[
            pl.BlockSpec(memory_space=pl.ANY),
            pl.BlockSpec(memory_space=pl.ANY),
            pl.BlockSpec((1, d, df), lambda i, te, nv, ts: (te[i], 0, 0)),
            pl.BlockSpec((1, d, df), lambda i, te, nv, ts: (te[i], 0, 0)),
            pl.BlockSpec((1, df, d), lambda i, te, nv, ts: (te[i], 0, 0)),
        ],
        out_specs=pl.BlockSpec((tg, d), lambda i, te, nv, ts: (i, 0)),
        scratch_shapes=(
            [pltpu.SMEM((IDX_WINDOW,), I32)] * 2
            + [pltpu.VMEM((tg, d), F32)] * ROW_BUFFERS
            + [pltpu.VMEM((d, df), BF16), pltpu.VMEM((d, df), BF16), pltpu.VMEM((df, d), BF16),
               pltpu.SemaphoreType.DMA((2,)), pltpu.SemaphoreType.DMA((ROW_BUFFERS,))]),
    )
    return pl.pallas_call(
        _experts_kernel,
        grid_spec=grid_spec,
        out_shape=jax.ShapeDtypeStruct((n_tiles * tg, d), F32),
        compiler_params=pltpu.CompilerParams(
            dimension_semantics=("arbitrary",), vmem_limit_bytes=_vmem_limit(57 << 20)),
        name="experts",
    )(tile_expert, n_valid, tile_start, tok_sorted, t_f32, w_gate, w_up, w_down)


def _combine_kernel(dest_hbm, y_hbm, gk_ref, t_ref, wsg_ref, wsu_ref, wsd_ref, x1_ref, gpost_ref, gate_ref, o_ref,
                    idx0, idx1, rows0, rows1, sem_idx, sem_row, *, tc, n_steps):
    i = pl.program_id(0)
    n_idx = TOP_K * tc
    idx_bufs, row_bufs = (idx0, idx1), (rows0, rows1)

    def idx_copy(j, j_static):
        slot = j_static % 2
        start = pl.multiple_of(jnp.minimum(j, n_steps - 1) * n_idx, n_idx)
        return pltpu.make_async_copy(dest_hbm.at[pl.ds(start, n_idx)], idx_bufs[slot], sem_idx.at[slot])

    def row_copy(j_static, k, t):
        slot = j_static % 2
        return pltpu.make_async_copy(y_hbm.at[idx_bufs[slot][k * tc + t]], row_bufs[slot].at[k, t], sem_row.at[slot])

    def wait_rows(slot):
        for k in range(TOP_K):
            pltpu.make_async_copy(y_hbm.at[pl.ds(0, tc)], row_bufs[slot].at[k], sem_row.at[slot]).wait()

    def step(phase):
        if phase == 0:
            @pl.when(i == 0)
            def _():
                idx_copy(0, 0).start()
                idx_copy(0, 0).wait()

                def first_rows(t, c):
                    for k in range(TOP_K):
                        row_copy(0, k, t).start()
                    return c

                lax.fori_loop(0, tc, first_rows, 0)
                idx_copy(1, 1).start()

        idx_copy(i + 1, phase + 1).wait()
        idx_copy(i + 2, phase + 2).start()
        wait_rows(phase)
        for t in range(tc):
            for k in range(TOP_K):
                row_copy(phase + 1, k, t).start()

        rows_ref = row_bufs[phase]
        t = t_ref[...].astype(BF16)
        hid = _silu(_dot(t, wsg_ref[...])) * _dot(t, wsu_ref[...])
        f = _dot(hid.astype(BF16), wsd_ref[...])
        g = gk_ref[...]
        for k in range(TOP_K):
            f = f + g[:, k:k + 1] * rows_ref[k]
        o_ref[...] = x1_ref[...] + gate_ref[0] * (_rms(f) * gpost_ref[...])

        @pl.when(i == n_steps - 1)
        def _():
            wait_rows(1 - phase)
            idx_copy(i + 2, phase + 2).wait()

    for phase in range(2):
        @pl.when((i & 1) == phase)
        def _(phase=phase):
            step(phase)


def _combine_call(dest_steps, y, gk, t_f32, wsg_bf, wsu_bf, wsd_bf, x1, g_post, mod3, seq):
    t, d = x1.shape
    tc = COMBINE_TOKENS
    tiles_per_seq = seq // tc
    return pl.pallas_call(
        functools.partial(_combine_kernel, tc=tc, n_steps=t // tc),
        grid=(t // tc,),
        in_specs=[
            pl.BlockSpec(memory_space=pl.ANY),
            pl.BlockSpec(memory_space=pl.ANY),
            pl.BlockSpec((tc, LANES), lambda i: (i, 0)),
            pl.BlockSpec((tc, d), lambda i: (i, 0)),
            pl.BlockSpec(wsg_bf.shape, lambda i: (0, 0)),
            pl.BlockSpec(wsu_bf.shape, lambda i: (0, 0)),
            pl.BlockSpec(wsd_bf.shape, lambda i: (0, 0)),
            pl.BlockSpec((tc, d), lambda i: (i, 0)),
            pl.BlockSpec((1, d), lambda i: (0, 0)),
            pl.BlockSpec((1, 1, d), lambda i: (i // tiles_per_seq, 0, 5)),
        ],
        out_specs=pl.BlockSpec((tc, d), lambda i: (i, 0)),
        out_shape=jax.ShapeDtypeStruct((t, d), F32),
        scratch_shapes=[
            pltpu.SMEM((TOP_K * tc,), I32),
            pltpu.SMEM((TOP_K * tc,), I32),
            pltpu.VMEM((TOP_K, tc, d), F32),
            pltpu.VMEM((TOP_K, tc, d), F32),
            pltpu.SemaphoreType.DMA((2,)),
            pltpu.SemaphoreType.DMA((2,)),
        ],
        compiler_params=pltpu.CompilerParams(
            dimension_semantics=("arbitrary",), vmem_limit_bytes=_vmem_limit(48 << 20)),
        name="combine",
    )(dest_steps, y, gk, t_f32, wsg_bf, wsu_bf, wsd_bf, x1, g_post, mod3)


def kernel(x, c, ctx, c_ctx, w_mod, b_mod, g_pre_mix, g_post_mix, g_pre_ffn, g_post_ffn, w_in, conv_w, conv_b,
           rpb, g_conv_out, g_attn_out, w_out, w_router, router_bias, w_exp_gate, w_exp_up, w_exp_down,
           w_sh_gate, w_sh_up, w_sh_down):
    batch, seq, d = x.shape
    ctx_len = ctx.shape[1]
    n_tok = batch * seq
    assert w_mod.shape[0] == 1, "single-layer kernel"
    assert seq % (Q_ROWS * GRID_W) == 0 and seq // GRID_W >= K_ROWS
    assert w_router.shape[2] == N_EXPERTS and seq % COMBINE_TOKENS == 0
    assert EXPERT_ROW_TILE <= IDX_ALIGN and n_tok * TOP_K % EXPERT_ROW_TILE == 0
    d_attn = N_HEADS * HEAD_DIM
    kv_offset = w_in.shape[2] - 2 * d_attn
    row2 = lambda a: a.reshape(1, -1)

    x2 = x.reshape(n_tok, d)
    ctx2 = ctx.reshape(batch * ctx_len, d)

    n_rows = -(-(batch + 1) // 8) * 8
    cvec = jnp.concatenate([c, c_ctx[None], jnp.zeros((n_rows - batch - 1, d), F32)], axis=0)
    mod = _mod_call(cvec, w_mod[0], row2(b_mod[0]))
    mod3 = mod.reshape(n_rows, 1, 6 * d)

    w_in_bf = _win_prep_call(w_in[0], w_in.shape[2] - 3 * d_attn)
    tm_in = 1024
    tiles_per_seq = seq // tm_in
    proj = _inproj_call(x2, row2(g_pre_mix[0]), mod3, w_in_bf, lambda i: i // tiles_per_seq,
                        tm_in, 1024, "in_proj")
    proj_ctx = _inproj_call(ctx2, row2(g_pre_mix[0]), mod3, w_in_bf, lambda i: batch,
                            min(tm_in, ctx2.shape[0]), 1024, "in_proj_ctx", first_col=kv_offset)

    cos_t, sin_t = _rope_tables(seq)
    bias = _bias_table(rpb[0], seq // GRID_W)
    ya = _attn_call(proj, proj_ctx, cos_t, sin_t, bias, row2(g_attn_out[0]), batch, seq, ctx_len)
    yc = _conv_call(proj, conv_w[0], row2(conv_b[0]), row2(g_conv_out[0]), batch, seq)

    x1, t_f32, logits_t = _outproj_call(yc, ya, w_out[0].astype(BF16), x2, row2(g_post_mix[0]),
                                        row2(g_pre_ffn[0]), mod3, w_router[0].T, seq)
    ek, rk, gk, cnt = _router_call(logits_t, router_bias[0].reshape(-1, 1))

    n_tiles_max = n_tok * TOP_K // EXPERT_ROW_TILE + N_EXPERTS
    dest_c, tok_sorted, tile_expert, n_valid, tile_start = _routing_tables(
        ek, rk, cnt[:, 0], n_tiles_max, COMBINE_TOKENS)
    y = _experts_call(tile_expert, n_valid, tile_start, tok_sorted, t_f32,
                      w_exp_gate[0], w_exp_up[0], w_exp_down[0])
    out = _combine_call(dest_c, y, gk, t_f32, w_sh_gate[0].astype(BF16), w_sh_up[0].astype(BF16),
                        w_sh_down[0].astype(BF16), x1, row2(g_post_ffn[0]), mod3, seq)
    return out.reshape(batch, seq, d)
```

```python
import functools

import numpy as np
import jax
import jax.numpy as jnp
from jax import lax
from jax.experimental import pallas as pl
from jax.experimental.pallas import tpu as pltpu

F32 = jnp.float32
BF16 = jnp.bfloat16
I32 = jnp.int32

GRID_W = 64
N_HEADS = 8
HEAD_DIM = 128
NA_ROWS = 8
NA_COLS = 16
ROPE_THETA = 10000.0
CONV_GROUP_DIM = 128
N_EXPERTS = 64
N_EXPERT_GROUPS = 8
TOPK_GROUPS = 4
TOP_K = 8
ROUTED_SCALE = 2.5
EPS = 1e-6
MASK_VALUE = -1e30

Q_ROWS = 4
K_ROWS = 12
ATTN_BLOCKS_PER_ITER = 4

V7X_VMEM_BYTES = 64 * 1024 * 1024
LANES = 128

EXPERT_ROW_TILE = 512
ROW_BUFFERS = 2
ROW_LOOKAHEAD = ROW_BUFFERS - 1
COMBINE_TOKENS = 128
IDX_ALIGN = 1024
IDX_WINDOW = 2 * IDX_ALIGN


def _vmem_limit(nbytes):
    return int(min(nbytes, V7X_VMEM_BYTES - 6 * 1024 * 1024))


def _rms(x):
    return x * lax.rsqrt(jnp.mean(x * x, axis=-1, keepdims=True) + EPS)


def _silu(x):
    return x * jax.nn.sigmoid(x)


def _dot(a, b):
    return jnp.dot(a, b, preferred_element_type=F32)


def _dot_nt(a, b, precision=None):
    return lax.dot_general(a, b, (((1,), (1,)), ((), ())), preferred_element_type=F32, precision=precision)


def _mod_kernel(c_ref, w_ref, b_ref, o_ref):
    a = _silu(c_ref[...]).astype(BF16)
    o_ref[...] = _dot(a, w_ref[...].astype(BF16)) + b_ref[...]


def _mod_call(cvec, w_mod, b_mod, tn=1024):
    rows, d = cvec.shape
    n = w_mod.shape[1]
    return pl.pallas_call(
        _mod_kernel,
        grid=(n // tn,),
        in_specs=[
            pl.BlockSpec((rows, d), lambda j: (0, 0)),
            pl.BlockSpec((d, tn), lambda j: (0, j)),
            pl.BlockSpec((1, tn), lambda j: (0, j)),
        ],
        out_specs=pl.BlockSpec((rows, tn), lambda j: (0, j)),
        out_shape=jax.ShapeDtypeStruct((rows, n), F32),
        compiler_params=pltpu.CompilerParams(
            dimension_semantics=("arbitrary",), vmem_limit_bytes=_vmem_limit(40 << 20)),
        name="mod",
    )(cvec, w_mod, b_mod)


def _inproj_kernel(x_ref, g_ref, sh_ref, sc_ref, w_ref, o_ref, h_ref):
    @pl.when(pl.program_id(1) == 0)
    def _():
        h = _rms(x_ref[...]) * g_ref[...]
        h = h * (1.0 + sc_ref[0]) + sh_ref[0]
        h_ref[...] = h.astype(BF16)

    o_ref[...] = _dot(h_ref[...], w_ref[...])


def _inproj_call(x2, g, mod3, w_bf, row_of_tile, tm, tn, name, first_col=0):
    t, d = x2.shape
    n = w_bf.shape[1] - first_col
    col0 = first_col // tn
    assert first_col % tn == 0
    return pl.pallas_call(
        _inproj_kernel,
        grid=(t // tm, n // tn),
        in_specs=[
            pl.BlockSpec((tm, d), lambda i, j: (i, 0)),
            pl.BlockSpec((1, d), lambda i, j: (0, 0)),
            pl.BlockSpec((1, 1, d), lambda i, j: (row_of_tile(i), 0, 0)),
            pl.BlockSpec((1, 1, d), lambda i, j: (row_of_tile(i), 0, 1)),
            pl.BlockSpec((d, tn), lambda i, j: (0, col0 + j)),
        ],
        out_specs=pl.BlockSpec((tm, tn), lambda i, j: (i, j)),
        out_shape=jax.ShapeDtypeStruct((t, n), F32),
        scratch_shapes=[pltpu.VMEM((tm, d), BF16)],
        compiler_params=pltpu.CompilerParams(
            dimension_semantics=("arbitrary", "arbitrary"), vmem_limit_bytes=_vmem_limit(52 << 20)),
        name=name,
    )(x2, g, mod3, mod3, w_bf)


def _attn_kernel(q_ref, k_ref, v_ref, kc_ref, vc_ref, cos_ref, sin_ref, bias_ref, g_ref, o_ref,
                 qs_ref, ks_ref, vs_ref, kcs_ref, vcs_ref, *, block_type):
    cos = cos_ref[...]
    sin = sin_ref[...]

    def rope(x):
        return x * cos + pltpu.roll(x, HEAD_DIM // 2, 1) * sin

    qs_ref[...] = rope(q_ref[...]).astype(BF16)
    ks_ref[...] = rope(k_ref[...]).astype(BF16)
    vs_ref[...] = v_ref[...].astype(BF16)
    kcs_ref[...] = kc_ref[...].astype(BF16)
    vcs_ref[...] = vc_ref[...].astype(BF16)

    scale = HEAD_DIM ** -0.5
    qb = Q_ROWS * GRID_W
    kb = K_ROWS * GRID_W
    n_blocks = q_ref.shape[0] // qb
    max_start = q_ref.shape[0] // GRID_W - K_ROWS

    def block(j):
        q0 = pl.multiple_of(j * qb, qb)
        k_row = jnp.clip(j * Q_ROWS - NA_ROWS // 2, 0, max_start)
        k0 = pl.multiple_of(k_row * GRID_W, qb)
        q = qs_ref[pl.ds(q0, qb), :]
        btype = 0
        for jj, tt in enumerate(block_type):
            btype = jnp.where(j == jj, tt, btype)
        s_loc = _dot_nt(q, ks_ref[pl.ds(k0, kb), :]) * scale + bias_ref[0, btype]
        s_ctx = _dot_nt(q, kcs_ref[...]) * scale
        m = jnp.maximum(jnp.max(s_loc, axis=-1, keepdims=True), jnp.max(s_ctx, axis=-1, keepdims=True))
        p_loc = jnp.exp(s_loc - m)
        p_ctx = jnp.exp(s_ctx - m)
        denom = jnp.sum(p_loc, axis=-1, keepdims=True) + jnp.sum(p_ctx, axis=-1, keepdims=True)
        o = _dot(p_loc.astype(BF16), vs_ref[pl.ds(k0, kb), :]) + _dot(p_ctx.astype(BF16), vcs_ref[...])
        o = o / denom
        o_ref[pl.ds(q0, qb), :] = (_rms(o) * g_ref[...]).astype(BF16)

    def body(p, carry):
        for u in range(ATTN_BLOCKS_PER_ITER):
            block(p * ATTN_BLOCKS_PER_ITER + u)
        return carry

    lax.fori_loop(0, n_blocks // ATTN_BLOCKS_PER_ITER, body, 0)


def _attn_call(proj, proj_ctx, cos_t, sin_t, bias, g_attn, batch, seq, ctx_len):
    d_attn = N_HEADS * HEAD_DIM
    q_blk = (proj.shape[1] - 3 * d_attn) // HEAD_DIM
    n_types = bias.shape[1]
    _, block_type = _block_row_offsets(seq // GRID_W)
    qb, kb = Q_ROWS * GRID_W, K_ROWS * GRID_W
    return pl.pallas_call(
        functools.partial(_attn_kernel, block_type=block_type),
        grid=(N_HEADS, batch),
        in_specs=[
            pl.BlockSpec((seq, HEAD_DIM), lambda h, b: (b, q_blk + h)),
            pl.BlockSpec((seq, HEAD_DIM), lambda h, b: (b, q_blk + N_HEADS + h)),
            pl.BlockSpec((seq, HEAD_DIM), lambda h, b: (b, q_blk + 2 * N_HEADS + h)),
            pl.BlockSpec((ctx_len, HEAD_DIM), lambda h, b: (b, h)),
            pl.BlockSpec((ctx_len, HEAD_DIM), lambda h, b: (b, N_HEADS + h)),
            pl.BlockSpec((seq, HEAD_DIM), lambda h, b: (0, 0)),
            pl.BlockSpec((seq, HEAD_DIM), lambda h, b: (0, 0)),
            pl.BlockSpec((1, n_types, qb, kb), lambda h, b: (h, 0, 0, 0)),
            pl.BlockSpec((1, HEAD_DIM), lambda h, b: (0, h)),
        ],
        out_specs=pl.BlockSpec((seq, HEAD_DIM), lambda h, b: (b, h)),
        out_shape=jax.ShapeDtypeStruct((batch * seq, d_attn), BF16),
        scratch_shapes=[
            pltpu.VMEM((seq, HEAD_DIM), BF16),
            pltpu.VMEM((seq, HEAD_DIM), BF16),
            pltpu.VMEM((seq, HEAD_DIM), BF16),
            pltpu.VMEM((ctx_len, HEAD_DIM), BF16),
            pltpu.VMEM((ctx_len, HEAD_DIM), BF16),
        ],
        compiler_params=pltpu.CompilerParams(
            dimension_semantics=("arbitrary", "arbitrary"), vmem_limit_bytes=_vmem_limit(48 << 20)),
        name="attn",
    )(proj, proj, proj, proj_ctx, proj_ctx, cos_t, sin_t, bias, g_attn)


def _rope_tables(seq):
    t = jnp.arange(seq, dtype=I32)
    pos = jnp.stack([t // GRID_W, t % GRID_W], axis=-1).astype(F32)
    n_freq = HEAD_DIM // 4
    inv_freq = ROPE_THETA ** (-jnp.arange(n_freq, dtype=F32) / n_freq)
    ang = pos[:, :, None] * inv_freq
    cos, sin = jnp.cos(ang).reshape(seq, 2 * n_freq), jnp.sin(ang).reshape(seq, 2 * n_freq)
    cos_t = jnp.concatenate([cos, cos], axis=-1)
    sin_t = jnp.concatenate([-sin, sin], axis=-1)
    return cos_t, sin_t


def _win_prep_kernel(w_ref, o_ref, *, first_qk_block, n_qk_blocks):
    j = pl.program_id(0)
    is_qk = (j >= first_qk_block) & (j < first_qk_block + n_qk_blocks)

    @pl.when(is_qk)
    def _():
        q = HEAD_DIM // 4
        for c0 in range(0, w_ref.shape[1], HEAD_DIM):
            x = w_ref[:, c0:c0 + HEAD_DIM]
            x = jnp.concatenate([x[:, 0:q], x[:, 2 * q:3 * q], x[:, q:2 * q], x[:, 3 * q:]], axis=1)
            o_ref[:, c0:c0 + HEAD_DIM] = x.astype(BF16)

    @pl.when(jnp.logical_not(is_qk))
    def _():
        o_ref[...] = w_ref[...].astype(BF16)


def _win_prep_call(w_in, n_lead_cols, tn=1024):
    rows, cols = w_in.shape
    qk_cols = 2 * N_HEADS * HEAD_DIM
    assert n_lead_cols % tn == 0 and qk_cols % tn == 0
    return pl.pallas_call(
        functools.partial(_win_prep_kernel, first_qk_block=n_lead_cols // tn, n_qk_blocks=qk_cols // tn),
        grid=(cols // tn,),
        in_specs=[pl.BlockSpec((rows, tn), lambda j: (0, j))],
        out_specs=pl.BlockSpec((rows, tn), lambda j: (0, j)),
        out_shape=jax.ShapeDtypeStruct((rows, cols), BF16),
        compiler_params=pltpu.CompilerParams(
            dimension_semantics=("arbitrary",), vmem_limit_bytes=_vmem_limit(40 << 20)),
        name="w_in_prep",
    )(w_in)


def _block_row_offsets(rows):
    n_blocks = rows // Q_ROWS
    kh = min(NA_ROWS, rows)
    sel = np.zeros((n_blocks, Q_ROWS, K_ROWS), np.int32)
    for j in range(n_blocks):
        k_start = int(np.clip(j * Q_ROWS - NA_ROWS // 2, 0, rows - K_ROWS))
        for a in range(Q_ROWS):
            r = j * Q_ROWS + a
            r0 = int(np.clip(r - kh // 2, 0, rows - kh))
            for b in range(K_ROWS):
                kr = k_start + b
                sel[j, a, b] = kr - r + NA_ROWS - 1 if r0 <= kr < r0 + kh else 2 * NA_ROWS - 1
    uniq, block_type = np.unique(sel.reshape(n_blocks, -1), axis=0, return_inverse=True)
    return uniq.reshape(-1, Q_ROWS, K_ROWS), tuple(int(v) for v in np.ravel(block_type))


def _bias_kernel(rpb_ref, o_ref, *, sel):
    w = GRID_W
    qc = lax.broadcasted_iota(I32, (w, w), 0)
    kc = lax.broadcasted_iota(I32, (w, w), 1)
    c0 = jnp.clip(qc - NA_COLS // 2, 0, w - NA_COLS)
    col_ok = (kc >= c0) & (kc < c0 + NA_COLS)
    blocks = []
    for ro in range(rpb_ref.shape[1]):
        r = jnp.broadcast_to(rpb_ref[0, ro:ro + 1, :], (w, LANES))
        rolled = pltpu.roll(r, LANES - (NA_COLS - 1), 1, stride=1, stride_axis=0)
        blocks.append(jnp.where(col_ok, rolled[:, :w], MASK_VALUE))
    blocks.append(jnp.full((w, w), MASK_VALUE, F32))
    for ty in range(sel.shape[0]):
        for a in range(Q_ROWS):
            o_ref[0, ty, a * w:(a + 1) * w, :] = jnp.concatenate(
                [blocks[int(sel[ty, a, b])] for b in range(K_ROWS)], axis=1)


def _bias_table(rpb, rows):
    n_heads, n_ro, n_co = rpb.shape
    sel, _ = _block_row_offsets(rows)
    rpb_lanes = jnp.pad(rpb, ((0, 0), (0, 0), (0, LANES - n_co)))
    return pl.pallas_call(
        functools.partial(_bias_kernel, sel=sel),
        grid=(n_heads,),
        in_specs=[pl.BlockSpec((1, n_ro, LANES), lambda h: (h, 0, 0))],
        out_specs=pl.BlockSpec((1, sel.shape[0], Q_ROWS * GRID_W, K_ROWS * GRID_W), lambda h: (h, 0, 0, 0)),
        out_shape=jax.ShapeDtypeStruct((n_heads, sel.shape[0], Q_ROWS * GRID_W, K_ROWS * GRID_W), F32),
        compiler_params=pltpu.CompilerParams(dimension_semantics=("arbitrary",)),
        name="bias_table",
    )(rpb_lanes)


def _conv_kernel(ub_ref, uc_ref, ux_ref, w_ref, b_ref, g_ref, o_ref):
    p = uc_ref[...] * ux_ref[...]
    seq = p.shape[0]
    row = lax.broadcasted_iota(I32, p.shape, 0)
    prev = jnp.where(row == 0, 0.0, pltpu.roll(p, 1, 0))
    nxt = jnp.where(row == seq - 1, 0.0, pltpu.roll(p, seq - 1, 0))
    w = w_ref[...]
    conv = prev * w[0:1] + p * w[1:2] + nxt * w[2:3] + b_ref[...]
    y = ub_ref[...] * conv
    g = g_ref[...]
    for c0 in range(0, y.shape[1], CONV_GROUP_DIM):
        sl = slice(c0, c0 + CONV_GROUP_DIM)
        o_ref[:, sl] = (_rms(y[:, sl]) * g[:, sl]).astype(BF16)


def _conv_call(proj, conv_w, conv_b, g_conv, batch, seq, cb=256):
    d_conv = conv_w.shape[1]
    nc = d_conv // cb
    return pl.pallas_call(
        _conv_kernel,
        grid=(batch, nc),
        in_specs=[
            pl.BlockSpec((seq, cb), lambda b, c: (b, c)),
            pl.BlockSpec((seq, cb), lambda b, c: (b, nc + c)),
            pl.BlockSpec((seq, cb), lambda b, c: (b, 2 * nc + c)),
            pl.BlockSpec((conv_w.shape[0], cb), lambda b, c: (0, c)),
            pl.BlockSpec((1, cb), lambda b, c: (0, c)),
            pl.BlockSpec((1, cb), lambda b, c: (0, c)),
        ],
        out_specs=pl.BlockSpec((seq, cb), lambda b, c: (b, c)),
        out_shape=jax.ShapeDtypeStruct((batch * seq, d_conv), BF16),
        compiler_params=pltpu.CompilerParams(
            dimension_semantics=("arbitrary", "arbitrary"), vmem_limit_bytes=_vmem_limit(48 << 20)),
        name="conv",
    )(proj, proj, proj, conv_w, conv_b, g_conv)


def _outproj_kernel(yc_ref, ya_ref, w_ref, x_ref, gpost_ref, gate_ref, gpre_ref, sh_ref, sc_ref, wr_ref,
                    x1_ref, t_ref, lg_ref):
    dc = yc_ref.shape[1]
    y = _dot(yc_ref[...], w_ref[0:dc, :]) + _dot(ya_ref[...], w_ref[dc:, :])
    x1 = x_ref[...] + gate_ref[0] * (_rms(y) * gpost_ref[...])
    x1_ref[...] = x1
    t = (_rms(x1) * gpre_ref[...]) * (1.0 + sc_ref[0]) + sh_ref[0]
    t_ref[...] = t
    ne = wr_ref.shape[0]
    wr = wr_ref[...]
    wr_hi = wr.astype(BF16)
    wr_lo = (wr - wr_hi.astype(F32)).astype(BF16)
    t_hi = t.astype(BF16)
    t_lo = (t - t_hi.astype(F32)).astype(BF16)
    both = _dot_nt(t_hi, jnp.concatenate([wr_hi, wr_lo], axis=0))
    logits = both[:, :ne] + (_dot_nt(t_lo, wr_hi) + both[:, ne:])
    lg_ref[...] = logits.T


def _outproj_call(yc, ya, w_out_bf, x2, g_post, g_pre, mod3, w_router_t, seq, tm=512):
    t, d = x2.shape
    dc = yc.shape[1]
    ne = w_router_t.shape[0]
    tiles_per_seq = seq // tm
    row = lambda i: i // tiles_per_seq
    return pl.pallas_call(
        _outproj_kernel,
        grid=(t // tm,),
        in_specs=[
            pl.BlockSpec((tm, dc), lambda i: (i, 0)),
            pl.BlockSpec((tm, ya.shape[1]), lambda i: (i, 0)),
            pl.BlockSpec(w_out_bf.shape, lambda i: (0, 0), pipeline_mode=pl.Buffered(1)),
            pl.BlockSpec((tm, d), lambda i: (i, 0)),
            pl.BlockSpec((1, d), lambda i: (0, 0)),
            pl.BlockSpec((1, 1, d), lambda i: (row(i), 0, 2)),
            pl.BlockSpec((1, d), lambda i: (0, 0)),
            pl.BlockSpec((1, 1, d), lambda i: (row(i), 0, 3)),
            pl.BlockSpec((1, 1, d), lambda i: (row(i), 0, 4)),
            pl.BlockSpec((ne, d), lambda i: (0, 0)),
        ],
        out_specs=[
            pl.BlockSpec((tm, d), lambda i: (i, 0)),
            pl.BlockSpec((tm, d), lambda i: (i, 0)),
            pl.BlockSpec((ne, tm), lambda i: (0, i)),
        ],
        out_shape=[
            jax.ShapeDtypeStruct((t, d), F32),
            jax.ShapeDtypeStruct((t, d), F32),
            jax.ShapeDtypeStruct((ne, t), F32),
        ],
        compiler_params=pltpu.CompilerParams(
            dimension_semantics=("arbitrary",), vmem_limit_bytes=_vmem_limit(57 << 20)),
        name="out_proj",
    )(yc, ya, w_out_bf, x2, g_post, mod3, g_pre, mod3, mod3, w_router_t)


def _first_argmax_mask(v, axis):
    m = jnp.max(v, axis=axis, keepdims=True)
    idx = lax.broadcasted_iota(I32, v.shape, axis)
    big = v.shape[axis]
    first = jnp.min(jnp.where(v == m, idx, big), axis=axis, keepdims=True)
    return idx == first, m


def _router_kernel(lg_ref, rb_ref, ek_ref, rk_ref, gk_ref, cnt_ref, carry_ref):
    ne, tm = lg_ref.shape
    per = ne // N_EXPERT_GROUPS

    @pl.when(pl.program_id(0) == 0)
    def _():
        carry_ref[...] = jnp.zeros_like(carry_ref)

    scores = jax.nn.sigmoid(lg_ref[...])
    sel = scores + rb_ref[...]
    s3 = sel.reshape(N_EXPERT_GROUPS, per, tm)
    hit1, m1 = _first_argmax_mask(s3, 1)
    m2 = jnp.max(jnp.where(hit1, -jnp.inf, s3), axis=1, keepdims=True)
    grp = (m1 + m2).reshape(N_EXPERT_GROUPS, tm)
    gmask = jnp.zeros(grp.shape, jnp.bool_)
    for _ in range(TOPK_GROUPS):
        hit, _ = _first_argmax_mask(jnp.where(gmask, -jnp.inf, grp), 0)
        gmask = gmask | hit
    emask = jnp.broadcast_to(gmask.reshape(N_EXPERT_GROUPS, 1, tm), s3.shape).reshape(ne, tm)
    cand = jnp.where(emask, sel, -jnp.inf)
    chosen = jnp.zeros(cand.shape, jnp.bool_)
    hits = []
    for _ in range(TOP_K):
        hit, _ = _first_argmax_mask(jnp.where(chosen, -jnp.inf, cand), 0)
        hit = hit & ~chosen
        hits.append(hit)
        chosen = chosen | hit
    wsel = jnp.where(chosen, scores, 0.0)
    gates = wsel / jnp.sum(wsel, axis=0, keepdims=True) * ROUTED_SCALE

    r_i = lax.broadcasted_iota(I32, (tm, tm), 0)
    c_i = lax.broadcasted_iota(I32, (tm, tm), 1)
    tri = jnp.where(r_i <= c_i, 1.0, 0.0).astype(BF16)
    csum = _dot(jnp.where(chosen, 1.0, 0.0).astype(BF16), tri)
    carry = carry_ref[:, 0:1]
    rank = carry + csum - 1.0
    eidx = lax.broadcasted_iota(I32, (ne, tm), 0).astype(F32)

    def pick(hit, v):
        return jnp.sum(jnp.where(hit, v, 0.0), axis=0, keepdims=True)

    ek_ref[...] = jnp.concatenate([pick(h, eidx) for h in hits], axis=0).astype(I32)
    rk_ref[...] = jnp.concatenate([pick(h, rank) for h in hits], axis=0).astype(I32)
    gk_rows = jnp.concatenate([pick(h, gates) for h in hits] + [jnp.zeros((LANES - TOP_K, tm), F32)], axis=0)
    gk_ref[...] = gk_rows.T
    new_carry = carry + csum[:, tm - 1:tm]
    carry_ref[...] = jnp.broadcast_to(new_carry, carry_ref.shape)
    cnt_ref[...] = jnp.broadcast_to(new_carry, cnt_ref.shape).astype(I32)


def _router_call(logits_t, router_bias_col, tm=512):
    ne, t = logits_t.shape
    return pl.pallas_call(
        _router_kernel,
        grid=(t // tm,),
        in_specs=[
            pl.BlockSpec((ne, tm), lambda i: (0, i)),
            pl.BlockSpec((ne, 1), lambda i: (0, 0)),
        ],
        out_specs=[
            pl.BlockSpec((TOP_K, tm), lambda i: (0, i)),
            pl.BlockSpec((TOP_K, tm), lambda i: (0, i)),
            pl.BlockSpec((tm, LANES), lambda i: (i, 0)),
            pl.BlockSpec((ne, LANES), lambda i: (0, 0)),
        ],
        out_shape=[
            jax.ShapeDtypeStruct((TOP_K, t), I32),
            jax.ShapeDtypeStruct((TOP_K, t), I32),
            jax.ShapeDtypeStruct((t, LANES), F32),
            jax.ShapeDtypeStruct((ne, LANES), I32),
        ],
        scratch_shapes=[pltpu.VMEM((ne, LANES), F32)],
        compiler_params=pltpu.CompilerParams(dimension_semantics=("arbitrary",)),
        name="router",
    )(logits_t, router_bias_col)


def _routing_tables(ek, rk, counts, n_tiles_max, tokens_per_step):
    tg = EXPERT_ROW_TILE
    k, t = ek.shape
    n_exp = counts.shape[0]
    padded = (counts + tg - 1) // tg * tg
    before = np.tril(np.ones((n_exp, n_exp), bool), -1)
    offsets = jnp.sum(jnp.where(before, padded[None, :], 0), axis=1)
    compact_offsets = jnp.sum(jnp.where(before, counts[None, :], 0), axis=1)
    ends = offsets + padded
    dest = rk
    for e in range(n_exp):
        dest = dest + jnp.where(ek == e, offsets[e], 0)
    tok = jnp.arange(t, dtype=I32)
    tok_sorted = lax.rem(jnp.sort((ek * t + tok).reshape(-1)), t)
    tok_sorted = jnp.concatenate([tok_sorted, jnp.zeros((IDX_WINDOW,), I32)])
    tile_ends = ends // tg
    n_valid = jnp.sum(padded) // tg
    tile_ids = jnp.arange(n_tiles_max, dtype=I32)
    tile_expert = jnp.sum(tile_ids[:, None] >= tile_ends[None, :], axis=1).astype(I32)
    last_expert = jnp.sum(n_valid - 1 >= tile_ends).astype(I32)
    tile_expert = jnp.where(tile_ids < n_valid, tile_expert, last_expert)
    own = tile_expert[:, None] == jnp.arange(n_exp, dtype=I32)[None, :]
    tile_start = tile_ids * tg + jnp.sum(jnp.where(own, (compact_offsets - offsets)[None, :], 0), axis=1)
    tile_start = jnp.where(tile_ids < n_valid, tile_start, 0).astype(I32)
    dest_steps = dest.reshape(k, t // tokens_per_step, tokens_per_step).transpose(1, 0, 2).reshape(-1)
    return dest_steps, tok_sorted, tile_expert, n_valid.reshape(1).astype(I32), tile_start


def _experts_kernel(te_ref, nv_ref, ts_ref, tok_hbm, t_hbm, wg_ref, wu_ref, wd_ref, y_ref, *scratch):
    idx_bufs = scratch[:2]
    x_bufs = scratch[2:2 + ROW_BUFFERS]
    wg_s, wu_s, wd_s, sem_idx, sem_row = scratch[2 + ROW_BUFFERS:]
    i = pl.program_id(0)
    nv = nv_ref[0]
    tg = y_ref.shape[0]
    ahead = ROW_LOOKAHEAD

    def tile_of(j):
        return jnp.minimum(j, nv - 1)

    def idx_copy(j, j_static):
        window = pl.multiple_of(lax.shift_left(lax.shift_right_logical(ts_ref[tile_of(j)], 10), 10), IDX_ALIGN)
        slot = j_static % 2
        return pltpu.make_async_copy(tok_hbm.at[pl.ds(window, IDX_WINDOW)], idx_bufs[slot], sem_idx.at[slot])

    def row_copy(j, j_static, r):
        base = ts_ref[tile_of(j)] & (IDX_ALIGN - 1)
        slot = j_static % ROW_BUFFERS
        return pltpu.make_async_copy(t_hbm.at[idx_bufs[j_static % 2][base + r]], x_bufs[slot].at[r],
                                     sem_row.at[slot])

    def wait_rows(slot):
        pltpu.make_async_copy(t_hbm.at[pl.ds(0, tg)], x_bufs[slot], sem_row.at[slot]).wait()

    def step(phase):
        if phase == 0:
            @pl.when(i == 0)
            def _():
                for j in range(ahead):
                    idx_copy(j, j).start()
                    idx_copy(j, j).wait()

                    def first_rows(r, c, j=j):
                        row_copy(j, j, r).start()
                        return c

                    lax.fori_loop(0, tg, first_rows, 0)
                idx_copy(ahead, ahead).start()

        idx_copy(i + ahead, phase + ahead).wait()
        idx_copy(i + ahead + 1, phase + ahead + 1).start()

        @pl.when((i == 0) | (te_ref[i] != te_ref[jnp.maximum(i - 1, 0)]))
        def _():
            wg_s[...] = wg_ref[0].astype(BF16)
            wu_s[...] = wu_ref[0].astype(BF16)
            wd_s[...] = wd_ref[0].astype(BF16)

        wait_rows(phase)
        x = x_bufs[phase][...].astype(BF16)
        hid = _silu(_dot(x, wg_s[...])) * _dot(x, wu_s[...])
        y_ref[...] = _dot(hid.astype(BF16), wd_s[...])
        for r in range(tg):
            row_copy(i + ahead, phase + ahead, r).start()

        @pl.when(i == nv - 1)
        def _():
            for other in range(1, ROW_BUFFERS):
                wait_rows((phase + other) % ROW_BUFFERS)
            idx_copy(i + ahead + 1, phase + ahead + 1).wait()

    for phase in range(ROW_BUFFERS):
        @pl.when((i < nv) & ((i & (ROW_BUFFERS - 1)) == phase))
        def _(phase=phase):
            step(phase)

    @pl.when(i >= nv)
    def _():
        y_ref[...] = jnp.zeros_like(y_ref)


def _experts_call(tile_expert, n_valid, tile_start, tok_sorted, t_f32, w_gate, w_up, w_down):
    _, d = t_f32.shape
    ne, _, df = w_gate.shape
    tg = EXPERT_ROW_TILE
    n_tiles = tile_expert.shape[0]
    grid_spec = pltpu.PrefetchScalarGridSpec(
        num_scalar_prefetch=3,
        grid=(n_tiles,),
        in_specs=[
            pl.BlockSpec(memory_space=pl.ANY),
            pl.BlockSpec(memory_space=pl.ANY),
            pl.BlockSpec((1, d, df), lambda i, te, nv, ts: (te[i], 0, 0)),
            pl.BlockSpec((1, d, df), lambda i, te, nv, ts: (te[i], 0, 0)),
            pl.BlockSpec((1, df, d), lambda i, te, nv, ts: (te[i], 0, 0)),
        ],
        out_specs=pl.BlockSpec((tg, d), lambda i, te, nv, ts: (i, 0)),
        scratch_shapes=(
            [pltpu.SMEM((IDX_WINDOW,), I32)] * 2
            + [pltpu.VMEM((tg, d), F32)] * ROW_BUFFERS
            + [pltpu.VMEM((d, df), BF16), pltpu.VMEM((d, df), BF16), pltpu.VMEM((df, d), BF16),
               pltpu.SemaphoreType.DMA((2,)), pltpu.SemaphoreType.DMA((ROW_BUFFERS,))]),
    )
    return pl.pallas_call(
        _experts_kernel,
        grid_spec=grid_spec,
        out_shape=jax.ShapeDtypeStruct((n_tiles * tg, d), F32),
        compiler_params=pltpu.CompilerParams(
            dimension_semantics=("arbitrary",), vmem_limit_bytes=_vmem_limit(57 << 20)),
        name="experts",
    )(tile_expert, n_valid, tile_start, tok_sorted, t_f32, w_gate, w_up, w_down)


def _combine_kernel(dest_hbm, y_hbm, gk_ref, t_ref, wsg_ref, wsu_ref, wsd_ref, x1_ref, gpost_ref, gate_ref, o_ref,
                    idx0, idx1, rows0, rows1, sem_idx, sem_row, *, tc, n_steps):
    i = pl.program_id(0)
    n_idx = TOP_K * tc
    idx_bufs, row_bufs = (idx0, idx1), (rows0, rows1)

    def idx_copy(j, j_static):
        slot = j_static % 2
        start = pl.multiple_of(jnp.minimum(j, n_steps - 1) * n_idx, n_idx)
        return pltpu.make_async_copy(dest_hbm.at[pl.ds(start, n_idx)], idx_bufs[slot], sem_idx.at[slot])

    def row_copy(j_static, k, t):
        slot = j_static % 2
        return pltpu.make_async_copy(y_hbm.at[idx_bufs[slot][k * tc + t]], row_bufs[slot].at[k, t], sem_row.at[slot])

    def wait_rows(slot):
        for k in range(TOP_K):
            pltpu.make_async_copy(y_hbm.at[pl.ds(0, tc)], row_bufs[slot].at[k], sem_row.at[slot]).wait()

    def step(phase):
        if phase == 0:
            @pl.when(i == 0)
            def _():
                idx_copy(0, 0).start()
                idx_copy(0, 0).wait()

                def first_rows(t, c):
                    for k in range(TOP_K):
                        row_copy(0, k, t).start()
                    return c

                lax.fori_loop(0, tc, first_rows, 0)
                idx_copy(1, 1).start()

        idx_copy(i + 1, phase + 1).wait()
        idx_copy(i + 2, phase + 2).start()
        wait_rows(phase)
        for t in range(tc):
            for k in range(TOP_K):
                row_copy(phase + 1, k, t).start()

        rows_ref = row_bufs[phase]
        t = t_ref[...].astype(BF16)
        hid = _silu(_dot(t, wsg_ref[...])) * _dot(t, wsu_ref[...])
        f = _dot(hid.astype(BF16), wsd_ref[...])
        g = gk_ref[...]
        for k in range(TOP_K):
            f = f + g[:, k:k + 1] * rows_ref[k]
        o_ref[...] = x1_ref[...] + gate_ref[0] * (_rms(f) * gpost_ref[...])

        @pl.when(i == n_steps - 1)
        def _():
            wait_rows(1 - phase)
            idx_copy(i + 2, phase + 2).wait()

    for phase in range(2):
        @pl.when((i & 1) == phase)
        def _(phase=phase):
            step(phase)


def _combine_call(dest_steps, y, gk, t_f32, wsg_bf, wsu_bf, wsd_bf, x1, g_post, mod3, seq):
    t, d = x1.shape
    tc = COMBINE_TOKENS
    tiles_per_seq = seq // tc
    return pl.pallas_call(
        functools.partial(_combine_kernel, tc=tc, n_steps=t // tc),
        grid=(t // tc,),
        in_specs=[
            pl.BlockSpec(memory_space=pl.ANY),
            pl.BlockSpec(memory_space=pl.ANY),
            pl.BlockSpec((tc, LANES), lambda i: (i, 0)),
            pl.BlockSpec((tc, d), lambda i: (i, 0)),
            pl.BlockSpec(wsg_bf.shape, lambda i: (0, 0)),
            pl.BlockSpec(wsu_bf.shape, lambda i: (0, 0)),
            pl.BlockSpec(wsd_bf.shape, lambda i: (0, 0)),
            pl.BlockSpec((tc, d), lambda i: (i, 0)),
            pl.BlockSpec((1, d), lambda i: (0, 0)),
            pl.BlockSpec((1, 1, d), lambda i: (i // tiles_per_seq, 0, 5)),
        ],
        out_specs=pl.BlockSpec((tc, d), lambda i: (i, 0)),
        out_shape=jax.ShapeDtypeStruct((t, d), F32),
        scratch_shapes=[
            pltpu.SMEM((TOP_K * tc,), I32),
            pltpu.SMEM((TOP_K * tc,), I32),
            pltpu.VMEM((TOP_K, tc, d), F32),
            pltpu.VMEM((TOP_K, tc, d), F32),
            pltpu.SemaphoreType.DMA((2,)),
            pltpu.SemaphoreType.DMA((2,)),
        ],
        compiler_params=pltpu.CompilerParams(
            dimension_semantics=("arbitrary",), vmem_limit_bytes=_vmem_limit(48 << 20)),
        name="combine",
    )(dest_steps, y, gk, t_f32, wsg_bf, wsu_bf, wsd_bf, x1, g_post, mod3)


def kernel(x, c, ctx, c_ctx, w_mod, b_mod, g_pre_mix, g_post_mix, g_pre_ffn, g_post_ffn, w_in, conv_w, conv_b,
           rpb, g_conv_out, g_attn_out, w_out, w_router, router_bias, w_exp_gate, w_exp_up, w_exp_down,
           w_sh_gate, w_sh_up, w_sh_down):
    batch, seq, d = x.shape
    ctx_len = ctx.shape[1]
    n_tok = batch * seq
    assert w_mod.shape[0] == 1, "single-layer kernel"
    assert seq % (Q_ROWS * GRID_W) == 0 and seq // GRID_W >= K_ROWS
    assert w_router.shape[2] == N_EXPERTS and seq % COMBINE_TOKENS == 0
    assert EXPERT_ROW_TILE <= IDX_ALIGN and n_tok * TOP_K % EXPERT_ROW_TILE == 0
    d_attn = N_HEADS * HEAD_DIM
    kv_offset = w_in.shape[2] - 2 * d_attn
    row2 = lambda a: a.reshape(1, -1)

    x2 = x.reshape(n_tok, d)
    ctx2 = ctx.reshape(batch * ctx_len, d)

    n_rows = -(-(batch + 1) // 8) * 8
    cvec = jnp.concatenate([c, c_ctx[None], jnp.zeros((n_rows - batch - 1, d), F32)], axis=0)
    mod = _mod_call(cvec, w_mod[0], row2(b_mod[0]))
    mod3 = mod.reshape(n_rows, 1, 6 * d)

    w_in_bf = _win_prep_call(w_in[0], w_in.shape[2] - 3 * d_attn)
    tm_in = 1024
    tiles_per_seq = seq // tm_in
    proj = _inproj_call(x2, row2(g_pre_mix[0]), mod3, w_in_bf, lambda i: i // tiles_per_seq,
                        tm_in, 1024, "in_proj")
    proj_ctx = _inproj_call(ctx2, row2(g_pre_mix[0]), mod3, w_in_bf, lambda i: batch,
                            min(tm_in, ctx2.shape[0]), 1024, "in_proj_ctx", first_col=kv_offset)

    cos_t, sin_t = _rope_tables(seq)
    bias = _bias_table(rpb[0], seq // GRID_W)
    ya = _attn_call(proj, proj_ctx, cos_t, sin_t, bias, row2(g_attn_out[0]), batch, seq, ctx_len)
    yc = _conv_call(proj, conv_w[0], row2(conv_b[0]), row2(g_conv_out[0]), batch, seq)

    x1, t_f32, logits_t = _outproj_call(yc, ya, w_out[0].astype(BF16), x2, row2(g_post_mix[0]),
                                        row2(g_pre_ffn[0]), mod3, w_router[0].T, seq)
    ek, rk, gk, cnt = _router_call(logits_t, router_bias[0].reshape(-1, 1))

    n_tiles_max = n_tok * TOP_K // EXPERT_ROW_TILE + N_EXPERTS
    dest_c, tok_sorted, tile_expert, n_valid, tile_start = _routing_tables(
        ek, rk, cnt[:, 0], n_tiles_max, COMBINE_TOKENS)
    y = _experts_call(tile_expert, n_valid, tile_start, tok_sorted, t_f32,
                      w_exp_gate[0], w_exp_up[0], w_exp_down[0])
    out = _combine_call(dest_c, y, gk, t_f32, w_sh_gate[0].astype(BF16), w_sh_up[0].astype(BF16),
                        w_sh_down[0].astype(BF16), x1, row2(g_post_ffn[0]), mod3, seq)
    return out.reshape(batch, seq, d)
```

```python
import functools

import numpy as np
import jax
import jax.numpy as jnp
from jax import lax
from jax.experimental import pallas as pl
from jax.experimental.pallas import tpu as pltpu

F32 = jnp.float32
BF16 = jnp.bfloat16
I32 = jnp.int32

GRID_W = 64
N_HEADS = 8
HEAD_DIM = 128
NA_ROWS = 8
NA_COLS = 16
ROPE_THETA = 10000.0
CONV_GROUP_DIM = 128
N_EXPERTS = 64
N_EXPERT_GROUPS = 8
TOPK_GROUPS = 4
TOP_K = 8
ROUTED_SCALE = 2.5
EPS = 1e-6
MASK_VALUE = -1e30

Q_ROWS = 4
K_ROWS = 12
ATTN_BLOCKS_PER_ITER = 4

V7X_VMEM_BYTES = 64 * 1024 * 1024
LANES = 128

EXPERT_ROW_TILE = 512
ROW_BUFFERS = 2
ROW_LOOKAHEAD = ROW_BUFFERS - 1
COMBINE_TOKENS = 128
IDX_ALIGN = 1024
IDX_WINDOW = 2 * IDX_ALIGN


def _vmem_limit(nbytes):
    return int(min(nbytes, V7X_VMEM_BYTES - 6 * 1024 * 1024))


def _rms(x):
    return x * lax.rsqrt(jnp.mean(x * x, axis=-1, keepdims=True) + EPS)


def _silu(x):
    return x * jax.nn.sigmoid(x)


def _dot(a, b):
    return jnp.dot(a, b, preferred_element_type=F32)


def _dot_nt(a, b, precision=None):
    return lax.dot_general(a, b, (((1,), (1,)), ((), ())), preferred_element_type=F32, precision=precision)


def _mod_kernel(c_ref, w_ref, b_ref, o_ref):
    a = _silu(c_ref[...]).astype(BF16)
    o_ref[...] = _dot(a, w_ref[...].astype(BF16)) + b_ref[...]


def _mod_call(cvec, w_mod, b_mod, tn=1024):
    rows, d = cvec.shape
    n = w_mod.shape[1]
    return pl.pallas_call(
        _mod_kernel,
        grid=(n // tn,),
        in_specs=[
            pl.BlockSpec((rows, d), lambda j: (0, 0)),
            pl.BlockSpec((d, tn), lambda j: (0, j)),
            pl.BlockSpec((1, tn), lambda j: (0, j)),
        ],
        out_specs=pl.BlockSpec((rows, tn), lambda j: (0, j)),
        out_shape=jax.ShapeDtypeStruct((rows, n), F32),
        compiler_params=pltpu.CompilerParams(
            dimension_semantics=("arbitrary",), vmem_limit_bytes=_vmem_limit(40 << 20)),
        name="mod",
    )(cvec, w_mod, b_mod)


def _inproj_kernel(x_ref, g_ref, sh_ref, sc_ref, w_ref, o_ref, h_ref):
    @pl.when(pl.program_id(1) == 0)
    def _():
        h = _rms(x_ref[...]) * g_ref[...]
        h = h * (1.0 + sc_ref[0]) + sh_ref[0]
        h_ref[...] = h.astype(BF16)

    o_ref[...] = _dot(h_ref[...], w_ref[...])


def _inproj_call(x2, g, mod3, w_bf, row_of_tile, tm, tn, name, first_col=0):
    t, d = x2.shape
    n = w_bf.shape[1] - first_col
    col0 = first_col // tn
    assert first_col % tn == 0
    return pl.pallas_call(
        _inproj_kernel,
        grid=(t // tm, n // tn),
        in_specs=[
            pl.BlockSpec((tm, d), lambda i, j: (i, 0)),
            pl.BlockSpec((1, d), lambda i, j: (0, 0)),
            pl.BlockSpec((1, 1, d), lambda i, j: (row_of_tile(i), 0, 0)),
            pl.BlockSpec((1, 1, d), lambda i, j: (row_of_tile(i), 0, 1)),
            pl.BlockSpec((d, tn), lambda i, j: (0, col0 + j)),
        ],
        out_specs=pl.BlockSpec((tm, tn), lambda i, j: (i, j)),
        out_shape=jax.ShapeDtypeStruct((t, n), F32),
        scratch_shapes=[pltpu.VMEM((tm, d), BF16)],
        compiler_params=pltpu.CompilerParams(
            dimension_semantics=("arbitrary", "arbitrary"), vmem_limit_bytes=_vmem_limit(52 << 20)),
        name=name,
    )(x2, g, mod3, mod3, w_bf)


def _attn_kernel(q_ref, k_ref, v_ref, kc_ref, vc_ref, cos_ref, sin_ref, bias_ref, g_ref, o_ref,
                 qs_ref, ks_ref, vs_ref, kcs_ref, vcs_ref, *, block_type):
    cos = cos_ref[...]
    sin = sin_ref[...]

    def rope(x):
        return x * cos + pltpu.roll(x, HEAD_DIM // 2, 1) * sin

    qs_ref[...] = rope(q_ref[...]).astype(BF16)
    ks_ref[...] = rope(k_ref[...]).astype(BF16)
    vs_ref[...] = v_ref[...].astype(BF16)
    kcs_ref[...] = kc_ref[...].astype(BF16)
    vcs_ref[...] = vc_ref[...].astype(BF16)

    scale = HEAD_DIM ** -0.5
    qb = Q_ROWS * GRID_W
    kb = K_ROWS * GRID_W
    n_blocks = q_ref.shape[0] // qb
    max_start = q_ref.shape[0] // GRID_W - K_ROWS

    def block(j):
        q0 = pl.multiple_of(j * qb, qb)
        k_row = jnp.clip(j * Q_ROWS - NA_ROWS // 2, 0, max_start)
        k0 = pl.multiple_of(k_row * GRID_W, qb)
        q = qs_ref[pl.ds(q0, qb), :]
        btype = 0
        for jj, tt in enumerate(block_type):
            btype = jnp.where(j == jj, tt, btype)
        s_loc = _dot_nt(q, ks_ref[pl.ds(k0, kb), :]) * scale + bias_ref[0, btype]
        s_ctx = _dot_nt(q, kcs_ref[...]) * scale
        m = jnp.maximum(jnp.max(s_loc, axis=-1, keepdims=True), jnp.max(s_ctx, axis=-1, keepdims=True))
        p_loc = jnp.exp(s_loc - m)
        p_ctx = jnp.exp(s_ctx - m)
        denom = jnp.sum(p_loc, axis=-1, keepdims=True) + jnp.sum(p_ctx, axis=-1, keepdims=True)
        o = _dot(p_loc.astype(BF16), vs_ref[pl.ds(k0, kb), :]) + _dot(p_ctx.astype(BF16), vcs_ref[...])
        o = o / denom
        o_ref[pl.ds(q0, qb), :] = (_rms(o) * g_ref[...]).astype(BF16)

    def body(p, carry):
        for u in range(ATTN_BLOCKS_PER_ITER):
            block(p * ATTN_BLOCKS_PER_ITER + u)
        return carry

    lax.fori_loop(0, n_blocks // ATTN_BLOCKS_PER_ITER, body, 0)


def _attn_call(proj, proj_ctx, cos_t, sin_t, bias, g_attn, batch, seq, ctx_len):
    d_attn = N_HEADS * HEAD_DIM
    q_blk = (proj.shape[1] - 3 * d_attn) // HEAD_DIM
    n_types = bias.shape[1]
    _, block_type = _block_row_offsets(seq // GRID_W)
    qb, kb = Q_ROWS * GRID_W, K_ROWS * GRID_W
    return pl.pallas_call(
        functools.partial(_attn_kernel, block_type=block_type),
        grid=(N_HEADS, batch),
        in_specs=[
            pl.BlockSpec((seq, HEAD_DIM), lambda h, b: (b, q_blk + h)),
            pl.BlockSpec((seq, HEAD_DIM), lambda h, b: (b, q_blk + N_HEADS + h)),
            pl.BlockSpec((seq, HEAD_DIM), lambda h, b: (b, q_blk + 2 * N_HEADS + h)),
            pl.BlockSpec((ctx_len, HEAD_DIM), lambda h, b: (b, h)),
            pl.BlockSpec((ctx_len, HEAD_DIM), lambda h, b: (b, N_HEADS + h)),
            pl.BlockSpec((seq, HEAD_DIM), lambda h, b: (0, 0)),
            pl.BlockSpec((seq, HEAD_DIM), lambda h, b: (0, 0)),
            pl.BlockSpec((1, n_types, qb, kb), lambda h, b: (h, 0, 0, 0)),
            pl.BlockSpec((1, HEAD_DIM), lambda h, b: (0, h)),
        ],
        out_specs=pl.BlockSpec((seq, HEAD_DIM), lambda h, b: (b, h)),
        out_shape=jax.ShapeDtypeStruct((batch * seq, d_attn), BF16),
        scratch_shapes=[
            pltpu.VMEM((seq, HEAD_DIM), BF16),
            pltpu.VMEM((seq, HEAD_DIM), BF16),
            pltpu.VMEM((seq, HEAD_DIM), BF16),
            pltpu.VMEM((ctx_len, HEAD_DIM), BF16),
            pltpu.VMEM((ctx_len, HEAD_DIM), BF16),
        ],
        compiler_params=pltpu.CompilerParams(
            dimension_semantics=("arbitrary", "arbitrary"), vmem_limit_bytes=_vmem_limit(48 << 20)),
        name="attn",
    )(proj, proj, proj, proj_ctx, proj_ctx, cos_t, sin_t, bias, g_attn)


def _rope_tables(seq):
    t = jnp.arange(seq, dtype=I32)
    pos = jnp.stack([t // GRID_W, t % GRID_W], axis=-1).astype(F32)
    n_freq = HEAD_DIM // 4
    inv_freq = ROPE_THETA ** (-jnp.arange(n_freq, dtype=F32) / n_freq)
    ang = pos[:, :, None] * inv_freq
    cos, sin = jnp.cos(ang).reshape(seq, 2 * n_freq), jnp.sin(ang).reshape(seq, 2 * n_freq)
    cos_t = jnp.concatenate([cos, cos], axis=-1)
    sin_t = jnp.concatenate([-sin, sin], axis=-1)
    return cos_t, sin_t


def _win_prep_kernel(w_ref, o_ref, *, first_qk_block, n_qk_blocks):
    j = pl.program_id(0)
    is_qk = (j >= first_qk_block) & (j < first_qk_block + n_qk_blocks)

    @pl.when(is_qk)
    def _():
        q = HEAD_DIM // 4
        for c0 in range(0, w_ref.shape[1], HEAD_DIM):
            x = w_ref[:, c0:c0 + HEAD_DIM]
            x = jnp.concatenate([x[:, 0:q], x[:, 2 * q:3 * q], x[:, q:2 * q], x[:, 3 * q:]], axis=1)
            o_ref[:, c0:c0 + HEAD_DIM] = x.astype(BF16)

    @pl.when(jnp.logical_not(is_qk))
    def _():
        o_ref[...] = w_ref[...].astype(BF16)


def _win_prep_call(w_in, n_lead_cols, tn=1024):
    rows, cols = w_in.shape
    qk_cols = 2 * N_HEADS * HEAD_DIM
    assert n_lead_cols % tn == 0 and qk_cols % tn == 0
    return pl.pallas_call(
        functools.partial(_win_prep_kernel, first_qk_block=n_lead_cols // tn, n_qk_blocks=qk_cols // tn),
        grid=(cols // tn,),
        in_specs=[pl.BlockSpec((rows, tn), lambda j: (0, j))],
        out_specs=pl.BlockSpec((rows, tn), lambda j: (0, j)),
        out_shape=jax.ShapeDtypeStruct((rows, cols), BF16),
        compiler_params=pltpu.CompilerParams(
            dimension_semantics=("arbitrary",), vmem_limit_bytes=_vmem_limit(40 << 20)),
        name="w_in_prep",
    )(w_in)


def _block_row_offsets(rows):
    n_blocks = rows // Q_ROWS
    kh = min(NA_ROWS, rows)
    sel = np.zeros((n_blocks, Q_ROWS, K_ROWS), np.int32)
    for j in range(n_blocks):
        k_start = int(np.clip(j * Q_ROWS - NA_ROWS // 2, 0, rows - K_ROWS))
        for a in range(Q_ROWS):
            r = j * Q_ROWS + a
            r0 = int(np.clip(r - kh // 2, 0, rows - kh))
            for b in range(K_ROWS):
                kr = k_start + b
                sel[j, a, b] = kr - r + NA_ROWS - 1 if r0 <= kr < r0 + kh else 2 * NA_ROWS - 1
    uniq, block_type = np.unique(sel.reshape(n_blocks, -1), axis=0, return_inverse=True)
    return uniq.reshape(-1, Q_ROWS, K_ROWS), tuple(int(v) for v in np.ravel(block_type))


def _bias_kernel(rpb_ref, o_ref, *, sel):
    w = GRID_W
    qc = lax.broadcasted_iota(I32, (w, w), 0)
    kc = lax.broadcasted_iota(I32, (w, w), 1)
    c0 = jnp.clip(qc - NA_COLS // 2, 0, w - NA_COLS)
    col_ok = (kc >= c0) & (kc < c0 + NA_COLS)
    blocks = []
    for ro in range(rpb_ref.shape[1]):
        r = jnp.broadcast_to(rpb_ref[0, ro:ro + 1, :], (w, LANES))
        rolled = pltpu.roll(r, LANES - (NA_COLS - 1), 1, stride=1, stride_axis=0)
        blocks.append(jnp.where(col_ok, rolled[:, :w], MASK_VALUE))
    blocks.append(jnp.full((w, w), MASK_VALUE, F32))
    for ty in range(sel.shape[0]):
        for a in range(Q_ROWS):
            o_ref[0, ty, a * w:(a + 1) * w, :] = jnp.concatenate(
                [blocks[int(sel[ty, a, b])] for b in range(K_ROWS)], axis=1)


def _bias_table(rpb, rows):
    n_heads, n_ro, n_co = rpb.shape
    sel, _ = _block_row_offsets(rows)
    rpb_lanes = jnp.pad(rpb, ((0, 0), (0, 0), (0, LANES - n_co)))
    return pl.pallas_call(
        functools.partial(_bias_kernel, sel=sel),
        grid=(n_heads,),
        in_specs=[pl.BlockSpec((1, n_ro, LANES), lambda h: (h, 0, 0))],
        out_specs=pl.BlockSpec((1, sel.shape[0], Q_ROWS * GRID_W, K_ROWS * GRID_W), lambda h: (h, 0, 0, 0)),
        out_shape=jax.ShapeDtypeStruct((n_heads, sel.shape[0], Q_ROWS * GRID_W, K_ROWS * GRID_W), F32),
        compiler_params=pltpu.CompilerParams(dimension_semantics=("arbitrary",)),
        name="bias_table",
    )(rpb_lanes)


def _conv_kernel(ub_ref, uc_ref, ux_ref, w_ref, b_ref, g_ref, o_ref):
    p = uc_ref[...] * ux_ref[...]
    seq = p.shape[0]
    row = lax.broadcasted_iota(I32, p.shape, 0)
    prev = jnp.where(row == 0, 0.0, pltpu.roll(p, 1, 0))
    nxt = jnp.where(row == seq - 1, 0.0, pltpu.roll(p, seq - 1, 0))
    w = w_ref[...]
    conv = prev * w[0:1] + p * w[1:2] + nxt * w[2:3] + b_ref[...]
    y = ub_ref[...] * conv
    g = g_ref[...]
    for c0 in range(0, y.shape[1], CONV_GROUP_DIM):
        sl = slice(c0, c0 + CONV_GROUP_DIM)
        o_ref[:, sl] = (_rms(y[:, sl]) * g[:, sl]).astype(BF16)


def _conv_call(proj, conv_w, conv_b, g_conv, batch, seq, cb=256):
    d_conv = conv_w.shape[1]
    nc = d_conv // cb
    return pl.pallas_call(
        _conv_kernel,
        grid=(batch, nc),
        in_specs=[
            pl.BlockSpec((seq, cb), lambda b, c: (b, c)),
            pl.BlockSpec((seq, cb), lambda b, c: (b, nc + c)),
            pl.BlockSpec((seq, cb), lambda b, c: (b, 2 * nc + c)),
            pl.BlockSpec((conv_w.shape[0], cb), lambda b, c: (0, c)),
            pl.BlockSpec((1, cb), lambda b, c: (0, c)),
            pl.BlockSpec((1, cb), lambda b, c: (0, c)),
        ],
        out_specs=pl.BlockSpec((seq, cb), lambda b, c: (b, c)),
        out_shape=jax.ShapeDtypeStruct((batch * seq, d_conv), BF16),
        compiler_params=pltpu.CompilerParams(
            dimension_semantics=("arbitrary", "arbitrary"), vmem_limit_bytes=_vmem_limit(48 << 20)),
        name="conv",
    )(proj, proj, proj, conv_w, conv_b, g_conv)


def _outproj_kernel(yc_ref, ya_ref, w_ref, x_ref, gpost_ref, gate_ref, gpre_ref, sh_ref, sc_ref, wr_ref,
                    x1_ref, t_ref, lg_ref):
    dc = yc_ref.shape[1]
    y = _dot(yc_ref[...], w_ref[0:dc, :]) + _dot(ya_ref[...], w_ref[dc:, :])
    x1 = x_ref[...] + gate_ref[0] * (_rms(y) * gpost_ref[...])
    x1_ref[...] = x1
    t = (_rms(x1) * gpre_ref[...]) * (1.0 + sc_ref[0]) + sh_ref[0]
    t_ref[...] = t
    ne = wr_ref.shape[0]
    wr = wr_ref[...]
    wr_hi = wr.astype(BF16)
    wr_lo = (wr - wr_hi.astype(F32)).astype(BF16)
    t_hi = t.astype(BF16)
    t_lo = (t - t_hi.astype(F32)).astype(BF16)
    both = _dot_nt(t_hi, jnp.concatenate([wr_hi, wr_lo], axis=0))
    logits = both[:, :ne] + (_dot_nt(t_lo, wr_hi) + both[:, ne:])
    lg_ref[...] = logits.T


def _outproj_call(yc, ya, w_out_bf, x2, g_post, g_pre, mod3, w_router_t, seq, tm=512):
    t, d = x2.shape
    dc = yc.shape[1]
    ne = w_router_t.shape[0]
    tiles_per_seq = seq // tm
    row = lambda i: i // tiles_per_seq
    return pl.pallas_call(
        _outproj_kernel,
        grid=(t // tm,),
        in_specs=[
            pl.BlockSpec((tm, dc), lambda i: (i, 0)),
            pl.BlockSpec((tm, ya.shape[1]), lambda i: (i, 0)),
            pl.BlockSpec(w_out_bf.shape, lambda i: (0, 0), pipeline_mode=pl.Buffered(1)),
            pl.BlockSpec((tm, d), lambda i: (i, 0)),
            pl.BlockSpec((1, d), lambda i: (0, 0)),
            pl.BlockSpec((1, 1, d), lambda i: (row(i), 0, 2)),
            pl.BlockSpec((1, d), lambda i: (0, 0)),
            pl.BlockSpec((1, 1, d), lambda i: (row(i), 0, 3)),
            pl.BlockSpec((1, 1, d), lambda i: (row(i), 0, 4)),
            pl.BlockSpec((ne, d), lambda i: (0, 0)),
        ],
        out_specs=[
            pl.BlockSpec((tm, d), lambda i: (i, 0)),
            pl.BlockSpec((tm, d), lambda i: (i, 0)),
            pl.BlockSpec((ne, tm), lambda i: (0, i)),
        ],
        out_shape=[
            jax.ShapeDtypeStruct((t, d), F32),
            jax.ShapeDtypeStruct((t, d), F32),
            jax.ShapeDtypeStruct((ne, t), F32),
        ],
        compiler_params=pltpu.CompilerParams(
            dimension_semantics=("arbitrary",), vmem_limit_bytes=_vmem_limit(57 << 20)),
        name="out_proj",
    )(yc, ya, w_out_bf, x2, g_post, mod3, g_pre, mod3, mod3, w_router_t)


def _first_argmax_mask(v, axis):
    m = jnp.max(v, axis=axis, keepdims=True)
    idx = lax.broadcasted_iota(I32, v.shape, axis)
    big = v.shape[axis]
    first = jnp.min(jnp.where(v == m, idx, big), axis=axis, keepdims=True)
    return idx == first, m


def _router_kernel(lg_ref, rb_ref, ek_ref, rk_ref, gk_ref, cnt_ref, carry_ref):
    ne, tm = lg_ref.shape
    per = ne // N_EXPERT_GROUPS

    @pl.when(pl.program_id(0) == 0)
    def _():
        carry_ref[...] = jnp.zeros_like(carry_ref)

    scores = jax.nn.sigmoid(lg_ref[...])
    sel = scores + rb_ref[...]
    s3 = sel.reshape(N_EXPERT_GROUPS, per, tm)
    hit1, m1 = _first_argmax_mask(s3, 1)
    m2 = jnp.max(jnp.where(hit1, -jnp.inf, s3), axis=1, keepdims=True)
    grp = (m1 + m2).reshape(N_EXPERT_GROUPS, tm)
    gmask = jnp.zeros(grp.shape, jnp.bool_)
    for _ in range(TOPK_GROUPS):
        hit, _ = _first_argmax_mask(jnp.where(gmask, -jnp.inf, grp), 0)
        gmask = gmask | hit
    emask = jnp.broadcast_to(gmask.reshape(N_EXPERT_GROUPS, 1, tm), s3.shape).reshape(ne, tm)
    cand = jnp.where(emask, sel, -jnp.inf)
    chosen = jnp.zeros(cand.shape, jnp.bool_)
    hits = []
    for _ in range(TOP_K):
        hit, _ = _first_argmax_mask(jnp.where(chosen, -jnp.inf, cand), 0)
        hit = hit & ~chosen
        hits.append(hit)
        chosen = chosen | hit
    wsel = jnp.where(chosen, scores, 0.0)
    gates = wsel / jnp.sum(wsel, axis=0, keepdims=True) * ROUTED_SCALE

    r_i = lax.broadcasted_iota(I32, (tm, tm), 0)
    c_i = lax.broadcasted_iota(I32, (tm, tm), 1)
    tri = jnp.where(r_i <= c_i, 1.0, 0.0).astype(BF16)
    csum = _dot(jnp.where(chosen, 1.0, 0.0).astype(BF16), tri)
    carry = carry_ref[:, 0:1]
    rank = carry + csum - 1.0
    eidx = lax.broadcasted_iota(I32, (ne, tm), 0).astype(F32)

    def pick(hit, v):
        return jnp.sum(jnp.where(hit, v, 0.0), axis=0, keepdims=True)

    ek_ref[...] = jnp.concatenate([pick(h, eidx) for h in hits], axis=0).astype(I32)
    rk_ref[...] = jnp.concatenate([pick(h, rank) for h in hits], axis=0).astype(I32)
    gk_rows = jnp.concatenate([pick(h, gates) for h in hits] + [jnp.zeros((LANES - TOP_K, tm), F32)], axis=0)
    gk_ref[...] = gk_rows.T
    new_carry = carry + csum[:, tm - 1:tm]
    carry_ref[...] = jnp.broadcast_to(new_carry, carry_ref.shape)
    cnt_ref[...] = jnp.broadcast_to(new_carry, cnt_ref.shape).astype(I32)


def _router_call(logits_t, router_bias_col, tm=512):
    ne, t = logits_t.shape
    return pl.pallas_call(
        _router_kernel,
        grid=(t // tm,),
        in_specs=[
            pl.BlockSpec((ne, tm), lambda i: (0, i)),
            pl.BlockSpec((ne, 1), lambda i: (0, 0)),
        ],
        out_specs=[
            pl.BlockSpec((TOP_K, tm), lambda i: (0, i)),
            pl.BlockSpec((TOP_K, tm), lambda i: (0, i)),
            pl.BlockSpec((tm, LANES), lambda i: (i, 0)),
            pl.BlockSpec((ne, LANES), lambda i: (0, 0)),
        ],
        out_shape=[
            jax.ShapeDtypeStruct((TOP_K, t), I32),
            jax.ShapeDtypeStruct((TOP_K, t), I32),
            jax.ShapeDtypeStruct((t, LANES), F32),
            jax.ShapeDtypeStruct((ne, LANES), I32),
        ],
        scratch_shapes=[pltpu.VMEM((ne, LANES), F32)],
        compiler_params=pltpu.CompilerParams(dimension_semantics=("arbitrary",)),
        name="router",
    )(logits_t, router_bias_col)


def _padded_rows_kernel(offs_ref, ek_ref, rk_ref, o_ref):
    ek = ek_ref[...]
    dest = rk_ref[...]
    for e in range(offs_ref.shape[0]):
        dest = dest + jnp.where(ek == e, offs_ref[e], 0)
    o_ref[...] = dest


def _padded_rows_call(offsets, ek, rk, tm=2048):
    k, t = ek.shape
    assert t % tm == 0
    grid_spec = pltpu.PrefetchScalarGridSpec(
        num_scalar_prefetch=1,
        grid=(t // tm,),
        in_specs=[pl.BlockSpec((k, tm), lambda i, offs: (0, i)), pl.BlockSpec((k, tm), lambda i, offs: (0, i))],
        out_specs=pl.BlockSpec((k, tm), lambda i, offs: (0, i)),
    )
    return pl.pallas_call(
        _padded_rows_kernel,
        grid_spec=grid_spec,
        out_shape=jax.ShapeDtypeStruct((k, t), I32),
        compiler_params=pltpu.CompilerParams(dimension_semantics=("arbitrary",)),
        name="padded_rows",
    )(offsets, ek, rk)


def _routing_tables(ek, rk, counts, n_tiles_max, tokens_per_step):
    tg = EXPERT_ROW_TILE
    k, t = ek.shape
    n_exp = counts.shape[0]
    padded = (counts + tg - 1) // tg * tg
    before = np.tril(np.ones((n_exp, n_exp), bool), -1)
    offsets = jnp.sum(jnp.where(before, padded[None, :], 0), axis=1)
    compact_offsets = jnp.sum(jnp.where(before, counts[None, :], 0), axis=1)
    ends = offsets + padded
    dest = _padded_rows_call(offsets.astype(I32), ek, rk)
    tok = jnp.arange(t, dtype=I32)
    tok_sorted = lax.rem(jnp.sort((ek * t + tok).reshape(-1)), t)
    tok_sorted = jnp.concatenate([tok_sorted, jnp.zeros((IDX_WINDOW,), I32)])
    tile_ends = ends // tg
    n_valid = jnp.sum(padded) // tg
    tile_ids = jnp.arange(n_tiles_max, dtype=I32)
    tile_expert = jnp.sum(tile_ids[:, None] >= tile_ends[None, :], axis=1).astype(I32)
    last_expert = jnp.sum(n_valid - 1 >= tile_ends).astype(I32)
    tile_expert = jnp.where(tile_ids < n_valid, tile_expert, last_expert)
    own = tile_expert[:, None] == jnp.arange(n_exp, dtype=I32)[None, :]
    tile_start = tile_ids * tg + jnp.sum(jnp.where(own, (compact_offsets - offsets)[None, :], 0), axis=1)
    tile_start = jnp.where(tile_ids < n_valid, tile_start, 0).astype(I32)
    dest_steps = dest.reshape(k, t // tokens_per_step, tokens_per_step).transpose(1, 0, 2).reshape(-1)
    return dest_steps, tok_sorted, tile_expert, n_valid.reshape(1).astype(I32), tile_start


def _experts_kernel(te_ref, nv_ref, ts_ref, tok_hbm, t_hbm, wg_ref, wu_ref, wd_ref, y_ref, *scratch):
    idx_bufs = scratch[:2]
    x_bufs = scratch[2:2 + ROW_BUFFERS]
    wg_s, wu_s, wd_s, sem_idx, sem_row = scratch[2 + ROW_BUFFERS:]
    i = pl.program_id(0)
    nv = nv_ref[0]
    tg = y_ref.shape[0]
    ahead = ROW_LOOKAHEAD

    def tile_of(j):
        return jnp.minimum(j, nv - 1)

    def idx_copy(j, j_static):
        window = pl.multiple_of(lax.shift_left(lax.shift_right_logical(ts_ref[tile_of(j)], 10), 10), IDX_ALIGN)
        slot = j_static % 2
        return pltpu.make_async_copy(tok_hbm.at[pl.ds(window, IDX_WINDOW)], idx_bufs[slot], sem_idx.at[slot])

    def row_copy(j, j_static, r):
        base = ts_ref[tile_of(j)] & (IDX_ALIGN - 1)
        slot = j_static % ROW_BUFFERS
        return pltpu.make_async_copy(t_hbm.at[idx_bufs[j_static % 2][base + r]], x_bufs[slot].at[r],
                                     sem_row.at[slot])

    def wait_rows(slot):
        pltpu.make_async_copy(t_hbm.at[pl.ds(0, tg)], x_bufs[slot], sem_row.at[slot]).wait()

    def step(phase):
        if phase == 0:
            @pl.when(i == 0)
            def _():
                for j in range(ahead):
                    idx_copy(j, j).start()
                    idx_copy(j, j).wait()

                    def first_rows(r, c, j=j):
                        row_copy(j, j, r).start()
                        return c

                    lax.fori_loop(0, tg, first_rows, 0)
                idx_copy(ahead, ahead).start()

        idx_copy(i + ahead, phase + ahead).wait()
        idx_copy(i + ahead + 1, phase + ahead + 1).start()

        @pl.when((i == 0) | (te_ref[i] != te_ref[jnp.maximum(i - 1, 0)]))
        def _():
            wg_s[...] = wg_ref[0].astype(BF16)
            wu_s[...] = wu_ref[0].astype(BF16)
            wd_s[...] = wd_ref[0].astype(BF16)

        wait_rows(phase)
        x = x_bufs[phase][...].astype(BF16)
        hid = _silu(_dot(x, wg_s[...])) * _dot(x, wu_s[...])
        y_ref[...] = _dot(hid.astype(BF16), wd_s[...])
        for r in range(tg):
            row_copy(i + ahead, phase + ahead, r).start()

        @pl.when(i == nv - 1)
        def _():
            for other in range(1, ROW_BUFFERS):
                wait_rows((phase + other) % ROW_BUFFERS)
            idx_copy(i + ahead + 1, phase + ahead + 1).wait()

    for phase in range(ROW_BUFFERS):
        @pl.when((i < nv) & ((i & (ROW_BUFFERS - 1)) == phase))
        def _(phase=phase):
            step(phase)

    @pl.when(i >= nv)
    def _():
        y_ref[...] = jnp.zeros_like(y_ref)


def _experts_call(tile_expert, n_valid, tile_start, tok_sorted, t_f32, w_gate, w_up, w_down):
    _, d = t_f32.shape
    ne, _, df = w_gate.shape
    tg = EXPERT_ROW_TILE
    n_tiles = tile_expert.shape[0]
    grid_spec = pltpu.PrefetchScalarGridSpec(
        num_scalar_prefetch=3,
        grid=(n_tiles,),
        in_specs=[
            pl.BlockSpec(memory_space=pl.ANY),
            pl.BlockSpec(memory_space=pl.ANY),
            pl.BlockSpec((1, d, df), lambda i, te, nv, ts: (te[i], 0, 0)),
            pl.BlockSpec((1, d, df), lambda i, te, nv, ts: (te[i], 0, 0)),
            pl.BlockSpec((1, df, d), lambda i, te, nv, ts: (te[i], 0, 0)),
        ],
        out_specs=pl.BlockSpec((tg, d), lambda i, te, nv, ts: (i, 0)),
        scratch_shapes=(
            [pltpu.SMEM((IDX_WINDOW,), I32)] * 2
            + [pltpu.VMEM((tg, d), F32)] * ROW_BUFFERS
            + [pltpu.VMEM((d, df), BF16), pltpu.VMEM((d, df), BF16), pltpu.VMEM((df, d), BF16),
               pltpu.SemaphoreType.DMA((2,)), pltpu.SemaphoreType.DMA((ROW_BUFFERS,))]),
    )
    return pl.pallas_call(
        _experts_kernel,
        grid_spec=grid_spec,
        out_shape=jax.ShapeDtypeStruct((n_tiles * tg, d), F32),
        compiler_params=pltpu.CompilerParams(
            dimension_semantics=("arbitrary",), vmem_limit_bytes=_vmem_limit(57 << 20)),
        name="experts",
    )(tile_expert, n_valid, tile_start, tok_sorted, t_f32, w_gate, w_up, w_down)


def _combine_kernel(dest_hbm, y_hbm, gk_ref, t_ref, wsg_ref, wsu_ref, wsd_ref, x1_ref, gpost_ref, gate_ref, o_ref,
                    idx0, idx1, rows0, rows1, sem_idx, sem_row, *, tc, n_steps):
    i = pl.program_id(0)
    n_idx = TOP_K * tc
    idx_bufs, row_bufs = (idx0, idx1), (rows0, rows1)

    def idx_copy(j, j_static):
        slot = j_static % 2
        start = pl.multiple_of(jnp.minimum(j, n_steps - 1) * n_idx, n_idx)
        return pltpu.make_async_copy(dest_hbm.at[pl.ds(start, n_idx)], idx_bufs[slot], sem_idx.at[slot])

    def row_copy(j_static, k, t):
        slot = j_static % 2
        return pltpu.make_async_copy(y_hbm.at[idx_bufs[slot][k * tc + t]], row_bufs[slot].at[k, t], sem_row.at[slot])

    def wait_rows(slot):
        for k in range(TOP_K):
            pltpu.make_async_copy(y_hbm.at[pl.ds(0, tc)], row_bufs[slot].at[k], sem_row.at[slot]).wait()

    def step(phase):
        if phase == 0:
            @pl.when(i == 0)
            def _():
                idx_copy(0, 0).start()
                idx_copy(0, 0).wait()

                def first_rows(t, c):
                    for k in range(TOP_K):
                        row_copy(0, k, t).start()
                    return c

                lax.fori_loop(0, tc, first_rows, 0)
                idx_copy(1, 1).start()

        idx_copy(i + 1, phase + 1).wait()
        idx_copy(i + 2, phase + 2).start()
        wait_rows(phase)
        for t in range(tc):
            for k in range(TOP_K):
                row_copy(phase + 1, k, t).start()

        rows_ref = row_bufs[phase]
        t = t_ref[...].astype(BF16)
        hid = _silu(_dot(t, wsg_ref[...])) * _dot(t, wsu_ref[...])
        f = _dot(hid.astype(BF16), wsd_ref[...])
        g = gk_ref[...]
        for k in range(TOP_K):
            f = f + g[:, k:k + 1] * rows_ref[k]
        o_ref[...] = x1_ref[...] + gate_ref[0] * (_rms(f) * gpost_ref[...])

        @pl.when(i == n_steps - 1)
        def _():
            wait_rows(1 - phase)
            idx_copy(i + 2, phase + 2).wait()

    for phase in range(2):
        @pl.when((i & 1) == phase)
        def _(phase=phase):
            step(phase)


def _combine_call(dest_steps, y, gk, t_f32, wsg_bf, wsu_bf, wsd_bf, x1, g_post, mod3, seq):
    t, d = x1.shape
    tc = COMBINE_TOKENS
    tiles_per_seq = seq // tc
    return pl.pallas_call(
        functools.partial(_combine_kernel, tc=tc, n_steps=t // tc),
        grid=(t // tc,),
        in_specs=[
            pl.BlockSpec(memory_space=pl.ANY),
            pl.BlockSpec(memory_space=pl.ANY),
            pl.BlockSpec((tc, LANES), lambda i: (i, 0)),
            pl.BlockSpec((tc, d), lambda i: (i, 0)),
            pl.BlockSpec(wsg_bf.shape, lambda i: (0, 0)),
            pl.BlockSpec(wsu_bf.shape, lambda i: (0, 0)),
            pl.BlockSpec(wsd_bf.shape, lambda i: (0, 0)),
            pl.BlockSpec((tc, d), lambda i: (i, 0)),
            pl.BlockSpec((1, d), lambda i: (0, 0)),
            pl.BlockSpec((1, 1, d), lambda i: (i // tiles_per_seq, 0, 5)),
        ],
        out_specs=pl.BlockSpec((tc, d), lambda i: (i, 0)),
        out_shape=jax.ShapeDtypeStruct((t, d), F32),
        scratch_shapes=[
            pltpu.SMEM((TOP_K * tc,), I32),
            pltpu.SMEM((TOP_K * tc,), I32),
            pltpu.VMEM((TOP_K, tc, d), F32),
            pltpu.VMEM((TOP_K, tc, d), F32),
            pltpu.SemaphoreType.DMA((2,)),
            pltpu.SemaphoreType.DMA((2,)),
        ],
        compiler_params=pltpu.CompilerParams(
            dimension_semantics=("arbitrary",), vmem_limit_bytes=_vmem_limit(48 << 20)),
        name="combine",
    )(dest_steps, y, gk, t_f32, wsg_bf, wsu_bf, wsd_bf, x1, g_post, mod3)


def kernel(x, c, ctx, c_ctx, w_mod, b_mod, g_pre_mix, g_post_mix, g_pre_ffn, g_post_ffn, w_in, conv_w, conv_b,
           rpb, g_conv_out, g_attn_out, w_out, w_router, router_bias, w_exp_gate, w_exp_up, w_exp_down,
           w_sh_gate, w_sh_up, w_sh_down):
    batch, seq, d = x.shape
    ctx_len = ctx.shape[1]
    n_tok = batch * seq
    assert w_mod.shape[0] == 1, "single-layer kernel"
    assert seq % (Q_ROWS * GRID_W) == 0 and seq // GRID_W >= K_ROWS
    assert w_router.shape[2] == N_EXPERTS and seq % COMBINE_TOKENS == 0
    assert EXPERT_ROW_TILE <= IDX_ALIGN and n_tok * TOP_K % EXPERT_ROW_TILE == 0
    d_attn = N_HEADS * HEAD_DIM
    kv_offset = w_in.shape[2] - 2 * d_attn
    row2 = lambda a: a.reshape(1, -1)

    x2 = x.reshape(n_tok, d)
    ctx2 = ctx.reshape(batch * ctx_len, d)

    n_rows = -(-(batch + 1) // 8) * 8
    cvec = jnp.concatenate([c, c_ctx[None], jnp.zeros((n_rows - batch - 1, d), F32)], axis=0)
    mod = _mod_call(cvec, w_mod[0], row2(b_mod[0]))
    mod3 = mod.reshape(n_rows, 1, 6 * d)

    w_in_bf = _win_prep_call(w_in[0], w_in.shape[2] - 3 * d_attn)
    tm_in = 1024
    tiles_per_seq = seq // tm_in
    proj = _inproj_call(x2, row2(g_pre_mix[0]), mod3, w_in_bf, lambda i: i // tiles_per_seq,
                        tm_in, 1024, "in_proj")
    proj_ctx = _inproj_call(ctx2, row2(g_pre_mix[0]), mod3, w_in_bf, lambda i: batch,
                            min(tm_in, ctx2.shape[0]), 1024, "in_proj_ctx", first_col=kv_offset)

    cos_t, sin_t = _rope_tables(seq)
    bias = _bias_table(rpb[0], seq // GRID_W)
    ya = _attn_call(proj, proj_ctx, cos_t, sin_t, bias, row2(g_attn_out[0]), batch, seq, ctx_len)
    yc = _conv_call(proj, conv_w[0], row2(conv_b[0]), row2(g_conv_out[0]), batch, seq)

    x1, t_f32, logits_t = _outproj_call(yc, ya, w_out[0].astype(BF16), x2, row2(g_post_mix[0]),
                                        row2(g_pre_ffn[0]), mod3, w_router[0].T, seq)
    ek, rk, gk, cnt = _router_call(logits_t, router_bias[0].reshape(-1, 1))

    n_tiles_max = n_tok * TOP_K // EXPERT_ROW_TILE + N_EXPERTS
    dest_c, tok_sorted, tile_expert, n_valid, tile_start = _routing_tables(
        ek, rk, cnt[:, 0], n_tiles_max, COMBINE_TOKENS)
    y = _experts_call(tile_expert, n_valid, tile_start, tok_sorted, t_f32,
                      w_exp_gate[0], w_exp_up[0], w_exp_down[0])
    out = _combine_call(dest_c, y, gk, t_f32, w_sh_gate[0].astype(BF16), w_sh_up[0].astype(BF16),
                        w_sh_down[0].astype(BF16), x1, row2(g_post_ffn[0]), mod3, seq)
    return out.reshape(batch, seq, d)
```

```python
import functools

import numpy as np
import jax
import jax.numpy as jnp
from jax import lax
from jax.experimental import pallas as pl
from jax.experimental.pallas import tpu as pltpu

F32 = jnp.float32
BF16 = jnp.bfloat16
I32 = jnp.int32

GRID_W = 64
N_HEADS = 8
HEAD_DIM = 128
NA_ROWS = 8
NA_COLS = 16
ROPE_THETA = 10000.0
CONV_GROUP_DIM = 128
N_EXPERTS = 64
N_EXPERT_GROUPS = 8
TOPK_GROUPS = 4
TOP_K = 8
ROUTED_SCALE = 2.5
EPS = 1e-6
MASK_VALUE = -1e30

Q_ROWS = 4
K_ROWS = 12
ATTN_BLOCKS_PER_ITER = 8

V7X_VMEM_BYTES = 64 * 1024 * 1024
LANES = 128

EXPERT_ROW_TILE = 512
ROW_BUFFERS = 2
ROW_LOOKAHEAD = ROW_BUFFERS - 1
COMBINE_TOKENS = 128
IDX_ALIGN = 1024
IDX_WINDOW = 2 * IDX_ALIGN


def _vmem_limit(nbytes):
    return int(min(nbytes, V7X_VMEM_BYTES - 6 * 1024 * 1024))


def _rms(x):
    return x * lax.rsqrt(jnp.mean(x * x, axis=-1, keepdims=True) + EPS)


def _silu(x):
    return x * jax.nn.sigmoid(x)


def _dot(a, b):
    return jnp.dot(a, b, preferred_element_type=F32)


def _dot_nt(a, b, precision=None):
    return lax.dot_general(a, b, (((1,), (1,)), ((), ())), preferred_element_type=F32, precision=precision)


def _mod_kernel(c_ref, w_ref, b_ref, o_ref):
    a = _silu(c_ref[...]).astype(BF16)
    o_ref[...] = _dot(a, w_ref[...].astype(BF16)) + b_ref[...]


def _mod_call(cvec, w_mod, b_mod, tn=1024):
    rows, d = cvec.shape
    n = w_mod.shape[1]
    return pl.pallas_call(
        _mod_kernel,
        grid=(n // tn,),
        in_specs=[
            pl.BlockSpec((rows, d), lambda j: (0, 0)),
            pl.BlockSpec((d, tn), lambda j: (0, j)),
            pl.BlockSpec((1, tn), lambda j: (0, j)),
        ],
        out_specs=pl.BlockSpec((rows, tn), lambda j: (0, j)),
        out_shape=jax.ShapeDtypeStruct((rows, n), F32),
        compiler_params=pltpu.CompilerParams(
            dimension_semantics=("arbitrary",), vmem_limit_bytes=_vmem_limit(40 << 20)),
        name="mod",
    )(cvec, w_mod, b_mod)


def _inproj_kernel(x_ref, g_ref, sh_ref, sc_ref, w_ref, o_ref, h_ref):
    @pl.when(pl.program_id(1) == 0)
    def _():
        h = _rms(x_ref[...]) * g_ref[...]
        h = h * (1.0 + sc_ref[0]) + sh_ref[0]
        h_ref[...] = h.astype(BF16)

    o_ref[...] = _dot(h_ref[...], w_ref[...])


def _inproj_call(x2, g, mod3, w_bf, row_of_tile, tm, tn, name, first_col=0):
    t, d = x2.shape
    n = w_bf.shape[1] - first_col
    col0 = first_col // tn
    assert first_col % tn == 0
    return pl.pallas_call(
        _inproj_kernel,
        grid=(t // tm, n // tn),
        in_specs=[
            pl.BlockSpec((tm, d), lambda i, j: (i, 0)),
            pl.BlockSpec((1, d), lambda i, j: (0, 0)),
            pl.BlockSpec((1, 1, d), lambda i, j: (row_of_tile(i), 0, 0)),
            pl.BlockSpec((1, 1, d), lambda i, j: (row_of_tile(i), 0, 1)),
            pl.BlockSpec((d, tn), lambda i, j: (0, col0 + j)),
        ],
        out_specs=pl.BlockSpec((tm, tn), lambda i, j: (i, j)),
        out_shape=jax.ShapeDtypeStruct((t, n), F32),
        scratch_shapes=[pltpu.VMEM((tm, d), BF16)],
        compiler_params=pltpu.CompilerParams(
            dimension_semantics=("arbitrary", "arbitrary"), vmem_limit_bytes=_vmem_limit(52 << 20)),
        name=name,
    )(x2, g, mod3, mod3, w_bf)


def _attn_kernel(q_ref, k_ref, v_ref, kc_ref, vc_ref, cos_ref, sin_ref, bias_ref, g_ref, o_ref,
                 qs_ref, ks_ref, vs_ref, kcs_ref, vcs_ref, *, block_type):
    cos = cos_ref[...]
    sin = sin_ref[...]

    def rope(x):
        return x * cos + pltpu.roll(x, HEAD_DIM // 2, 1) * sin

    qs_ref[...] = rope(q_ref[...]).astype(BF16)
    ks_ref[...] = rope(k_ref[...]).astype(BF16)
    vs_ref[...] = v_ref[...].astype(BF16)
    kcs_ref[...] = kc_ref[...].astype(BF16)
    vcs_ref[...] = vc_ref[...].astype(BF16)

    scale = HEAD_DIM ** -0.5
    qb = Q_ROWS * GRID_W
    kb = K_ROWS * GRID_W
    n_blocks = q_ref.shape[0] // qb
    max_start = q_ref.shape[0] // GRID_W - K_ROWS

    def block(j):
        q0 = pl.multiple_of(j * qb, qb)
        k_row = jnp.clip(j * Q_ROWS - NA_ROWS // 2, 0, max_start)
        k0 = pl.multiple_of(k_row * GRID_W, qb)
        q = qs_ref[pl.ds(q0, qb), :]
        btype = 0
        for jj, tt in enumerate(block_type):
            btype = jnp.where(j == jj, tt, btype)
        s_loc = _dot_nt(q, ks_ref[pl.ds(k0, kb), :]) * scale + bias_ref[0, btype]
        s_ctx = _dot_nt(q, kcs_ref[...]) * scale
        m = jnp.maximum(jnp.max(s_loc, axis=-1, keepdims=True), jnp.max(s_ctx, axis=-1, keepdims=True))
        p_loc = jnp.exp(s_loc - m)
        p_ctx = jnp.exp(s_ctx - m)
        denom = jnp.sum(p_loc, axis=-1, keepdims=True) + jnp.sum(p_ctx, axis=-1, keepdims=True)
        o = _dot(p_loc.astype(BF16), vs_ref[pl.ds(k0, kb), :]) + _dot(p_ctx.astype(BF16), vcs_ref[...])
        o = o / denom
        o_ref[pl.ds(q0, qb), :] = (_rms(o) * g_ref[...]).astype(BF16)

    def body(p, carry):
        for u in range(ATTN_BLOCKS_PER_ITER):
            block(p * ATTN_BLOCKS_PER_ITER + u)
        return carry

    lax.fori_loop(0, n_blocks // ATTN_BLOCKS_PER_ITER, body, 0)


def _attn_call(proj, proj_ctx, cos_t, sin_t, bias, g_attn, batch, seq, ctx_len):
    d_attn = N_HEADS * HEAD_DIM
    q_blk = (proj.shape[1] - 3 * d_attn) // HEAD_DIM
    n_types = bias.shape[1]
    _, block_type = _block_row_offsets(seq // GRID_W)
    qb, kb = Q_ROWS * GRID_W, K_ROWS * GRID_W
    return pl.pallas_call(
        functools.partial(_attn_kernel, block_type=block_type),
        grid=(N_HEADS, batch),
        in_specs=[
            pl.BlockSpec((seq, HEAD_DIM), lambda h, b: (b, q_blk + h)),
            pl.BlockSpec((seq, HEAD_DIM), lambda h, b: (b, q_blk + N_HEADS + h)),
            pl.BlockSpec((seq, HEAD_DIM), lambda h, b: (b, q_blk + 2 * N_HEADS + h)),
            pl.BlockSpec((ctx_len, HEAD_DIM), lambda h, b: (b, h)),
            pl.BlockSpec((ctx_len, HEAD_DIM), lambda h, b: (b, N_HEADS + h)),
            pl.BlockSpec((seq, HEAD_DIM), lambda h, b: (0, 0)),
            pl.BlockSpec((seq, HEAD_DIM), lambda h, b: (0, 0)),
            pl.BlockSpec((1, n_types, qb, kb), lambda h, b: (h, 0, 0, 0)),
            pl.BlockSpec((1, HEAD_DIM), lambda h, b: (0, h)),
        ],
        out_specs=pl.BlockSpec((seq, HEAD_DIM), lambda h, b: (b, h)),
        out_shape=jax.ShapeDtypeStruct((batch * seq, d_attn), BF16),
        scratch_shapes=[
            pltpu.VMEM((seq, HEAD_DIM), BF16),
            pltpu.VMEM((seq, HEAD_DIM), BF16),
            pltpu.VMEM((seq, HEAD_DIM), BF16),
            pltpu.VMEM((ctx_len, HEAD_DIM), BF16),
            pltpu.VMEM((ctx_len, HEAD_DIM), BF16),
        ],
        compiler_params=pltpu.CompilerParams(
            dimension_semantics=("arbitrary", "arbitrary"), vmem_limit_bytes=_vmem_limit(48 << 20)),
        name="attn",
    )(proj, proj, proj, proj_ctx, proj_ctx, cos_t, sin_t, bias, g_attn)


def _rope_tables(seq):
    t = jnp.arange(seq, dtype=I32)
    pos = jnp.stack([t // GRID_W, t % GRID_W], axis=-1).astype(F32)
    n_freq = HEAD_DIM // 4
    inv_freq = ROPE_THETA ** (-jnp.arange(n_freq, dtype=F32) / n_freq)
    ang = pos[:, :, None] * inv_freq
    cos, sin = jnp.cos(ang).reshape(seq, 2 * n_freq), jnp.sin(ang).reshape(seq, 2 * n_freq)
    cos_t = jnp.concatenate([cos, cos], axis=-1)
    sin_t = jnp.concatenate([-sin, sin], axis=-1)
    return cos_t, sin_t


def _win_prep_kernel(w_ref, o_ref, *, first_qk_block, n_qk_blocks):
    j = pl.program_id(0)
    is_qk = (j >= first_qk_block) & (j < first_qk_block + n_qk_blocks)

    @pl.when(is_qk)
    def _():
        q = HEAD_DIM // 4
        for c0 in range(0, w_ref.shape[1], HEAD_DIM):
            x = w_ref[:, c0:c0 + HEAD_DIM]
            x = jnp.concatenate([x[:, 0:q], x[:, 2 * q:3 * q], x[:, q:2 * q], x[:, 3 * q:]], axis=1)
            o_ref[:, c0:c0 + HEAD_DIM] = x.astype(BF16)

    @pl.when(jnp.logical_not(is_qk))
    def _():
        o_ref[...] = w_ref[...].astype(BF16)


def _win_prep_call(w_in, n_lead_cols, tn=1024):
    rows, cols = w_in.shape
    qk_cols = 2 * N_HEADS * HEAD_DIM
    assert n_lead_cols % tn == 0 and qk_cols % tn == 0
    return pl.pallas_call(
        functools.partial(_win_prep_kernel, first_qk_block=n_lead_cols // tn, n_qk_blocks=qk_cols // tn),
        grid=(cols // tn,),
        in_specs=[pl.BlockSpec((rows, tn), lambda j: (0, j))],
        out_specs=pl.BlockSpec((rows, tn), lambda j: (0, j)),
        out_shape=jax.ShapeDtypeStruct((rows, cols), BF16),
        compiler_params=pltpu.CompilerParams(
            dimension_semantics=("arbitrary",), vmem_limit_bytes=_vmem_limit(40 << 20)),
        name="w_in_prep",
    )(w_in)


def _block_row_offsets(rows):
    n_blocks = rows // Q_ROWS
    kh = min(NA_ROWS, rows)
    sel = np.zeros((n_blocks, Q_ROWS, K_ROWS), np.int32)
    for j in range(n_blocks):
        k_start = int(np.clip(j * Q_ROWS - NA_ROWS // 2, 0, rows - K_ROWS))
        for a in range(Q_ROWS):
            r = j * Q_ROWS + a
            r0 = int(np.clip(r - kh // 2, 0, rows - kh))
            for b in range(K_ROWS):
                kr = k_start + b
                sel[j, a, b] = kr - r + NA_ROWS - 1 if r0 <= kr < r0 + kh else 2 * NA_ROWS - 1
    uniq, block_type = np.unique(sel.reshape(n_blocks, -1), axis=0, return_inverse=True)
    return uniq.reshape(-1, Q_ROWS, K_ROWS), tuple(int(v) for v in np.ravel(block_type))


def _bias_kernel(rpb_ref, o_ref, *, sel):
    w = GRID_W
    qc = lax.broadcasted_iota(I32, (w, w), 0)
    kc = lax.broadcasted_iota(I32, (w, w), 1)
    c0 = jnp.clip(qc - NA_COLS // 2, 0, w - NA_COLS)
    col_ok = (kc >= c0) & (kc < c0 + NA_COLS)
    blocks = []
    for ro in range(rpb_ref.shape[1]):
        r = jnp.broadcast_to(rpb_ref[0, ro:ro + 1, :], (w, LANES))
        rolled = pltpu.roll(r, LANES - (NA_COLS - 1), 1, stride=1, stride_axis=0)
        blocks.append(jnp.where(col_ok, rolled[:, :w], MASK_VALUE))
    blocks.append(jnp.full((w, w), MASK_VALUE, F32))
    for ty in range(sel.shape[0]):
        for a in range(Q_ROWS):
            o_ref[0, ty, a * w:(a + 1) * w, :] = jnp.concatenate(
                [blocks[int(sel[ty, a, b])] for b in range(K_ROWS)], axis=1)


def _bias_table(rpb, rows):
    n_heads, n_ro, n_co = rpb.shape
    sel, _ = _block_row_offsets(rows)
    rpb_lanes = jnp.pad(rpb, ((0, 0), (0, 0), (0, LANES - n_co)))
    return pl.pallas_call(
        functools.partial(_bias_kernel, sel=sel),
        grid=(n_heads,),
        in_specs=[pl.BlockSpec((1, n_ro, LANES), lambda h: (h, 0, 0))],
        out_specs=pl.BlockSpec((1, sel.shape[0], Q_ROWS * GRID_W, K_ROWS * GRID_W), lambda h: (h, 0, 0, 0)),
        out_shape=jax.ShapeDtypeStruct((n_heads, sel.shape[0], Q_ROWS * GRID_W, K_ROWS * GRID_W), F32),
        compiler_params=pltpu.CompilerParams(dimension_semantics=("arbitrary",)),
        name="bias_table",
    )(rpb_lanes)


def _conv_kernel(ub_ref, uc_ref, ux_ref, w_ref, b_ref, g_ref, o_ref):
    p = uc_ref[...] * ux_ref[...]
    seq = p.shape[0]
    row = lax.broadcasted_iota(I32, p.shape, 0)
    prev = jnp.where(row == 0, 0.0, pltpu.roll(p, 1, 0))
    nxt = jnp.where(row == seq - 1, 0.0, pltpu.roll(p, seq - 1, 0))
    w = w_ref[...]
    conv = prev * w[0:1] + p * w[1:2] + nxt * w[2:3] + b_ref[...]
    y = ub_ref[...] * conv
    g = g_ref[...]
    for c0 in range(0, y.shape[1], CONV_GROUP_DIM):
        sl = slice(c0, c0 + CONV_GROUP_DIM)
        o_ref[:, sl] = (_rms(y[:, sl]) * g[:, sl]).astype(BF16)


def _conv_call(proj, conv_w, conv_b, g_conv, batch, seq, cb=256):
    d_conv = conv_w.shape[1]
    nc = d_conv // cb
    return pl.pallas_call(
        _conv_kernel,
        grid=(batch, nc),
        in_specs=[
            pl.BlockSpec((seq, cb), lambda b, c: (b, c)),
            pl.BlockSpec((seq, cb), lambda b, c: (b, nc + c)),
            pl.BlockSpec((seq, cb), lambda b, c: (b, 2 * nc + c)),
            pl.BlockSpec((conv_w.shape[0], cb), lambda b, c: (0, c)),
            pl.BlockSpec((1, cb), lambda b, c: (0, c)),
            pl.BlockSpec((1, cb), lambda b, c: (0, c)),
        ],
        out_specs=pl.BlockSpec((seq, cb), lambda b, c: (b, c)),
        out_shape=jax.ShapeDtypeStruct((batch * seq, d_conv), BF16),
        compiler_params=pltpu.CompilerParams(
            dimension_semantics=("arbitrary", "arbitrary"), vmem_limit_bytes=_vmem_limit(48 << 20)),
        name="conv",
    )(proj, proj, proj, conv_w, conv_b, g_conv)


def _outproj_kernel(yc_ref, ya_ref, w_ref, x_ref, gpost_ref, gate_ref, gpre_ref, sh_ref, sc_ref, wr_ref,
                    x1_ref, t_ref, lg_ref):
    dc = yc_ref.shape[1]
    y = _dot(yc_ref[...], w_ref[0:dc, :]) + _dot(ya_ref[...], w_ref[dc:, :])
    x1 = x_ref[...] + gate_ref[0] * (_rms(y) * gpost_ref[...])
    x1_ref[...] = x1
    t = (_rms(x1) * gpre_ref[...]) * (1.0 + sc_ref[0]) + sh_ref[0]
    t_ref[...] = t
    ne = wr_ref.shape[0]
    wr = wr_ref[...]
    wr_hi = wr.astype(BF16)
    wr_lo = (wr - wr_hi.astype(F32)).astype(BF16)
    t_hi = t.astype(BF16)
    t_lo = (t - t_hi.astype(F32)).astype(BF16)
    both = _dot_nt(t_hi, jnp.concatenate([wr_hi, wr_lo], axis=0))
    logits = both[:, :ne] + (_dot_nt(t_lo, wr_hi) + both[:, ne:])
    lg_ref[...] = logits.T


def _outproj_call(yc, ya, w_out_bf, x2, g_post, g_pre, mod3, w_router_t, seq, tm=512):
    t, d = x2.shape
    dc = yc.shape[1]
    ne = w_router_t.shape[0]
    tiles_per_seq = seq // tm
    row = lambda i: i // tiles_per_seq
    return pl.pallas_call(
        _outproj_kernel,
        grid=(t // tm,),
        in_specs=[
            pl.BlockSpec((tm, dc), lambda i: (i, 0)),
            pl.BlockSpec((tm, ya.shape[1]), lambda i: (i, 0)),
            pl.BlockSpec(w_out_bf.shape, lambda i: (0, 0), pipeline_mode=pl.Buffered(1)),
            pl.BlockSpec((tm, d), lambda i: (i, 0)),
            pl.BlockSpec((1, d), lambda i: (0, 0)),
            pl.BlockSpec((1, 1, d), lambda i: (row(i), 0, 2)),
            pl.BlockSpec((1, d), lambda i: (0, 0)),
            pl.BlockSpec((1, 1, d), lambda i: (row(i), 0, 3)),
            pl.BlockSpec((1, 1, d), lambda i: (row(i), 0, 4)),
            pl.BlockSpec((ne, d), lambda i: (0, 0)),
        ],
        out_specs=[
            pl.BlockSpec((tm, d), lambda i: (i, 0)),
            pl.BlockSpec((tm, d), lambda i: (i, 0)),
            pl.BlockSpec((ne, tm), lambda i: (0, i)),
        ],
        out_shape=[
            jax.ShapeDtypeStruct((t, d), F32),
            jax.ShapeDtypeStruct((t, d), F32),
            jax.ShapeDtypeStruct((ne, t), F32),
        ],
        compiler_params=pltpu.CompilerParams(
            dimension_semantics=("arbitrary",), vmem_limit_bytes=_vmem_limit(57 << 20)),
        name="out_proj",
    )(yc, ya, w_out_bf, x2, g_post, mod3, g_pre, mod3, mod3, w_router_t)


def _first_argmax_mask(v, axis):
    m = jnp.max(v, axis=axis, keepdims=True)
    idx = lax.broadcasted_iota(I32, v.shape, axis)
    big = v.shape[axis]
    first = jnp.min(jnp.where(v == m, idx, big), axis=axis, keepdims=True)
    return idx == first, m


def _router_kernel(lg_ref, rb_ref, ek_ref, rk_ref, gk_ref, cnt_ref, carry_ref):
    ne, tm = lg_ref.shape
    per = ne // N_EXPERT_GROUPS

    @pl.when(pl.program_id(0) == 0)
    def _():
        carry_ref[...] = jnp.zeros_like(carry_ref)

    scores = jax.nn.sigmoid(lg_ref[...])
    sel = scores + rb_ref[...]
    s3 = sel.reshape(N_EXPERT_GROUPS, per, tm)
    hit1, m1 = _first_argmax_mask(s3, 1)
    m2 = jnp.max(jnp.where(hit1, -jnp.inf, s3), axis=1, keepdims=True)
    grp = (m1 + m2).reshape(N_EXPERT_GROUPS, tm)
    gmask = jnp.zeros(grp.shape, jnp.bool_)
    for _ in range(TOPK_GROUPS):
        hit, _ = _first_argmax_mask(jnp.where(gmask, -jnp.inf, grp), 0)
        gmask = gmask | hit
    emask = jnp.broadcast_to(gmask.reshape(N_EXPERT_GROUPS, 1, tm), s3.shape).reshape(ne, tm)
    cand = jnp.where(emask, sel, -jnp.inf)
    chosen = jnp.zeros(cand.shape, jnp.bool_)
    hits = []
    for _ in range(TOP_K):
        hit, _ = _first_argmax_mask(jnp.where(chosen, -jnp.inf, cand), 0)
        hit = hit & ~chosen
        hits.append(hit)
        chosen = chosen | hit
    wsel = jnp.where(chosen, scores, 0.0)
    gates = wsel / jnp.sum(wsel, axis=0, keepdims=True) * ROUTED_SCALE

    r_i = lax.broadcasted_iota(I32, (tm, tm), 0)
    c_i = lax.broadcasted_iota(I32, (tm, tm), 1)
    tri = jnp.where(r_i <= c_i, 1.0, 0.0).astype(BF16)
    csum = _dot(jnp.where(chosen, 1.0, 0.0).astype(BF16), tri)
    carry = carry_ref[:, 0:1]
    rank = carry + csum - 1.0
    eidx = lax.broadcasted_iota(I32, (ne, tm), 0).astype(F32)

    def pick(hit, v):
        return jnp.sum(jnp.where(hit, v, 0.0), axis=0, keepdims=True)

    ek_ref[...] = jnp.concatenate([pick(h, eidx) for h in hits], axis=0).astype(I32)
    rk_ref[...] = jnp.concatenate([pick(h, rank) for h in hits], axis=0).astype(I32)
    gk_rows = jnp.concatenate([pick(h, gates) for h in hits] + [jnp.zeros((LANES - TOP_K, tm), F32)], axis=0)
    gk_ref[...] = gk_rows.T
    new_carry = carry + csum[:, tm - 1:tm]
    carry_ref[...] = jnp.broadcast_to(new_carry, carry_ref.shape)
    cnt_ref[...] = jnp.broadcast_to(new_carry, cnt_ref.shape).astype(I32)


def _router_call(logits_t, router_bias_col, tm=512):
    ne, t = logits_t.shape
    return pl.pallas_call(
        _router_kernel,
        grid=(t // tm,),
        in_specs=[
            pl.BlockSpec((ne, tm), lambda i: (0, i)),
            pl.BlockSpec((ne, 1), lambda i: (0, 0)),
        ],
        out_specs=[
            pl.BlockSpec((TOP_K, tm), lambda i: (0, i)),
            pl.BlockSpec((TOP_K, tm), lambda i: (0, i)),
            pl.BlockSpec((tm, LANES), lambda i: (i, 0)),
            pl.BlockSpec((ne, LANES), lambda i: (0, 0)),
        ],
        out_shape=[
            jax.ShapeDtypeStruct((TOP_K, t), I32),
            jax.ShapeDtypeStruct((TOP_K, t), I32),
            jax.ShapeDtypeStruct((t, LANES), F32),
            jax.ShapeDtypeStruct((ne, LANES), I32),
        ],
        scratch_shapes=[pltpu.VMEM((ne, LANES), F32)],
        compiler_params=pltpu.CompilerParams(dimension_semantics=("arbitrary",)),
        name="router",
    )(logits_t, router_bias_col)


def _padded_rows_kernel(offs_ref, ek_ref, rk_ref, o_ref):
    ek = ek_ref[...]
    dest = rk_ref[...]
    for e in range(offs_ref.shape[0]):
        dest = dest + jnp.where(ek == e, offs_ref[e], 0)
    o_ref[...] = dest


def _padded_rows_call(offsets, ek, rk, tm=2048):
    k, t = ek.shape
    assert t % tm == 0
    grid_spec = pltpu.PrefetchScalarGridSpec(
        num_scalar_prefetch=1,
        grid=(t // tm,),
        in_specs=[pl.BlockSpec((k, tm), lambda i, offs: (0, i)), pl.BlockSpec((k, tm), lambda i, offs: (0, i))],
        out_specs=pl.BlockSpec((k, tm), lambda i, offs: (0, i)),
    )
    return pl.pallas_call(
        _padded_rows_kernel,
        grid_spec=grid_spec,
        out_shape=jax.ShapeDtypeStruct((k, t), I32),
        compiler_params=pltpu.CompilerParams(dimension_semantics=("arbitrary",)),
        name="padded_rows",
    )(offsets, ek, rk)


def _routing_tables(ek, rk, counts, n_tiles_max, tokens_per_step):
    tg = EXPERT_ROW_TILE
    k, t = ek.shape
    n_exp = counts.shape[0]
    padded = (counts + tg - 1) // tg * tg
    before = np.tril(np.ones((n_exp, n_exp), bool), -1)
    offsets = jnp.sum(jnp.where(before, padded[None, :], 0), axis=1)
    compact_offsets = jnp.sum(jnp.where(before, counts[None, :], 0), axis=1)
    ends = offsets + padded
    dest = _padded_rows_call(offsets.astype(I32), ek, rk)
    tok = jnp.arange(t, dtype=I32)
    tok_sorted = lax.rem(jnp.sort((ek * t + tok).reshape(-1)), t)
    tok_sorted = jnp.concatenate([tok_sorted, jnp.zeros((IDX_WINDOW,), I32)])
    tile_ends = ends // tg
    n_valid = jnp.sum(padded) // tg
    tile_ids = jnp.arange(n_tiles_max, dtype=I32)
    tile_expert = jnp.sum(tile_ids[:, None] >= tile_ends[None, :], axis=1).astype(I32)
    last_expert = jnp.sum(n_valid - 1 >= tile_ends).astype(I32)
    tile_expert = jnp.where(tile_ids < n_valid, tile_expert, last_expert)
    own = tile_expert[:, None] == jnp.arange(n_exp, dtype=I32)[None, :]
    tile_start = tile_ids * tg + jnp.sum(jnp.where(own, (compact_offsets - offsets)[None, :], 0), axis=1)
    tile_start = jnp.where(tile_ids < n_valid, tile_start, 0).astype(I32)
    dest_steps = dest.reshape(k, t // tokens_per_step, tokens_per_step).transpose(1, 0, 2).reshape(-1)
    return dest_steps, tok_sorted, tile_expert, n_valid.reshape(1).astype(I32), tile_start


def _experts_kernel(te_ref, nv_ref, ts_ref, tok_hbm, t_hbm, wg_ref, wu_ref, wd_ref, y_ref, *scratch):
    idx_bufs = scratch[:2]
    x_bufs = scratch[2:2 + ROW_BUFFERS]
    wg_s, wu_s, wd_s, sem_idx, sem_row = scratch[2 + ROW_BUFFERS:]
    i = pl.program_id(0)
    nv = nv_ref[0]
    tg = y_ref.shape[0]
    ahead = ROW_LOOKAHEAD

    def tile_of(j):
        return jnp.minimum(j, nv - 1)

    def idx_copy(j, j_static):
        window = pl.multiple_of(lax.shift_left(lax.shift_right_logical(ts_ref[tile_of(j)], 10), 10), IDX_ALIGN)
        slot = j_static % 2
        return pltpu.make_async_copy(tok_hbm.at[pl.ds(window, IDX_WINDOW)], idx_bufs[slot], sem_idx.at[slot])

    def row_copy(j, j_static, r):
        base = ts_ref[tile_of(j)] & (IDX_ALIGN - 1)
        slot = j_static % ROW_BUFFERS
        return pltpu.make_async_copy(t_hbm.at[idx_bufs[j_static % 2][base + r]], x_bufs[slot].at[r],
                                     sem_row.at[slot])

    def wait_rows(slot):
        pltpu.make_async_copy(t_hbm.at[pl.ds(0, tg)], x_bufs[slot], sem_row.at[slot]).wait()

    def step(phase):
        if phase == 0:
            @pl.when(i == 0)
            def _():
                for j in range(ahead):
                    idx_copy(j, j).start()
                    idx_copy(j, j).wait()

                    def first_rows(r, c, j=j):
                        row_copy(j, j, r).start()
                        return c

                    lax.fori_loop(0, tg, first_rows, 0)
                idx_copy(ahead, ahead).start()

        idx_copy(i + ahead, phase + ahead).wait()
        idx_copy(i + ahead + 1, phase + ahead + 1).start()

        @pl.when((i == 0) | (te_ref[i] != te_ref[jnp.maximum(i - 1, 0)]))
        def _():
            wg_s[...] = wg_ref[0].astype(BF16)
            wu_s[...] = wu_ref[0].astype(BF16)
            wd_s[...] = wd_ref[0].astype(BF16)

        wait_rows(phase)
        x = x_bufs[phase][...].astype(BF16)
        hid = _silu(_dot(x, wg_s[...])) * _dot(x, wu_s[...])
        y_ref[...] = _dot(hid.astype(BF16), wd_s[...])
        for r in range(tg):
            row_copy(i + ahead, phase + ahead, r).start()

        @pl.when(i == nv - 1)
        def _():
            for other in range(1, ROW_BUFFERS):
                wait_rows((phase + other) % ROW_BUFFERS)
            idx_copy(i + ahead + 1, phase + ahead + 1).wait()

    for phase in range(ROW_BUFFERS):
        @pl.when((i < nv) & ((i & (ROW_BUFFERS - 1)) == phase))
        def _(phase=phase):
            step(phase)

    @pl.when(i >= nv)
    def _():
        y_ref[...] = jnp.zeros_like(y_ref)


def _experts_call(tile_expert, n_valid, tile_start, tok_sorted, t_f32, w_gate, w_up, w_down):
    _, d = t_f32.shape
    ne, _, df = w_gate.shape
    tg = EXPERT_ROW_TILE
    n_tiles = tile_expert.shape[0]
    grid_spec = pltpu.PrefetchScalarGridSpec(
        num_scalar_prefetch=3,
        grid=(n_tiles,),
        in_specs=[
            pl.BlockSpec(memory_space=pl.ANY),
            pl.BlockSpec(memory_space=pl.ANY),
            pl.BlockSpec((1, d, df), lambda i, te, nv, ts: (te[i], 0, 0)),
            pl.BlockSpec((1, d, df), lambda i, te, nv, ts: (te[i], 0, 0)),
            pl.BlockSpec((1, df, d), lambda i, te, nv, ts: (te[i], 0, 0)),
        ],
        out_specs=pl.BlockSpec((tg, d), lambda i, te, nv, ts: (i, 0)),
        scratch_shapes=(
            [pltpu.SMEM((IDX_WINDOW,), I32)] * 2
            + [pltpu.VMEM((tg, d), F32)] * ROW_BUFFERS
            + [pltpu.VMEM((d, df), BF16), pltpu.VMEM((d, df), BF16), pltpu.VMEM((df, d), BF16),
               pltpu.SemaphoreType.DMA((2,)), pltpu.SemaphoreType.DMA((ROW_BUFFERS,))]),
    )
    return pl.pallas_call(
        _experts_kernel,
        grid_spec=grid_spec,
        out_shape=jax.ShapeDtypeStruct((n_tiles * tg, d), F32),
        compiler_params=pltpu.CompilerParams(
            dimension_semantics=("arbitrary",), vmem_limit_bytes=_vmem_limit(57 << 20)),
        name="experts",
    )(tile_expert, n_valid, tile_start, tok_sorted, t_f32, w_gate, w_up, w_down)


def _combine_kernel(dest_hbm, y_hbm, gk_ref, t_ref, wsg_ref, wsu_ref, wsd_ref, x1_ref, gpost_ref, gate_ref, o_ref,
                    idx0, idx1, rows0, rows1, sem_idx, sem_row, *, tc, n_steps):
    i = pl.program_id(0)
    n_idx = TOP_K * tc
    idx_bufs, row_bufs = (idx0, idx1), (rows0, rows1)

    def idx_copy(j, j_static):
        slot = j_static % 2
        start = pl.multiple_of(jnp.minimum(j, n_steps - 1) * n_idx, n_idx)
        return pltpu.make_async_copy(dest_hbm.at[pl.ds(start, n_idx)], idx_bufs[slot], sem_idx.at[slot])

    def row_copy(j_static, k, t):
        slot = j_static % 2
        return pltpu.make_async_copy(y_hbm.at[idx_bufs[slot][k * tc + t]], row_bufs[slot].at[k, t], sem_row.at[slot])

    def wait_rows(slot):
        for k in range(TOP_K):
            pltpu.make_async_copy(y_hbm.at[pl.ds(0, tc)], row_bufs[slot].at[k], sem_row.at[slot]).wait()

    def step(phase):
        if phase == 0:
            @pl.when(i == 0)
            def _():
                idx_copy(0, 0).start()
                idx_copy(0, 0).wait()

                def first_rows(t, c):
                    for k in range(TOP_K):
                        row_copy(0, k, t).start()
                    return c

                lax.fori_loop(0, tc, first_rows, 0)
                idx_copy(1, 1).start()

        idx_copy(i + 1, phase + 1).wait()
        idx_copy(i + 2, phase + 2).start()
        wait_rows(phase)
        for t in range(tc):
            for k in range(TOP_K):
                row_copy(phase + 1, k, t).start()

        rows_ref = row_bufs[phase]
        t = t_ref[...].astype(BF16)
        hid = _silu(_dot(t, wsg_ref[...])) * _dot(t, wsu_ref[...])
        f = _dot(hid.astype(BF16), wsd_ref[...])
        g = gk_ref[...]
        for k in range(TOP_K):
            f = f + g[:, k:k + 1] * rows_ref[k]
        o_ref[...] = x1_ref[...] + gate_ref[0] * (_rms(f) * gpost_ref[...])

        @pl.when(i == n_steps - 1)
        def _():
            wait_rows(1 - phase)
            idx_copy(i + 2, phase + 2).wait()

    for phase in range(2):
        @pl.when((i & 1) == phase)
        def _(phase=phase):
            step(phase)


def _combine_call(dest_steps, y, gk, t_f32, wsg_bf, wsu_bf, wsd_bf, x1, g_post, mod3, seq):
    t, d = x1.shape
    tc = COMBINE_TOKENS
    tiles_per_seq = seq // tc
    return pl.pallas_call(
        functools.partial(_combine_kernel, tc=tc, n_steps=t // tc),
        grid=(t // tc,),
        in_specs=[
            pl.BlockSpec(memory_space=pl.ANY),
            pl.BlockSpec(memory_space=pl.ANY),
            pl.BlockSpec((tc, LANES), lambda i: (i, 0)),
            pl.BlockSpec((tc, d), lambda i: (i, 0)),
            pl.BlockSpec(wsg_bf.shape, lambda i: (0, 0)),
            pl.BlockSpec(wsu_bf.shape, lambda i: (0, 0)),
            pl.BlockSpec(wsd_bf.shape, lambda i: (0, 0)),
            pl.BlockSpec((tc, d), lambda i: (i, 0)),
            pl.BlockSpec((1, d), lambda i: (0, 0)),
            pl.BlockSpec((1, 1, d), lambda i: (i // tiles_per_seq, 0, 5)),
        ],
        out_specs=pl.BlockSpec((tc, d), lambda i: (i, 0)),
        out_shape=jax.ShapeDtypeStruct((t, d), F32),
        scratch_shapes=[
            pltpu.SMEM((TOP_K * tc,), I32),
            pltpu.SMEM((TOP_K * tc,), I32),
            pltpu.VMEM((TOP_K, tc, d), F32),
            pltpu.VMEM((TOP_K, tc, d), F32),
            pltpu.SemaphoreType.DMA((2,)),
            pltpu.SemaphoreType.DMA((2,)),
        ],
        compiler_params=pltpu.CompilerParams(
            dimension_semantics=("arbitrary",), vmem_limit_bytes=_vmem_limit(48 << 20)),
        name="combine",
    )(dest_steps, y, gk, t_f32, wsg_bf, wsu_bf, wsd_bf, x1, g_post, mod3)


def kernel(x, c, ctx, c_ctx, w_mod, b_mod, g_pre_mix, g_post_mix, g_pre_ffn, g_post_ffn, w_in, conv_w, conv_b,
           rpb, g_conv_out, g_attn_out, w_out, w_router, router_bias, w_exp_gate, w_exp_up, w_exp_down,
           w_sh_gate, w_sh_up, w_sh_down):
    batch, seq, d = x.shape
    ctx_len = ctx.shape[1]
    n_tok = batch * seq
    assert w_mod.shape[0] == 1, "single-layer kernel"
    assert seq % (Q_ROWS * GRID_W) == 0 and seq // GRID_W >= K_ROWS
    assert w_router.shape[2] == N_EXPERTS and seq % COMBINE_TOKENS == 0
    assert EXPERT_ROW_TILE <= IDX_ALIGN and n_tok * TOP_K % EXPERT_ROW_TILE == 0
    d_attn = N_HEADS * HEAD_DIM
    kv_offset = w_in.shape[2] - 2 * d_attn
    row2 = lambda a: a.reshape(1, -1)

    x2 = x.reshape(n_tok, d)
    ctx2 = ctx.reshape(batch * ctx_len, d)

    n_rows = -(-(batch + 1) // 8) * 8
    cvec = jnp.concatenate([c, c_ctx[None], jnp.zeros((n_rows - batch - 1, d), F32)], axis=0)
    mod = _mod_call(cvec, w_mod[0], row2(b_mod[0]))
    mod3 = mod.reshape(n_rows, 1, 6 * d)

    w_in_bf = _win_prep_call(w_in[0], w_in.shape[2] - 3 * d_attn)
    tm_in = 1024
    tiles_per_seq = seq // tm_in
    proj = _inproj_call(x2, row2(g_pre_mix[0]), mod3, w_in_bf, lambda i: i // tiles_per_seq,
                        tm_in, 1024, "in_proj")
    proj_ctx = _inproj_call(ctx2, row2(g_pre_mix[0]), mod3, w_in_bf, lambda i: batch,
                            min(tm_in, ctx2.shape[0]), 1024, "in_proj_ctx", first_col=kv_offset)

    cos_t, sin_t = _rope_tables(seq)
    bias = _bias_table(rpb[0], seq // GRID_W)
    ya = _attn_call(proj, proj_ctx, cos_t, sin_t, bias, row2(g_attn_out[0]), batch, seq, ctx_len)
    yc = _conv_call(proj, conv_w[0], row2(conv_b[0]), row2(g_conv_out[0]), batch, seq)

    x1, t_f32, logits_t = _outproj_call(yc, ya, w_out[0].astype(BF16), x2, row2(g_post_mix[0]),
                                        row2(g_pre_ffn[0]), mod3, w_router[0].T, seq)
    ek, rk, gk, cnt = _router_call(logits_t, router_bias[0].reshape(-1, 1))

    n_tiles_max = n_tok * TOP_K // EXPERT_ROW_TILE + N_EXPERTS
    dest_c, tok_sorted, tile_expert, n_valid, tile_start = _routing_tables(
        ek, rk, cnt[:, 0], n_tiles_max, COMBINE_TOKENS)
    y = _experts_call(tile_expert, n_valid, tile_start, tok_sorted, t_f32,
                      w_exp_gate[0], w_exp_up[0], w_exp_down[0])
    out = _combine_call(dest_c, y, gk, t_f32, w_sh_gate[0].astype(BF16), w_sh_up[0].astype(BF16),
                        w_sh_down[0].astype(BF16), x1, row2(g_post_ffn[0]), mod3, seq)
    return out.reshape(batch, seq, d)
```
